```python
import math
import jax, jax.numpy as jnp
from jax import lax
import numpy as np


D_MODEL = 1024
BATCH = 8
SEQ = 4096
DEPTH = 4

GRID_W = 64
CTX_LEN = 256
D_MIX = D_MODEL
BRANCH_W = D_MIX // 2
A_HEADS = 8
A_QK = BRANCH_W // (2 * A_HEADS)
A_V = 2 * A_QK
CONV_W = 3
C_HEADS = 8
C_HEAD_DIM = BRANCH_W // C_HEADS
NA_KH = 8
NA_KW = 16
D_GROUPS = 8
D_GROUP_W = BRANCH_W // D_GROUPS
Q_BLOCK = 128
ROPE_BASE = 10000.0
EPS = 1e-6
N_EVEN = (DEPTH + 1) // 2
N_ODD = DEPTH // 2
EVEN_IN = 8 * BRANCH_W
ODD_IN = 6 * BRANCH_W

kernel_name = 'hybrid_diffusion_backbone'


def rms_norm(x, g):
    xf = x.astype(jnp.float32)
    y = xf * lax.rsqrt(jnp.mean(xf * xf, axis=-1, keepdims=True) + EPS)
    return (y * g.astype(jnp.float32)).astype(x.dtype)


def lambda_init(layer):
    return 0.8 - 0.6 * math.exp(-0.3 * layer)


def axial_rope(x, row, col):
    half = x.shape[-1] // 2
    nf = half // 2
    inv_freq = ROPE_BASE ** (-jnp.arange(nf, dtype=jnp.float32) / nf)

    def rotate(u, pos):
        ang = pos.astype(jnp.float32)[:, None] * inv_freq
        cos = jnp.cos(ang)[None, :, None, None, :]
        sin = jnp.sin(ang)[None, :, None, None, :]
        u1, u2 = u[..., :nf], u[..., nf:]
        return jnp.concatenate([u1 * cos - u2 * sin, u2 * cos + u1 * sin], axis=-1)

    xf = x.astype(jnp.float32)
    y = jnp.concatenate([rotate(xf[..., :half], row), rotate(xf[..., half:], col)], axis=-1)
    return y.astype(x.dtype)


def diff_attend(q, k, v, lam):
    s = jnp.einsum('bqhmd,bkhmd->bhmqk', q, k).astype(jnp.float32) * (A_QK ** -0.5)
    p = jax.nn.softmax(s, axis=-1)
    a = p[:, :, 0] - lam * p[:, :, 1]
    return jnp.einsum('bhqk,bkhd->bqhd', a, v.astype(jnp.float32))


def diff_attention_post(o, subln, lam_init):
    B, N = o.shape[:2]
    return (rms_norm(o, subln) * (1.0 - lam_init)).reshape(B, N, BRANCH_W)


def short_conv(b, c, u, w):
    z = c * u
    y = lax.conv_general_dilated(z, w.reshape(CONV_W, 1, BRANCH_W).astype(z.dtype),
                                 window_strides=(1,), padding='SAME',
                                 dimension_numbers=('NWC', 'WIO', 'NWC'),
                                 feature_group_count=BRANCH_W)
    return b * y


def fourier_mix(u):
    B, N, _ = u.shape
    ug = u.astype(jnp.float32).reshape(B, N, D_GROUPS, D_GROUP_W)
    y = jnp.fft.fft2(ug, axes=(1, 3), norm='ortho').real
    return y.reshape(B, N, BRANCH_W).astype(u.dtype)


def dense_attend(q, k, v):
    s = jnp.einsum('bqhd,bkhd->bhqk', q, k).astype(jnp.float32) * (C_HEAD_DIM ** -0.5)
    p = jax.nn.softmax(s, axis=-1)
    return jnp.einsum('bhqk,bkhd->bqhd', p, v.astype(jnp.float32))


def neighbourhood_attention(q, k, v, k_c, v_c, rpb):
    B, S, H, d = q.shape
    rows = S // GRID_W
    kh = min(NA_KH, rows)
    kw = NA_KW
    r = jnp.arange(rows)
    cq = jnp.arange(GRID_W)
    rs = jnp.clip(r - kh // 2, 0, rows - kh)
    cs = jnp.clip(cq - kw // 2, 0, GRID_W - kw)
    col_idx = cs[:, None] + jnp.arange(kw)[None, :]
    row_off = rs[:, None] + jnp.arange(kh)[None, :] - r[:, None] + (NA_KH - 1)
    col_off = col_idx - cq[:, None] + (NA_KW - 1)
    scale = d ** -0.5
    qg = jnp.moveaxis(q.reshape(B, rows, GRID_W, H, d), 1, 0)
    kg = k.reshape(B, rows, GRID_W, H, d)
    vg = v.reshape(B, rows, GRID_W, H, d)
    s_c_all = None

    def one_row(args):
        q_r, r0, roff = args
        kb = lax.dynamic_slice_in_dim(kg, r0, kh, axis=1)
        vb = lax.dynamic_slice_in_dim(vg, r0, kh, axis=1)
        kwin = kb[:, :, col_idx]
        vwin = vb[:, :, col_idx]
        bias = rpb[:, roff][:, :, col_off]
        s_w = jnp.einsum('bqhd,biqjhd->bhqij', q_r, kwin).astype(jnp.float32) * scale
        s_w = s_w + jnp.transpose(bias, (0, 2, 1, 3))[None].astype(jnp.float32)
        s_w = s_w.reshape(B, H, GRID_W, kh * kw)
        s_c = jnp.einsum('bqhd,bkhd->bhqk', q_r, k_c).astype(jnp.float32) * scale
        p = jax.nn.softmax(jnp.concatenate([s_w, s_c], axis=-1), axis=-1)
        p_w = p[..., :kh * kw].reshape(B, H, GRID_W, kh, kw)
        p_c = p[..., kh * kw:]
        return (jnp.einsum('bhqij,biqjhd->bqhd', p_w, vwin.astype(jnp.float32))
                + jnp.einsum('bhqk,bkhd->bqhd', p_c, v_c.astype(jnp.float32)))

    o = lax.map(one_row, (qg, rs, row_off))
    return jnp.moveaxis(o, 0, 1).reshape(B, S, H * d)


def even_mixer(px, pc, lam_p, subln, conv_w, lam_init, row, col, need_ctx):
    B, S, _ = px.shape
    L = pc.shape[1]
    qa, ka, va, ga, bb, cb, ub, gb = jnp.split(px, 8, axis=-1)
    qa = axial_rope(qa.reshape(B, S, A_HEADS, 2, A_QK), row, col)
    ka = axial_rope(ka.reshape(B, S, A_HEADS, 2, A_QK), row, col)
    va = va.reshape(B, S, A_HEADS, A_V)
    if need_ctx:
        qc, kc, vc, gc, bbc, cbc, ubc, gbc = jnp.split(pc, 8, axis=-1)
    else:
        kc, vc = jnp.split(pc, 2, axis=-1)
    kc = kc.reshape(B, L, A_HEADS, 2, A_QK)
    vc = vc.reshape(B, L, A_HEADS, A_V)
    lp = lam_p.astype(jnp.float32)
    lam = jnp.exp(jnp.sum(lp[0] * lp[1])) - jnp.exp(jnp.sum(lp[2] * lp[3])) + lam_init
    k_all = jnp.concatenate([kc, ka], axis=1)
    v_all = jnp.concatenate([vc, va], axis=1)
    nb = S // Q_BLOCK
    q_blocks = jnp.moveaxis(qa.reshape(B, nb, Q_BLOCK, A_HEADS, 2, A_QK), 1, 0)
    o = lax.map(lambda qb: diff_attend(qb, k_all, v_all, lam), q_blocks)
    o = jnp.moveaxis(o, 0, 1).reshape(B, S, A_HEADS, A_V)
    out_a = diff_attention_post(o, subln, lam_init).astype(px.dtype) * jax.nn.silu(ga)
    out_b = short_conv(bb, cb, ub, conv_w) * jax.nn.silu(gb)
    out_x = jnp.concatenate([out_a, out_b], axis=-1)
    if not need_ctx:
        return out_x, None
    oc = diff_attend(qc.reshape(B, L, A_HEADS, 2, A_QK), kc, vc, lam)
    out_ac = diff_attention_post(oc, subln, lam_init).astype(pc.dtype) * jax.nn.silu(gc)
    out_bc = short_conv(bbc, cbc, ubc, conv_w) * jax.nn.silu(gbc)
    return out_x, jnp.concatenate([out_ac, out_bc], axis=-1)


def odd_mixer(px, pc, rpb, need_ctx):
    B, S, _ = px.shape
    L = pc.shape[1]

    def heads(u):
        return u.reshape(u.shape[0], u.shape[1], C_HEADS, C_HEAD_DIM)

    qn, kn, vn, gn, ud, gd = jnp.split(px, 6, axis=-1)
    if need_ctx:
        qc, kc, vc, gc, udc, gdc = jnp.split(pc, 6, axis=-1)
    else:
        kc, vc = jnp.split(pc, 2, axis=-1)
    kc, vc = heads(kc), heads(vc)
    o_n = neighbourhood_attention(heads(qn), heads(kn), heads(vn), kc, vc, rpb)
    out_x = jnp.concatenate([o_n.astype(px.dtype) * jax.nn.silu(gn),
                             fourier_mix(ud) * jax.nn.silu(gd)], axis=-1)
    if not need_ctx:
        return out_x, None
    o_nc = dense_attend(heads(qc), kc, vc).reshape(B, L, BRANCH_W)
    out_c = jnp.concatenate([o_nc.astype(pc.dtype) * jax.nn.silu(gc),
                             fourier_mix(udc) * jax.nn.silu(gdc)], axis=-1)
    return out_x, out_c


def setup_inputs(seed: int = 0) -> dict:
    key = jax.random.key(seed)
    ks = jax.random.split(key, 15)
    f = jnp.float32
    nrm = jax.random.normal
    return {
        'x': nrm(ks[0], (BATCH, SEQ, D_MODEL), f),
        'c': nrm(ks[1], (BATCH, D_MODEL), f),
        'ctx': nrm(ks[2], (BATCH, CTX_LEN, D_MODEL), f),
        'c_ctx': nrm(ks[3], (D_MODEL,), f),
        'w_mod': nrm(ks[4], (DEPTH, D_MODEL, 3 * D_MODEL), f) * D_MODEL ** -0.5,
        'b_mod': 0.02 * nrm(ks[5], (DEPTH, 3 * D_MODEL), f),
        'norm_pre': 1.0 + 0.1 * nrm(ks[6], (DEPTH, D_MODEL), f),
        'norm_post': 1.0 + 0.1 * nrm(ks[7], (DEPTH, D_MODEL), f),
        'w_in_even': nrm(ks[8], (N_EVEN, D_MODEL, EVEN_IN), f) * D_MODEL ** -0.5,
        'lam_a': 0.1 * nrm(ks[9], (N_EVEN, 4, A_QK), f),
        'subln_a': 1.0 + 0.1 * nrm(ks[10], (N_EVEN, A_V), f),
        'conv_b': nrm(ks[11], (N_EVEN, CONV_W, BRANCH_W), f) * CONV_W ** -0.5,
        'w_in_odd': nrm(ks[12], (N_ODD, D_MODEL, ODD_IN), f) * D_MODEL ** -0.5,
        'rpb_c': 0.1 * nrm(ks[13], (N_ODD, C_HEADS, 2 * NA_KH - 1, 2 * NA_KW - 1), f),
        'w_out': nrm(ks[14], (DEPTH, D_MIX, D_MODEL), f) * D_MIX ** -0.5,
    }


def reference(x, c, ctx, c_ctx, w_mod, b_mod, norm_pre, norm_post, w_in_even, lam_a,
              subln_a, conv_b, w_in_odd, rpb_c, w_out):
    S = x.shape[1]
    t = jnp.arange(S)
    row = t // GRID_W
    col = t % GRID_W
    silu_c = jax.nn.silu(c)
    silu_cc = jax.nn.silu(c_ctx)
    for l in range(DEPTH):
        need_ctx = l < DEPTH - 1
        j = l // 2
        mod_x = (silu_c @ w_mod[l] + b_mod[l])[:, None, :]
        mod_c = (silu_cc @ w_mod[l] + b_mod[l])[None, None, :]
        sh_x, sc_x, g_x = jnp.split(mod_x, 3, axis=-1)
        sh_c, sc_c, g_c = jnp.split(mod_c, 3, axis=-1)
        hx = rms_norm(x, norm_pre[l]) * (1.0 + sc_x) + sh_x
        hc = rms_norm(ctx, norm_pre[l]) * (1.0 + sc_c) + sh_c
        w_in = w_in_even[j] if l % 2 == 0 else w_in_odd[j]
        px = hx @ w_in
        pc = hc @ (w_in if need_ctx else w_in[:, BRANCH_W:3 * BRANCH_W])
        if l % 2 == 0:
            mx, mc = even_mixer(px, pc, lam_a[j], subln_a[j], conv_b[j], lambda_init(l),
                                row, col, need_ctx)
        else:
            mx, mc = odd_mixer(px, pc, rpb_c[j], need_ctx)
        x = x + g_x * rms_norm(mx @ w_out[l], norm_post[l])
        if need_ctx:
            ctx = ctx + g_c * rms_norm(mc @ w_out[l], norm_post[l])
    return x
```

```python
import functools
import math

import numpy as np
import jax
import jax.numpy as jnp
from jax import lax
from jax.experimental import pallas as pl
from jax.experimental.pallas import tpu as pltpu

GRID_W = 64
A_HEADS = 8
A_QK = 32
A_V = 64
C_HEADS = 8
C_HEAD_DIM = 64
NA_KH = 8
NA_KW = 16
D_GROUPS = 8
D_GROUP_W = 64
CONV_W = 3
ROPE_BASE = 10000.0
EPS = 1e-6
LOG2E = 1.4426950408889634
NEG_BIG = -1e30

LANES = 128
HEAD_PAIR_W = 128
KEY_CHUNK = 256
VMEM_LIMIT = 48 * 1024 * 1024

F32 = jnp.float32
BF16 = jnp.bfloat16


def _silu(v):
    return v * jax.nn.sigmoid(v)


def _params(n_axes):
    return pltpu.CompilerParams(dimension_semantics=("arbitrary",) * n_axes,
                                vmem_limit_bytes=VMEM_LIMIT)


def _nt_dot(a, b):
    return lax.dot_general(a, b, (((1,), (1,)), ((), ())), preferred_element_type=F32)


def _mod_kernel(cc_ref, w_ref, b_ref, o_ref):
    s = _silu(cc_ref[...]).astype(BF16)
    o_ref[0] = jnp.dot(s, w_ref[0].astype(BF16), preferred_element_type=F32) + b_ref[0]


def _mod_all(cc, w_mod, b_mod):
    depth, d, d3 = w_mod.shape
    tn = d
    return pl.pallas_call(
        _mod_kernel,
        grid=(depth, d3 // tn),
        in_specs=[pl.BlockSpec((cc.shape[0], d), lambda l, n: (0, 0)),
                  pl.BlockSpec((1, d, tn), lambda l, n: (l, 0, n)),
                  pl.BlockSpec((1, 1, tn), lambda l, n: (l, 0, n))],
        out_specs=pl.BlockSpec((1, cc.shape[0], tn), lambda l, n: (l, 0, n)),
        out_shape=jax.ShapeDtypeStruct((depth, cc.shape[0], d3), F32),
        compiler_params=_params(2), name="adaln_mod",
    )(cc, w_mod, b_mod.reshape(depth, 1, d3))


def _prenorm(x_ref, mod_ref, g_ref):
    d = x_ref.shape[-1]
    x = x_ref[...]
    y = x * lax.rsqrt(jnp.mean(x * x, axis=-1, keepdims=True) + EPS) * g_ref[...]
    m = mod_ref[0]
    h = y * (1.0 + m[:, d:2 * d]) + m[:, 0:d]
    return h.astype(BF16)


def _inproj_even_kernel(*refs, rope, tm):
    if rope:
        (x_ref, mod_ref, g_ref, w_ref, wqt_ref, wvt_ref, wqts_ref, wks_ref,
         cos_ref, sin_ref, cost_ref, sint_ref, qt_ref, k_ref, vt_ref, rest_ref) = refs
    else:
        (x_ref, mod_ref, g_ref, w_ref, wqt_ref, wvt_ref, qt_ref, k_ref, vt_ref, rest_ref) = refs
    bw = qt_ref.shape[1]
    hb = _prenorm(x_ref, mod_ref, g_ref)

    qt = _nt_dot(wqt_ref[...], hb)
    if rope:
        reps = bw // cost_ref.shape[0]
        qt = (qt * jnp.tile(cost_ref[...], (reps, 1))
              + _nt_dot(wqts_ref[...], hb) * jnp.tile(sint_ref[...], (reps, 1)))
    qt_ref[0] = (qt * (A_QK ** -0.5 * LOG2E)).astype(BF16)

    k = jnp.dot(hb, w_ref[:, bw:2 * bw], preferred_element_type=F32)
    if rope:
        reps = bw // cos_ref.shape[1]
        k = (k * jnp.tile(cos_ref[...], (1, reps))
             + jnp.dot(hb, wks_ref[...], preferred_element_type=F32) * jnp.tile(sin_ref[...], (1, reps)))
    k_ref[0] = k.astype(BF16)

    vt = _nt_dot(wvt_ref[...], hb).astype(BF16)
    for c in range(tm // KEY_CHUNK):
        vt_ref[0, c] = vt[:, c * KEY_CHUNK:(c + 1) * KEY_CHUNK]

    n_rest = rest_ref.shape[1] // bw
    for c in range(n_rest):
        col = (3 + c) * bw
        rest_ref[:, c * bw:(c + 1) * bw] = jnp.dot(
            hb, w_ref[:, col:col + bw], preferred_element_type=F32).astype(BF16)


def _inproj_even(x2, mod3, mod_row_fn, g_pre, w_bf, seq, rope_tabs, tm):
    n, d = x2.shape
    bw = w_bf.shape[1] // 8
    nb = n // seq
    tps = seq // tm
    rope = rope_tabs is not None
    wqt = w_bf[:, 0:bw].T
    wvt = w_bf[:, 2 * bw:3 * bw].T
    const = lambda i: (0, 0)
    in_specs = [pl.BlockSpec((tm, d), lambda i: (i, 0)),
                pl.BlockSpec((1, 1, mod3.shape[2]), lambda i: (mod_row_fn(i // tps), 0, 0)),
                pl.BlockSpec((1, d), const),
                pl.BlockSpec(w_bf.shape, const),
                pl.BlockSpec(wqt.shape, const),
                pl.BlockSpec(wvt.shape, const)]
    args = [x2, mod3, g_pre.reshape(1, d), w_bf, wqt, wvt]
    if rope:
        cos_n, sin_n, cos_t, sin_t = rope_tabs
        swap = np.arange(bw) ^ 8
        wqts = w_bf[:, 0:bw][:, swap].T
        wks = w_bf[:, bw:2 * bw][:, swap]
        in_specs += [pl.BlockSpec(wqts.shape, const),
                     pl.BlockSpec(wks.shape, const),
                     pl.BlockSpec((tm, cos_n.shape[1]), lambda i: (i % tps, 0)),
                     pl.BlockSpec((tm, cos_n.shape[1]), lambda i: (i % tps, 0)),
                     pl.BlockSpec((cos_t.shape[0], tm), lambda i: (0, i % tps)),
                     pl.BlockSpec((cos_t.shape[0], tm), lambda i: (0, i % tps))]
        args += [wqts, wks, cos_n, sin_n, cos_t, sin_t]
    cpt = tm // KEY_CHUNK
    out_specs = [pl.BlockSpec((1, bw, tm), lambda i: (i // tps, 0, i % tps)),
                 pl.BlockSpec((1, tm, bw), lambda i: (i // tps, i % tps, 0)),
                 pl.BlockSpec((1, cpt, bw, KEY_CHUNK), lambda i: (i // tps, i % tps, 0, 0)),
                 pl.BlockSpec((tm, 5 * bw), lambda i: (i, 0))]
    out_shape = [jax.ShapeDtypeStruct((nb, bw, seq), BF16),
                 jax.ShapeDtypeStruct((nb, seq, bw), BF16),
                 jax.ShapeDtypeStruct((nb, seq // KEY_CHUNK, bw, KEY_CHUNK), BF16),
                 jax.ShapeDtypeStruct((n, 5 * bw), BF16)]
    return pl.pallas_call(
        functools.partial(_inproj_even_kernel, rope=rope, tm=tm),
        grid=(n // tm,), in_specs=in_specs, out_specs=out_specs, out_shape=out_shape,
        compiler_params=_params(1), name="inproj_even",
    )(*args)


def _inproj_odd_kernel(x_ref, mod_ref, g_ref, w_ref, cs_ref, q_ref, k_ref, v_ref, rest_ref, p_ref, pq_ref):
    bw = q_ref.shape[2]
    hb = _prenorm(x_ref, mod_ref, g_ref)

    def proj(c):
        return jnp.dot(hb, w_ref[:, c * bw:(c + 1) * bw], preferred_element_type=F32)

    q_ref[0] = (proj(0) * (C_HEAD_DIM ** -0.5 * LOG2E)).astype(BF16)
    k_ref[0] = proj(1).astype(BF16)
    v_ref[0] = proj(2).astype(BF16)
    rest_ref[:, 0:bw] = proj(3).astype(BF16)
    rest_ref[:, bw:2 * bw] = proj(5).astype(BF16)
    pq = jnp.dot(proj(4).astype(BF16), cs_ref[...], preferred_element_type=F32)
    p_ref[0] = pq[:, 0:bw].astype(BF16)
    pq_ref[0] = pq[:, bw:2 * bw].astype(BF16)


def _inproj_odd(x2, mod3, mod_row_fn, g_pre, w_bf, cs_bf, seq, tm):
    n, d = x2.shape
    bw = w_bf.shape[1] // 6
    nb = n // seq
    tps = seq // tm
    const = lambda i: (0, 0)
    tok = pl.BlockSpec((1, tm, bw), lambda i: (i // tps, i % tps, 0))
    tok_shape = jax.ShapeDtypeStruct((nb, seq, bw), BF16)
    return pl.pallas_call(
        _inproj_odd_kernel,
        grid=(n // tm,),
        in_specs=[pl.BlockSpec((tm, d), lambda i: (i, 0)),
                  pl.BlockSpec((1, 1, mod3.shape[2]), lambda i: (mod_row_fn(i // tps), 0, 0)),
                  pl.BlockSpec((1, d), const),
                  pl.BlockSpec(w_bf.shape, const),
                  pl.BlockSpec(cs_bf.shape, const)],
        out_specs=[tok, tok, tok, pl.BlockSpec((tm, 2 * bw), lambda i: (i, 0)), tok, tok],
        out_shape=[tok_shape, tok_shape, tok_shape, jax.ShapeDtypeStruct((n, 2 * bw), BF16),
                   tok_shape, tok_shape],
        compiler_params=_params(1), name="inproj_odd",
    )(x2, mod3, g_pre.reshape(1, d), w_bf, cs_bf)


def _diff_attn_kernel(*refs, n_ctx, n_main, tq, lam_init):
    if n_main:
        (lam_ref, sub_ref, qt_ref, kc_ref, vct_ref, ka_ref, vat_ref, ga_ref, o_ref, sbuf) = refs
    else:
        (lam_ref, sub_ref, qt_ref, kc_ref, vct_ref, ga_ref, o_ref, sbuf) = refs
    kc = KEY_CHUNK
    lp = lam_ref[...]
    lam = (jnp.exp(jnp.sum(lp[0:1] * lp[1:2], axis=1, keepdims=True))
           - jnp.exp(jnp.sum(lp[2:3] * lp[3:4], axis=1, keepdims=True)) + lam_init)
    qf = qt_ref[0].astype(F32)
    rowid = lax.broadcasted_iota(jnp.int32, qf.shape, 0)
    sub = jnp.tile(sub_ref[...], (1, tq // LANES))

    def fold(v):
        return v.reshape(kc // 8, 8, tq)

    heads = []
    for hh in range(2):
        maps = []
        for m in range(2):
            lo = hh * A_V + m * A_QK
            qm = jnp.where((rowid >= lo) & (rowid < lo + A_QK), qf, 0.0).astype(BF16)

            def scores(kchunk, idx, mx):
                s = jnp.dot(kchunk, qm, preferred_element_type=F32)
                sbuf[idx] = s
                return jnp.maximum(mx, jnp.max(fold(s), axis=0))

            mx = jnp.full((8, tq), -jnp.inf, F32)
            for c in range(n_ctx):
                mx = scores(kc_ref[0, c * kc:(c + 1) * kc, :], c, mx)
            if n_main:
                mx = lax.fori_loop(
                    0, n_main,
                    lambda c, v: scores(ka_ref[0, pl.ds(pl.multiple_of(c * kc, kc), kc), :], n_ctx + c, v),
                    mx)
            mrow = jnp.max(mx, axis=0, keepdims=True)

            def weigh(vchunk, idx, carry):
                acc, l = carry
                e = jnp.exp2(sbuf[idx] - mrow)
                l = l + jnp.sum(fold(e), axis=0)
                acc = acc + jnp.dot(vchunk, e.astype(BF16), preferred_element_type=F32)
                return acc, l

            carry = (jnp.zeros((A_V, tq), F32), jnp.zeros((8, tq), F32))
            for c in range(n_ctx):
                carry = weigh(vct_ref[0, c, hh * A_V:(hh + 1) * A_V, :], c, carry)
            if n_main:
                carry = lax.fori_loop(
                    0, n_main,
                    lambda c, v: weigh(vat_ref[0, c, hh * A_V:(hh + 1) * A_V, :], n_ctx + c, v),
                    carry)
            acc, l = carry
            maps.append(acc * (1.0 / jnp.sum(l, axis=0, keepdims=True)))
        ot = maps[0] - lam * maps[1]
        ot = ot * lax.rsqrt(jnp.mean(ot * ot, axis=0, keepdims=True) + EPS) * sub * (1.0 - lam_init)
        heads.append(ot)
    o = jnp.concatenate(heads, axis=0).T
    o_ref[...] = o * _silu(ga_ref[...].astype(F32))


def _diff_attn(lam_p, subln, qt, kc, vct, ka, vat, rest, lam_init, tq):
    nb, bw, sq = qt.shape
    nq = sq // tq
    n_ctx = kc.shape[1] // KEY_CHUNK
    n_main = 0 if ka is None else ka.shape[1] // KEY_CHUNK
    hp = HEAD_PAIR_W
    sub_b = jnp.broadcast_to(subln.reshape(A_V, 1), (A_V, LANES))
    in_specs = [pl.BlockSpec(lam_p.shape, lambda b, h, i: (0, 0)),
                pl.BlockSpec(sub_b.shape, lambda b, h, i: (0, 0)),
                pl.BlockSpec((1, hp, tq), lambda b, h, i: (b, h, i)),
                pl.BlockSpec((1, kc.shape[1], hp), lambda b, h, i: (b, 0, h)),
                pl.BlockSpec((1, n_ctx, hp, KEY_CHUNK), lambda b, h, i: (b, 0, h, 0))]
    args = [lam_p, sub_b, qt, kc, vct]
    if n_main:
        in_specs += [pl.BlockSpec((1, ka.shape[1], hp), lambda b, h, i: (b, 0, h)),
                     pl.BlockSpec((1, n_main, hp, KEY_CHUNK), lambda b, h, i: (b, 0, h, 0))]
        args += [ka, vat]
    in_specs.append(pl.BlockSpec((tq, hp), lambda b, h, i: (b * nq + i, h)))
    args.append(rest)
    return pl.pallas_call(
        functools.partial(_diff_attn_kernel, n_ctx=n_ctx, n_main=n_main, tq=tq, lam_init=lam_init),
        grid=(nb, bw // hp, nq),
        in_specs=in_specs,
        out_specs=pl.BlockSpec((tq, hp), lambda b, h, i: (b * nq + i, h)),
        out_shape=jax.ShapeDtypeStruct((nb * sq, bw), F32),
        scratch_shapes=[pltpu.VMEM((n_ctx + n_main, KEY_CHUNK, tq), F32)],
        compiler_params=_params(3), name="diff_attn",
    )(*args)


NA_QROWS = 4
NA_KROWS = 12


def _softmax_pv(q, pieces, gate_ref, o_ref):
    lane = lax.broadcasted_iota(jnp.int32, q.shape, 1)
    qf = q.astype(F32)
    outs = []
    for hh in range(2):
        qm = jnp.where((lane >= hh * C_HEAD_DIM) & (lane < (hh + 1) * C_HEAD_DIM), qf, 0.0).astype(BF16)
        ss = []
        for k, _, bias in pieces:
            s = _nt_dot(qm, k)
            if bias is not None:
                s = s + bias[hh]
            ss.append(s)
        m = functools.reduce(jnp.maximum, [jnp.max(s, axis=1, keepdims=True) for s in ss])
        es = [jnp.exp2(s - m) for s in ss]
        l = functools.reduce(jnp.add, [jnp.sum(e, axis=1, keepdims=True) for e in es])
        pv = functools.reduce(jnp.add, [jnp.dot(e.astype(BF16), v, preferred_element_type=F32)
                                        for e, (_, v, _) in zip(es, pieces)])
        outs.append(pv * (1.0 / l))
    o = jnp.where(lane < C_HEAD_DIM, outs[0], outs[1])
    o_ref[...] = o * _silu(gate_ref[...].astype(F32))


def _na_kernel(q_ref, k_ref, v_ref, kc_ref, vc_ref, bias_ref, gate_ref, o_ref, *, rows):
    g = pl.program_id(2)
    base = jnp.clip(NA_QROWS * g - NA_KH // 2, 0, rows - NA_KROWS)
    start = pl.multiple_of(base * GRID_W, NA_QROWS * GRID_W)
    nk = NA_KROWS * GRID_W
    kw = k_ref[0, pl.ds(start, nk), :]
    vw = v_ref[0, pl.ds(start, nk), :]

    class _Bias:
        def __getitem__(self, hh):
            return bias_ref[0, hh]

    _softmax_pv(q_ref[0], [(kw, vw, _Bias()), (kc_ref[0], vc_ref[0], None)], gate_ref, o_ref)


def _na_case_tables(rows):
    rs = np.clip(np.arange(rows) - NA_KH // 2, 0, rows - NA_KH)
    n_groups = rows // NA_QROWS
    pats = []
    for g in range(n_groups):
        base = int(np.clip(NA_QROWS * g - NA_KH // 2, 0, rows - NA_KROWS))
        pat = np.full((NA_QROWS, NA_KROWS), 2 * NA_KH - 1, np.int32)
        for j in range(NA_QROWS):
            r = NA_QROWS * g + j
            for i in range(NA_KROWS):
                kr = base + i
                if rs[r] <= kr < rs[r] + NA_KH:
                    pat[j, i] = kr - r + NA_KH - 1
            assert (pat[j] != 2 * NA_KH - 1).sum() == NA_KH
        pats.append(pat)
    cases = np.stack([pats[0], pats[1], pats[-1]])
    for g in range(n_groups):
        want = 0 if g == 0 else (2 if g == n_groups - 1 else 1)
        assert (pats[g] == cases[want]).all()
    return cases


def _na_bias(rpb, rows):
    h = rpb.shape[0]
    cq = np.arange(GRID_W)
    cs = np.clip(cq - NA_KW // 2, 0, GRID_W - NA_KW)
    kcol = np.arange(GRID_W)
    col_ok = (kcol[None, :] >= cs[:, None]) & (kcol[None, :] < cs[:, None] + NA_KW)
    col_off = np.clip(kcol[None, :] - cq[:, None] + NA_KW - 1, 0, 2 * NA_KW - 2)
    toep = jnp.where(col_ok[None, None], rpb[:, :, col_off] * LOG2E, NEG_BIG)
    toep = jnp.concatenate([toep, jnp.full((h, 1, GRID_W, GRID_W), NEG_BIG, F32)], axis=1)
    cases = _na_case_tables(rows)
    t = toep[:, cases]
    t = jnp.transpose(t, (1, 0, 2, 4, 3, 5))
    return t.reshape(3, h, NA_QROWS * GRID_W, NA_KROWS * GRID_W)


def _na_attn(q, k, v, kc, vc, bias, rest):
    nb, s, bw = q.shape
    hp = HEAD_PAIR_W
    tq = NA_QROWS * GRID_W
    ng = s // tq
    l = kc.shape[1]
    case = lambda g: jnp.where(g == 0, 0, jnp.where(g == ng - 1, 2, 1))
    return pl.pallas_call(
        functools.partial(_na_kernel, rows=s // GRID_W),
        grid=(nb, bw // hp, ng),
        in_specs=[pl.BlockSpec((1, tq, hp), lambda b, h, g: (b, g, h)),
                  pl.BlockSpec((1, s, hp), lambda b, h, g: (b, 0, h)),
                  pl.BlockSpec((1, s, hp), lambda b, h, g: (b, 0, h)),
                  pl.BlockSpec((1, l, hp), lambda b, h, g: (b, 0, h)),
                  pl.BlockSpec((1, l, hp), lambda b, h, g: (b, 0, h)),
                  pl.BlockSpec((1, 2, tq, NA_KROWS * GRID_W), lambda b, h, g: (case(g), h, 0, 0)),
                  pl.BlockSpec((tq, hp), lambda b, h, g: (b * ng + g, h))],
        out_specs=pl.BlockSpec((tq, hp), lambda b, h, g: (b * ng + g, h)),
        out_shape=jax.ShapeDtypeStruct((nb * s, bw), F32),
        compiler_params=_params(3), name="nbr_attn",
    )(q, k, v, kc, vc, bias, rest)


def _dense_attn_kernel(q_ref, k_ref, v_ref, gate_ref, o_ref):
    _softmax_pv(q_ref[0], [(k_ref[0], v_ref[0], None)], gate_ref, o_ref)


def _dense_attn(q, k, v, rest):
    nb, l, bw = q.shape
    hp = HEAD_PAIR_W
    tok = pl.BlockSpec((1, l, hp), lambda b, h: (b, 0, h))
    return pl.pallas_call(
        _dense_attn_kernel,
        grid=(nb, bw // hp),
        in_specs=[tok, tok, tok, pl.BlockSpec((l, hp), lambda b, h: (b, h))],
        out_specs=pl.BlockSpec((l, hp), lambda b, h: (b, h)),
        out_shape=jax.ShapeDtypeStruct((nb * l, bw), F32),
        compiler_params=_params(2), name="ctx_dense_attn",
    )(q, k, v, rest)


def _dft_kernel(c_ref, s_ref, p_ref, q_ref, o_ref):
    o_ref[0] = (jnp.dot(c_ref[...], p_ref[0], preferred_element_type=F32)
                + jnp.dot(s_ref[...], q_ref[0], preferred_element_type=F32))


def _dft_tables(n):
    idx = jnp.arange(n, dtype=jnp.int32)
    ang = ((idx[:, None] * idx[None, :]) % n).astype(F32) * (2.0 * math.pi / n)
    scale = n ** -0.5
    return (jnp.cos(ang) * scale).astype(BF16), (jnp.sin(ang) * scale).astype(BF16)


def _dft_positions(cn, sn, p, q, tmo):
    nb, n, bw = p.shape
    return pl.pallas_call(
        _dft_kernel,
        grid=(n // tmo, nb),
        in_specs=[pl.BlockSpec((tmo, n), lambda i, b: (i, 0)),
                  pl.BlockSpec((tmo, n), lambda i, b: (i, 0)),
                  pl.BlockSpec((1, n, bw), lambda i, b: (b, 0, 0)),
                  pl.BlockSpec((1, n, bw), lambda i, b: (b, 0, 0))],
        out_specs=pl.BlockSpec((1, tmo, bw), lambda i, b: (b, i, 0)),
        out_shape=jax.ShapeDtypeStruct((nb, n, bw), F32),
        compiler_params=_params(2), name="dft_positions",
    )(cn, sn, p, q)


def _channel_dft_matrix():
    j = np.arange(D_GROUP_W)
    ang = 2.0 * np.pi * ((j[:, None] * j[None, :]) % D_GROUP_W) / D_GROUP_W
    eye = np.eye(D_GROUPS)
    cc = np.kron(eye, np.cos(ang) * D_GROUP_W ** -0.5)
    sc = np.kron(eye, np.sin(ang) * D_GROUP_W ** -0.5)
    return jnp.asarray(np.concatenate([cc, -sc], axis=1), BF16)


def _post_kernel(*refs, even, tm, seq):
    if even:
        (x_ref, a_ref, bb_ref, cb_ref, ub_ref, gb_ref, cbp_ref, ubp_ref, cbn_ref, ubn_ref,
         cw_ref, mod_ref, g_ref, w_ref, o_ref) = refs
    else:
        (x_ref, a_ref, y_ref, gd_ref, mod_ref, g_ref, w_ref, o_ref) = refs
    d = x_ref.shape[-1]
    bw = a_ref.shape[-1]
    if even:
        i = pl.program_id(0)
        t0 = (i * tm) % seq
        z = cb_ref[...].astype(F32) * ub_ref[...].astype(F32)
        zp = cbp_ref[7:8, :].astype(F32) * ubp_ref[7:8, :].astype(F32) * jnp.where(t0 == 0, 0.0, 1.0)
        zn = cbn_ref[0:1, :].astype(F32) * ubn_ref[0:1, :].astype(F32) * jnp.where(t0 + tm == seq, 0.0, 1.0)
        rid = lax.broadcasted_iota(jnp.int32, z.shape, 0)
        z_prev = jnp.where(rid == 0, zp, pltpu.roll(z, 1, 0))
        z_next = jnp.where(rid == tm - 1, zn, pltpu.roll(z, tm - 1, 0))
        cw = cw_ref[...]
        y = cw[0:1] * z_prev + cw[1:2] * z + cw[2:3] * z_next
        second = bb_ref[...].astype(F32) * y * _silu(gb_ref[...].astype(F32))
    else:
        second = y_ref[...] * _silu(gd_ref[...].astype(F32))
    yo = (jnp.dot(a_ref[...].astype(BF16), w_ref[0:bw, :], preferred_element_type=F32)
          + jnp.dot(second.astype(BF16), w_ref[bw:2 * bw, :], preferred_element_type=F32))
    nrm = yo * lax.rsqrt(jnp.mean(yo * yo, axis=-1, keepdims=True) + EPS) * g_ref[...]
    o_ref[...] = x_ref[...] + mod_ref[0][:, 2 * d:3 * d] * nrm


def _post(x2, a, second, rest, conv_w, mod3, mod_row_fn, g_post, w_out_bf, seq, tm, even):
    n, d = x2.shape
    bw = a.shape[1]
    tps = seq // tm
    const = lambda i: (0, 0)
    row = lambda i: (i, 0)
    in_specs = [pl.BlockSpec((tm, d), row), pl.BlockSpec((tm, bw), row)]
    args = [x2, a]
    if even:
        nblk8 = n // 8
        for c in (1, 2, 3, 4):
            in_specs.append(pl.BlockSpec((tm, bw), lambda i, c=c: (i, c)))
            args.append(rest)
        prev = lambda i, c: (jnp.maximum(i * (tm // 8) - 1, 0), c)
        nxt = lambda i, c: (jnp.minimum((i + 1) * (tm // 8), nblk8 - 1), c)
        for fn in (prev, nxt):
            for c in (2, 3):
                in_specs.append(pl.BlockSpec((8, bw), functools.partial(fn, c=c)))
                args.append(rest)
        in_specs.append(pl.BlockSpec(conv_w.shape, const))
        args.append(conv_w)
    else:
        in_specs += [pl.BlockSpec((tm, bw), row), pl.BlockSpec((tm, bw), lambda i: (i, 1))]
        args += [second, rest]
    in_specs += [pl.BlockSpec((1, 1, mod3.shape[2]), lambda i: (mod_row_fn(i // tps), 0, 0)),
                 pl.BlockSpec((1, d), const),
                 pl.BlockSpec(w_out_bf.shape, const)]
    args += [mod3, g_post.reshape(1, d), w_out_bf]
    return pl.pallas_call(
        functools.partial(_post_kernel, even=even, tm=tm, seq=seq),
        grid=(n // tm,), in_specs=in_specs,
        out_specs=pl.BlockSpec((tm, d), row),
        out_shape=jax.ShapeDtypeStruct((n, d), F32),
        compiler_params=_params(1), name="post_even" if even else "post_odd",
    )(*args)


def _rope_tables(seq):
    t = jnp.arange(seq)
    row = (t // GRID_W).astype(F32)
    col = (t % GRID_W).astype(F32)
    nf = A_QK // 4
    inv_freq = ROPE_BASE ** (-jnp.arange(nf, dtype=F32) / nf)
    p = np.arange(A_QK)
    f_idx = p % nf
    use_col = (p % A_QK) >= A_QK // 2
    sign = np.where((p % (2 * nf)) < nf, -1.0, 1.0).astype(np.float32)
    pos = jnp.where(use_col[None, :], col[:, None], row[:, None])
    ang = pos * inv_freq[f_idx][None, :]
    cos = jnp.cos(ang)
    sin = jnp.sin(ang) * sign[None, :]
    reps = LANES // A_QK
    return jnp.tile(cos, (1, reps)), jnp.tile(sin, (1, reps)), cos.T, sin.T


def _lambda_init(layer):
    return 0.8 - 0.6 * math.exp(-0.3 * layer)


def kernel(x, c, ctx, c_ctx, w_mod, b_mod, norm_pre, norm_post, w_in_even, lam_a, subln_a, conv_b,
           w_in_odd, rpb_c, w_out):
    nb, seq, d = x.shape
    lctx = ctx.shape[1]
    depth = w_mod.shape[0]
    assert seq % (NA_QROWS * GRID_W) == 0 and seq // GRID_W >= NA_KROWS
    assert lctx % KEY_CHUNK == 0 and seq % KEY_CHUNK == 0 and nb < 16

    pad = jnp.zeros((16 - nb - 1, d), F32)
    cc = jnp.concatenate([c, c_ctx[None, :], pad], axis=0)
    mods = _mod_all(cc, w_mod, b_mod)

    rope_tabs = _rope_tables(seq)
    cs_bf = _channel_dft_matrix()
    cn_x, sn_x = _dft_tables(seq)
    cn_c, sn_c = _dft_tables(lctx)

    tm_x = 512
    tm_c = lctx
    x_row = lambda b: b
    c_row = lambda b: nb

    x2 = x.reshape(nb * seq, d)
    c2 = ctx.reshape(nb * lctx, d)
    for l in range(depth):
        need_ctx = l < depth - 1
        j = l // 2
        mod3 = mods[l].reshape(16, 1, 3 * d)
        w_out_bf = w_out[l].astype(BF16)
        if l % 2 == 0:
            w_bf = w_in_even[j].astype(BF16)
            li = _lambda_init(l)
            qt, k, vt, rest = _inproj_even(x2, mod3, x_row, norm_pre[l], w_bf, seq, rope_tabs, tm_x)
            qct, kc, vct, rest_c = _inproj_even(c2, mod3, c_row, norm_pre[l], w_bf, lctx, None, tm_c)
            a = _diff_attn(lam_a[j], subln_a[j], qt, kc, vct, k, vt, rest, li, 256)
            x2n = _post(x2, a, None, rest, conv_b[j], mod3, x_row, norm_post[l], w_out_bf, seq, tm_x, True)
            if need_ctx:
                ac = _diff_attn(lam_a[j], subln_a[j], qct, kc, vct, None, None, rest_c, li, lctx)
                c2 = _post(c2, ac, None, rest_c, conv_b[j], mod3, c_row, norm_post[l], w_out_bf,
                           lctx, tm_c, True)
            x2 = x2n
        else:
            w_bf = w_in_odd[j].astype(BF16)
            q, k, v, rest, p, pq = _inproj_odd(x2, mod3, x_row, norm_pre[l], w_bf, cs_bf, seq, tm_x)
            qc, kc, vc, rest_c, p_c, pq_c = _inproj_odd(c2, mod3, c_row, norm_pre[l], w_bf, cs_bf, lctx, tm_c)
            bias = _na_bias(rpb_c[j], seq // GRID_W)
            a = _na_attn(q, k, v, kc, vc, bias, rest)
            y = _dft_positions(cn_x, sn_x, p, pq, 512).reshape(nb * seq, -1)
            x2n = _post(x2, a, y, rest, None, mod3, x_row, norm_post[l], w_out_bf, seq, tm_x, False)
            if need_ctx:
                ac = _dense_attn(qc, kc, vc, rest_c)
                yc = _dft_positions(cn_c, sn_c, p_c, pq_c, lctx).reshape(nb * lctx, -1)
                c2 = _post(c2, ac, yc, rest_c, None, mod3, c_row, norm_post[l], w_out_bf, lctx, tm_c, False)
            x2 = x2n
    return x2.reshape(nb, seq, d)
```

```python
import functools
import math

import numpy as np
import jax
import jax.numpy as jnp
from jax import lax
from jax.experimental import pallas as pl
from jax.experimental.pallas import tpu as pltpu

GRID_W = 64
A_HEADS = 8
A_QK = 32
A_V = 64
C_HEADS = 8
C_HEAD_DIM = 64
NA_KH = 8
NA_KW = 16
D_GROUPS = 8
D_GROUP_W = 64
CONV_W = 3
ROPE_BASE = 10000.0
EPS = 1e-6
LOG2E = 1.4426950408889634
NEG_BIG = -1e30

LANES = 128
HEAD_PAIR_W = 128
KEY_CHUNK = 512
VMEM_LIMIT = 48 * 1024 * 1024

F32 = jnp.float32
BF16 = jnp.bfloat16


def _silu(v):
    return v * jax.nn.sigmoid(v)


def _params(n_axes):
    return pltpu.CompilerParams(dimension_semantics=("arbitrary",) * n_axes,
                                vmem_limit_bytes=VMEM_LIMIT)


def _nt_dot(a, b):
    return lax.dot_general(a, b, (((1,), (1,)), ((), ())), preferred_element_type=F32)


def _mod_kernel(cc_ref, w_ref, b_ref, o_ref):
    s = _silu(cc_ref[...]).astype(BF16)
    o_ref[0] = jnp.dot(s, w_ref[0].astype(BF16), preferred_element_type=F32) + b_ref[0]


def _mod_all(cc, w_mod, b_mod):
    depth, d, d3 = w_mod.shape
    tn = d
    return pl.pallas_call(
        _mod_kernel,
        grid=(depth, d3 // tn),
        in_specs=[pl.BlockSpec((cc.shape[0], d), lambda l, n: (0, 0)),
                  pl.BlockSpec((1, d, tn), lambda l, n: (l, 0, n)),
                  pl.BlockSpec((1, 1, tn), lambda l, n: (l, 0, n))],
        out_specs=pl.BlockSpec((1, cc.shape[0], tn), lambda l, n: (l, 0, n)),
        out_shape=jax.ShapeDtypeStruct((depth, cc.shape[0], d3), F32),
        compiler_params=_params(2), name="adaln_mod",
    )(cc, w_mod, b_mod.reshape(depth, 1, d3))


def _prenorm(x_ref, mod_ref, g_ref):
    d = x_ref.shape[-1]
    x = x_ref[...]
    y = x * lax.rsqrt(jnp.mean(x * x, axis=-1, keepdims=True) + EPS) * g_ref[...]
    m = mod_ref[0]
    h = y * (1.0 + m[:, d:2 * d]) + m[:, 0:d]
    return h.astype(BF16)


def _inproj_even_kernel(*refs, rope, tm):
    if rope:
        (x_ref, mod_ref, g_ref, w_ref, wqt_ref, wvt_ref, wqts_ref, wks_ref,
         cos_ref, sin_ref, cost_ref, sint_ref, qt_ref, k_ref, vt_ref, rest_ref) = refs
    else:
        (x_ref, mod_ref, g_ref, w_ref, wqt_ref, wvt_ref, qt_ref, k_ref, vt_ref, rest_ref) = refs
    bw = qt_ref.shape[1]
    hb = _prenorm(x_ref, mod_ref, g_ref)

    qt = _nt_dot(wqt_ref[...], hb)
    if rope:
        reps = bw // cost_ref.shape[0]
        qt = (qt * jnp.tile(cost_ref[...], (reps, 1))
              + _nt_dot(wqts_ref[...], hb) * jnp.tile(sint_ref[...], (reps, 1)))
    qt_ref[0] = (qt * (A_QK ** -0.5 * LOG2E)).astype(BF16)

    k = jnp.dot(hb, w_ref[:, bw:2 * bw], preferred_element_type=F32)
    if rope:
        reps = bw // cos_ref.shape[1]
        k = (k * jnp.tile(cos_ref[...], (1, reps))
             + jnp.dot(hb, wks_ref[...], preferred_element_type=F32) * jnp.tile(sin_ref[...], (1, reps)))
    k_ref[0] = k.astype(BF16)

    vt_ref[0] = _nt_dot(wvt_ref[...], hb).astype(BF16)

    n_rest = rest_ref.shape[1] // bw
    for c in range(n_rest):
        col = (3 + c) * bw
        rest_ref[:, c * bw:(c + 1) * bw] = jnp.dot(
            hb, w_ref[:, col:col + bw], preferred_element_type=F32).astype(BF16)


def _inproj_even(x2, mod3, mod_row_fn, g_pre, w_bf, seq, rope_tabs, tm):
    n, d = x2.shape
    bw = w_bf.shape[1] // 8
    nb = n // seq
    tps = seq // tm
    rope = rope_tabs is not None
    wqt = w_bf[:, 0:bw].T
    wvt = w_bf[:, 2 * bw:3 * bw].T
    const = lambda i: (0, 0)
    in_specs = [pl.BlockSpec((tm, d), lambda i: (i, 0)),
                pl.BlockSpec((1, 1, mod3.shape[2]), lambda i: (mod_row_fn(i // tps), 0, 0)),
                pl.BlockSpec((1, d), const),
                pl.BlockSpec(w_bf.shape, const),
                pl.BlockSpec(wqt.shape, const),
                pl.BlockSpec(wvt.shape, const)]
    args = [x2, mod3, g_pre.reshape(1, d), w_bf, wqt, wvt]
    if rope:
        cos_n, sin_n, cos_t, sin_t = rope_tabs
        swap = np.arange(bw) ^ 8
        wqts = w_bf[:, 0:bw][:, swap].T
        wks = w_bf[:, bw:2 * bw][:, swap]
        in_specs += [pl.BlockSpec(wqts.shape, const),
                     pl.BlockSpec(wks.shape, const),
                     pl.BlockSpec((tm, cos_n.shape[1]), lambda i: (i % tps, 0)),
                     pl.BlockSpec((tm, cos_n.shape[1]), lambda i: (i % tps, 0)),
                     pl.BlockSpec((cos_t.shape[0], tm), lambda i: (0, i % tps)),
                     pl.BlockSpec((cos_t.shape[0], tm), lambda i: (0, i % tps))]
        args += [wqts, wks, cos_n, sin_n, cos_t, sin_t]
    out_specs = [pl.BlockSpec((1, bw, tm), lambda i: (i // tps, 0, i % tps)),
                 pl.BlockSpec((1, tm, bw), lambda i: (i // tps, i % tps, 0)),
                 pl.BlockSpec((1, bw, tm), lambda i: (i // tps, 0, i % tps)),
                 pl.BlockSpec((tm, 5 * bw), lambda i: (i, 0))]
    out_shape = [jax.ShapeDtypeStruct((nb, bw, seq), BF16),
                 jax.ShapeDtypeStruct((nb, seq, bw), BF16),
                 jax.ShapeDtypeStruct((nb, bw, seq), BF16),
                 jax.ShapeDtypeStruct((n, 5 * bw), BF16)]
    return pl.pallas_call(
        functools.partial(_inproj_even_kernel, rope=rope, tm=tm),
        grid=(n // tm,), in_specs=in_specs, out_specs=out_specs, out_shape=out_shape,
        compiler_params=_params(1), name="inproj_even",
    )(*args)


def _inproj_odd_kernel(x_ref, mod_ref, g_ref, w_ref, cs_ref, q_ref, k_ref, v_ref, rest_ref, p_ref, pq_ref):
    bw = q_ref.shape[2]
    hb = _prenorm(x_ref, mod_ref, g_ref)

    def proj(c):
        return jnp.dot(hb, w_ref[:, c * bw:(c + 1) * bw], preferred_element_type=F32)

    q_ref[0] = (proj(0) * (C_HEAD_DIM ** -0.5 * LOG2E)).astype(BF16)
    k_ref[0] = proj(1).astype(BF16)
    v_ref[0] = proj(2).astype(BF16)
    rest_ref[:, 0:bw] = proj(3).astype(BF16)
    rest_ref[:, bw:2 * bw] = proj(5).astype(BF16)
    pq = jnp.dot(proj(4).astype(BF16), cs_ref[...], preferred_element_type=F32)
    p_ref[0] = pq[:, 0:bw].astype(BF16)
    pq_ref[0] = pq[:, bw:2 * bw].astype(BF16)


def _inproj_odd(x2, mod3, mod_row_fn, g_pre, w_bf, cs_bf, seq, tm):
    n, d = x2.shape
    bw = w_bf.shape[1] // 6
    nb = n // seq
    tps = seq // tm
    const = lambda i: (0, 0)
    tok = pl.BlockSpec((1, tm, bw), lambda i: (i // tps, i % tps, 0))
    tok_shape = jax.ShapeDtypeStruct((nb, seq, bw), BF16)
    return pl.pallas_call(
        _inproj_odd_kernel,
        grid=(n // tm,),
        in_specs=[pl.BlockSpec((tm, d), lambda i: (i, 0)),
                  pl.BlockSpec((1, 1, mod3.shape[2]), lambda i: (mod_row_fn(i // tps), 0, 0)),
                  pl.BlockSpec((1, d), const),
                  pl.BlockSpec(w_bf.shape, const),
                  pl.BlockSpec(cs_bf.shape, const)],
        out_specs=[tok, tok, tok, pl.BlockSpec((tm, 2 * bw), lambda i: (i, 0)), tok, tok],
        out_shape=[tok_shape, tok_shape, tok_shape, jax.ShapeDtypeStruct((n, 2 * bw), BF16),
                   tok_shape, tok_shape],
        compiler_params=_params(1), name="inproj_odd",
    )(x2, mod3, g_pre.reshape(1, d), w_bf, cs_bf)


def _diff_attn_kernel(*refs, has_main, tq, lam_init):
    if has_main:
        (lam_ref, sub_ref, qt_ref, kc_ref, vct_ref, ka_ref, vat_ref, ga_ref, o_ref, sbuf) = refs
        key_refs = [(kc_ref, vct_ref), (ka_ref, vat_ref)]
    else:
        (lam_ref, sub_ref, qt_ref, kc_ref, vct_ref, ga_ref, o_ref, sbuf) = refs
        key_refs = [(kc_ref, vct_ref)]
    chunks = []
    off = 0
    for k_ref, vt_ref in key_refs:
        for st in range(0, k_ref.shape[1], KEY_CHUNK):
            sz = min(KEY_CHUNK, k_ref.shape[1] - st)
            chunks.append((k_ref, vt_ref, st, sz, off))
            off += sz
    lp = lam_ref[...]
    lam = (jnp.exp(jnp.sum(lp[0:1] * lp[1:2], axis=1, keepdims=True))
           - jnp.exp(jnp.sum(lp[2:3] * lp[3:4], axis=1, keepdims=True)) + lam_init)
    qf = qt_ref[0].astype(F32)
    rowid = lax.broadcasted_iota(jnp.int32, qf.shape, 0)
    sub = jnp.tile(sub_ref[...], (1, tq // LANES))

    def fold(v):
        return v.reshape(v.shape[0] // 8, 8, tq)

    heads = []
    for hh in range(2):
        maps = []
        for m in range(2):
            lo = hh * A_V + m * A_QK
            qm = jnp.where((rowid >= lo) & (rowid < lo + A_QK), qf, 0.0).astype(BF16)
            sb = sbuf.at[m]
            mx = None
            for k_ref, _, st, sz, off in chunks:
                s = jnp.dot(k_ref[0, st:st + sz, :], qm, preferred_element_type=F32)
                sb[off:off + sz, :] = s
                cm = jnp.max(fold(s), axis=0)
                mx = cm if mx is None else jnp.maximum(mx, cm)
            mrow = jnp.max(mx, axis=0, keepdims=True)
            acc = jnp.zeros((A_V, tq), F32)
            l = jnp.zeros((8, tq), F32)
            for _, vt_ref, st, sz, off in chunks:
                e = jnp.exp2(sb[off:off + sz, :] - mrow)
                l = l + jnp.sum(fold(e), axis=0)
                acc = acc + jnp.dot(vt_ref[0, hh * A_V:(hh + 1) * A_V, st:st + sz], e.astype(BF16),
                                    preferred_element_type=F32)
            maps.append(acc * (1.0 / jnp.sum(l, axis=0, keepdims=True)))
        ot = maps[0] - lam * maps[1]
        ot = ot * lax.rsqrt(jnp.mean(ot * ot, axis=0, keepdims=True) + EPS) * sub * (1.0 - lam_init)
        heads.append(ot)
    o = jnp.concatenate(heads, axis=0).T
    o_ref[...] = o * _silu(ga_ref[...].astype(F32))


def _diff_attn(lam_p, subln, qt, kc, vct, ka, vat, rest, lam_init, tq):
    nb, bw, sq = qt.shape
    nq = sq // tq
    n_keys = kc.shape[1] + (0 if ka is None else ka.shape[1])
    hp = HEAD_PAIR_W
    sub_b = jnp.broadcast_to(subln.reshape(A_V, 1), (A_V, LANES))
    in_specs = [pl.BlockSpec(lam_p.shape, lambda b, h, i: (0, 0)),
                pl.BlockSpec(sub_b.shape, lambda b, h, i: (0, 0)),
                pl.BlockSpec((1, hp, tq), lambda b, h, i: (b, h, i)),
                pl.BlockSpec((1, kc.shape[1], hp), lambda b, h, i: (b, 0, h)),
                pl.BlockSpec((1, hp, kc.shape[1]), lambda b, h, i: (b, h, 0))]
    args = [lam_p, sub_b, qt, kc, vct]
    if ka is not None:
        in_specs += [pl.BlockSpec((1, ka.shape[1], hp), lambda b, h, i: (b, 0, h)),
                     pl.BlockSpec((1, hp, ka.shape[1]), lambda b, h, i: (b, h, 0))]
        args += [ka, vat]
    in_specs.append(pl.BlockSpec((tq, hp), lambda b, h, i: (b * nq + i, h)))
    args.append(rest)
    return pl.pallas_call(
        functools.partial(_diff_attn_kernel, has_main=ka is not None, tq=tq, lam_init=lam_init),
        grid=(nb, bw // hp, nq),
        in_specs=in_specs,
        out_specs=pl.BlockSpec((tq, hp), lambda b, h, i: (b * nq + i, h)),
        out_shape=jax.ShapeDtypeStruct((nb * sq, bw), F32),
        scratch_shapes=[pltpu.VMEM((2, n_keys, tq), F32)],
        compiler_params=_params(3), name="diff_attn",
    )(*args)


NA_QROWS = 4
NA_KROWS = 12


def _softmax_pv(q, pieces, gate_ref, o_ref):
    lane = lax.broadcasted_iota(jnp.int32, q.shape, 1)
    qf = q.astype(F32)
    outs = []
    for hh in range(2):
        qm = jnp.where((lane >= hh * C_HEAD_DIM) & (lane < (hh + 1) * C_HEAD_DIM), qf, 0.0).astype(BF16)
        ss = []
        for k, _, bias in pieces:
            s = _nt_dot(qm, k)
            if bias is not None:
                s = s + bias[hh]
            ss.append(s)
        m = functools.reduce(jnp.maximum, [jnp.max(s, axis=1, keepdims=True) for s in ss])
        es = [jnp.exp2(s - m) for s in ss]
        l = functools.reduce(jnp.add, [jnp.sum(e, axis=1, keepdims=True) for e in es])
        pv = functools.reduce(jnp.add, [jnp.dot(e.astype(BF16), v, preferred_element_type=F32)
                                        for e, (_, v, _) in zip(es, pieces)])
        outs.append(pv * (1.0 / l))
    o = jnp.where(lane < C_HEAD_DIM, outs[0], outs[1])
    o_ref[...] = o * _silu(gate_ref[...].astype(F32))


def _na_kernel(q_ref, k_ref, v_ref, kc_ref, vc_ref, bias_ref, gate_ref, o_ref, *, rows):
    g = pl.program_id(2)
    base = jnp.clip(NA_QROWS * g - NA_KH // 2, 0, rows - NA_KROWS)
    start = pl.multiple_of(base * GRID_W, NA_QROWS * GRID_W)
    nk = NA_KROWS * GRID_W
    kw = k_ref[0, pl.ds(start, nk), :]
    vw = v_ref[0, pl.ds(start, nk), :]

    class _Bias:
        def __getitem__(self, hh):
            return bias_ref[0, hh]

    _softmax_pv(q_ref[0], [(kw, vw, _Bias()), (kc_ref[0], vc_ref[0], None)], gate_ref, o_ref)


def _na_case_tables(rows):
    rs = np.clip(np.arange(rows) - NA_KH // 2, 0, rows - NA_KH)
    n_groups = rows // NA_QROWS
    pats = []
    for g in range(n_groups):
        base = int(np.clip(NA_QROWS * g - NA_KH // 2, 0, rows - NA_KROWS))
        pat = np.full((NA_QROWS, NA_KROWS), 2 * NA_KH - 1, np.int32)
        for j in range(NA_QROWS):
            r = NA_QROWS * g + j
            for i in range(NA_KROWS):
                kr = base + i
                if rs[r] <= kr < rs[r] + NA_KH:
                    pat[j, i] = kr - r + NA_KH - 1
            assert (pat[j] != 2 * NA_KH - 1).sum() == NA_KH
        pats.append(pat)
    cases = np.stack([pats[0], pats[1], pats[-1]])
    for g in range(n_groups):
        want = 0 if g == 0 else (2 if g == n_groups - 1 else 1)
        assert (pats[g] == cases[want]).all()
    return cases


def _na_bias(rpb, rows):
    h = rpb.shape[0]
    cq = np.arange(GRID_W)
    cs = np.clip(cq - NA_KW // 2, 0, GRID_W - NA_KW)
    kcol = np.arange(GRID_W)
    col_ok = (kcol[None, :] >= cs[:, None]) & (kcol[None, :] < cs[:, None] + NA_KW)
    col_off = np.clip(kcol[None, :] - cq[:, None] + NA_KW - 1, 0, 2 * NA_KW - 2)
    toep = jnp.where(col_ok[None, None], rpb[:, :, col_off] * LOG2E, NEG_BIG)
    toep = jnp.concatenate([toep, jnp.full((h, 1, GRID_W, GRID_W), NEG_BIG, F32)], axis=1)
    cases = _na_case_tables(rows)
    t = toep[:, cases]
    t = jnp.transpose(t, (1, 0, 2, 4, 3, 5))
    return t.reshape(3, h, NA_QROWS * GRID_W, NA_KROWS * GRID_W)


def _na_attn(q, k, v, kc, vc, bias, rest):
    nb, s, bw = q.shape
    hp = HEAD_PAIR_W
    tq = NA_QROWS * GRID_W
    ng = s // tq
    l = kc.shape[1]
    case = lambda g: jnp.where(g == 0, 0, jnp.where(g == ng - 1, 2, 1))
    return pl.pallas_call(
        functools.partial(_na_kernel, rows=s // GRID_W),
        grid=(nb, bw // hp, ng),
        in_specs=[pl.BlockSpec((1, tq, hp), lambda b, h, g: (b, g, h)),
                  pl.BlockSpec((1, s, hp), lambda b, h, g: (b, 0, h)),
                  pl.BlockSpec((1, s, hp), lambda b, h, g: (b, 0, h)),
                  pl.BlockSpec((1, l, hp), lambda b, h, g: (b, 0, h)),
                  pl.BlockSpec((1, l, hp), lambda b, h, g: (b, 0, h)),
                  pl.BlockSpec((1, 2, tq, NA_KROWS * GRID_W), lambda b, h, g: (case(g), h, 0, 0)),
                  pl.BlockSpec((tq, hp), lambda b, h, g: (b * ng + g, h))],
        out_specs=pl.BlockSpec((tq, hp), lambda b, h, g: (b * ng + g, h)),
        out_shape=jax.ShapeDtypeStruct((nb * s, bw), F32),
        compiler_params=_params(3), name="nbr_attn",
    )(q, k, v, kc, vc, bias, rest)


def _dense_attn_kernel(q_ref, k_ref, v_ref, gate_ref, o_ref):
    _softmax_pv(q_ref[0], [(k_ref[0], v_ref[0], None)], gate_ref, o_ref)


def _dense_attn(q, k, v, rest):
    nb, l, bw = q.shape
    hp = HEAD_PAIR_W
    tok = pl.BlockSpec((1, l, hp), lambda b, h: (b, 0, h))
    return pl.pallas_call(
        _dense_attn_kernel,
        grid=(nb, bw // hp),
        in_specs=[tok, tok, tok, pl.BlockSpec((l, hp), lambda b, h: (b, h))],
        out_specs=pl.BlockSpec((l, hp), lambda b, h: (b, h)),
        out_shape=jax.ShapeDtypeStruct((nb * l, bw), F32),
        compiler_params=_params(2), name="ctx_dense_attn",
    )(q, k, v, rest)


def _dft_kernel(c_ref, s_ref, p_ref, q_ref, o_ref):
    o_ref[0] = (jnp.dot(c_ref[...], p_ref[0], preferred_element_type=F32)
                + jnp.dot(s_ref[...], q_ref[0], preferred_element_type=F32))


def _dft_tables(n):
    idx = jnp.arange(n, dtype=jnp.int32)
    ang = ((idx[:, None] * idx[None, :]) % n).astype(F32) * (2.0 * math.pi / n)
    scale = n ** -0.5
    return (jnp.cos(ang) * scale).astype(BF16), (jnp.sin(ang) * scale).astype(BF16)


def _dft_positions(cn, sn, p, q, tmo):
    nb, n, bw = p.shape
    return pl.pallas_call(
        _dft_kernel,
        grid=(n // tmo, nb),
        in_specs=[pl.BlockSpec((tmo, n), lambda i, b: (i, 0)),
                  pl.BlockSpec((tmo, n), lambda i, b: (i, 0)),
                  pl.BlockSpec((1, n, bw), lambda i, b: (b, 0, 0)),
                  pl.BlockSpec((1, n, bw), lambda i, b: (b, 0, 0))],
        out_specs=pl.BlockSpec((1, tmo, bw), lambda i, b: (b, i, 0)),
        out_shape=jax.ShapeDtypeStruct((nb, n, bw), F32),
        compiler_params=_params(2), name="dft_positions",
    )(cn, sn, p, q)


def _channel_dft_matrix():
    j = np.arange(D_GROUP_W)
    ang = 2.0 * np.pi * ((j[:, None] * j[None, :]) % D_GROUP_W) / D_GROUP_W
    eye = np.eye(D_GROUPS)
    cc = np.kron(eye, np.cos(ang) * D_GROUP_W ** -0.5)
    sc = np.kron(eye, np.sin(ang) * D_GROUP_W ** -0.5)
    return jnp.asarray(np.concatenate([cc, -sc], axis=1), BF16)


def _post_kernel(*refs, even, tm, seq):
    if even:
        (x_ref, a_ref, bb_ref, cb_ref, ub_ref, gb_ref, cbp_ref, ubp_ref, cbn_ref, ubn_ref,
         cw_ref, mod_ref, g_ref, w_ref, o_ref) = refs
    else:
        (x_ref, a_ref, y_ref, gd_ref, mod_ref, g_ref, w_ref, o_ref) = refs
    d = x_ref.shape[-1]
    bw = a_ref.shape[-1]
    if even:
        i = pl.program_id(0)
        t0 = (i * tm) % seq
        z = cb_ref[...].astype(F32) * ub_ref[...].astype(F32)
        zp = cbp_ref[7:8, :].astype(F32) * ubp_ref[7:8, :].astype(F32) * jnp.where(t0 == 0, 0.0, 1.0)
        zn = cbn_ref[0:1, :].astype(F32) * ubn_ref[0:1, :].astype(F32) * jnp.where(t0 + tm == seq, 0.0, 1.0)
        rid = lax.broadcasted_iota(jnp.int32, z.shape, 0)
        z_prev = jnp.where(rid == 0, zp, pltpu.roll(z, 1, 0))
        z_next = jnp.where(rid == tm - 1, zn, pltpu.roll(z, tm - 1, 0))
        cw = cw_ref[...]
        y = cw[0:1] * z_prev + cw[1:2] * z + cw[2:3] * z_next
        second = bb_ref[...].astype(F32) * y * _silu(gb_ref[...].astype(F32))
    else:
        second = y_ref[...] * _silu(gd_ref[...].astype(F32))
    yo = (jnp.dot(a_ref[...].astype(BF16), w_ref[0:bw, :], preferred_element_type=F32)
          + jnp.dot(second.astype(BF16), w_ref[bw:2 * bw, :], preferred_element_type=F32))
    nrm = yo * lax.rsqrt(jnp.mean(yo * yo, axis=-1, keepdims=True) + EPS) * g_ref[...]
    o_ref[...] = x_ref[...] + mod_ref[0][:, 2 * d:3 * d] * nrm


def _post(x2, a, second, rest, conv_w, mod3, mod_row_fn, g_post, w_out_bf, seq, tm, even):
    n, d = x2.shape
    bw = a.shape[1]
    tps = seq // tm
    const = lambda i: (0, 0)
    row = lambda i: (i, 0)
    in_specs = [pl.BlockSpec((tm, d), row), pl.BlockSpec((tm, bw), row)]
    args = [x2, a]
    if even:
        nblk8 = n // 8
        for c in (1, 2, 3, 4):
            in_specs.append(pl.BlockSpec((tm, bw), lambda i, c=c: (i, c)))
            args.append(rest)
        prev = lambda i, c: (jnp.maximum(i * (tm // 8) - 1, 0), c)
        nxt = lambda i, c: (jnp.minimum((i + 1) * (tm // 8), nblk8 - 1), c)
        for fn in (prev, nxt):
            for c in (2, 3):
                in_specs.append(pl.BlockSpec((8, bw), functools.partial(fn, c=c)))
                args.append(rest)
        in_specs.append(pl.BlockSpec(conv_w.shape, const))
        args.append(conv_w)
    else:
        in_specs += [pl.BlockSpec((tm, bw), row), pl.BlockSpec((tm, bw), lambda i: (i, 1))]
        args += [second, rest]
    in_specs += [pl.BlockSpec((1, 1, mod3.shape[2]), lambda i: (mod_row_fn(i // tps), 0, 0)),
                 pl.BlockSpec((1, d), const),
                 pl.BlockSpec(w_out_bf.shape, const)]
    args += [mod3, g_post.reshape(1, d), w_out_bf]
    return pl.pallas_call(
        functools.partial(_post_kernel, even=even, tm=tm, seq=seq),
        grid=(n // tm,), in_specs=in_specs,
        out_specs=pl.BlockSpec((tm, d), row),
        out_shape=jax.ShapeDtypeStruct((n, d), F32),
        compiler_params=_params(1), name="post_even" if even else "post_odd",
    )(*args)


def _rope_tables(seq):
    t = jnp.arange(seq)
    row = (t // GRID_W).astype(F32)
    col = (t % GRID_W).astype(F32)
    nf = A_QK // 4
    inv_freq = ROPE_BASE ** (-jnp.arange(nf, dtype=F32) / nf)
    p = np.arange(A_QK)
    f_idx = p % nf
    use_col = (p % A_QK) >= A_QK // 2
    sign = np.where((p % (2 * nf)) < nf, -1.0, 1.0).astype(np.float32)
    pos = jnp.where(use_col[None, :], col[:, None], row[:, None])
    ang = pos * inv_freq[f_idx][None, :]
    cos = jnp.cos(ang)
    sin = jnp.sin(ang) * sign[None, :]
    reps = LANES // A_QK
    return jnp.tile(cos, (1, reps)), jnp.tile(sin, (1, reps)), cos.T, sin.T


def _lambda_init(layer):
    return 0.8 - 0.6 * math.exp(-0.3 * layer)


def kernel(x, c, ctx, c_ctx, w_mod, b_mod, norm_pre, norm_post, w_in_even, lam_a, subln_a, conv_b,
           w_in_odd, rpb_c, w_out):
    nb, seq, d = x.shape
    lctx = ctx.shape[1]
    depth = w_mod.shape[0]
    assert seq % (NA_QROWS * GRID_W) == 0 and seq // GRID_W >= NA_KROWS
    assert lctx % LANES == 0 and seq % KEY_CHUNK == 0 and nb < 16

    pad = jnp.zeros((16 - nb - 1, d), F32)
    cc = jnp.concatenate([c, c_ctx[None, :], pad], axis=0)
    mods = _mod_all(cc, w_mod, b_mod)

    rope_tabs = _rope_tables(seq)
    cs_bf = _channel_dft_matrix()
    cn_x, sn_x = _dft_tables(seq)
    cn_c, sn_c = _dft_tables(lctx)

    tm_x = 512
    tm_c = lctx
    x_row = lambda b: b
    c_row = lambda b: nb

    x2 = x.reshape(nb * seq, d)
    c2 = ctx.reshape(nb * lctx, d)
    for l in range(depth):
        need_ctx = l < depth - 1
        j = l // 2
        mod3 = mods[l].reshape(16, 1, 3 * d)
        w_out_bf = w_out[l].astype(BF16)
        if l % 2 == 0:
            w_bf = w_in_even[j].astype(BF16)
            li = _lambda_init(l)
            qt, k, vt, rest = _inproj_even(x2, mod3, x_row, norm_pre[l], w_bf, seq, rope_tabs, tm_x)
            qct, kc, vct, rest_c = _inproj_even(c2, mod3, c_row, norm_pre[l], w_bf, lctx, None, tm_c)
            a = _diff_attn(lam_a[j], subln_a[j], qt, kc, vct, k, vt, rest, li, 256)
            x2n = _post(x2, a, None, rest, conv_b[j], mod3, x_row, norm_post[l], w_out_bf, seq, tm_x, True)
            if need_ctx:
                ac = _diff_attn(lam_a[j], subln_a[j], qct, kc, vct, None, None, rest_c, li, lctx)
                c2 = _post(c2, ac, None, rest_c, conv_b[j], mod3, c_row, norm_post[l], w_out_bf,
                           lctx, tm_c, True)
            x2 = x2n
        else:
            w_bf = w_in_odd[j].astype(BF16)
            q, k, v, rest, p, pq = _inproj_odd(x2, mod3, x_row, norm_pre[l], w_bf, cs_bf, seq, tm_x)
            qc, kc, vc, rest_c, p_c, pq_c = _inproj_odd(c2, mod3, c_row, norm_pre[l], w_bf, cs_bf, lctx, tm_c)
            bias = _na_bias(rpb_c[j], seq // GRID_W)
            a = _na_attn(q, k, v, kc, vc, bias, rest)
            y = _dft_positions(cn_x, sn_x, p, pq, 512).reshape(nb * seq, -1)
            x2n = _post(x2, a, y, rest, None, mod3, x_row, norm_post[l], w_out_bf, seq, tm_x, False)
            if need_ctx:
                ac = _dense_attn(qc, kc, vc, rest_c)
                yc = _dft_positions(cn_c, sn_c, p_c, pq_c, lctx).reshape(nb * lctx, -1)
                c2 = _post(c2, ac, yc, rest_c, None, mod3, c_row, norm_post[l], w_out_bf, lctx, tm_c, False)
            x2 = x2n
    return x2.reshape(nb, seq, d)
```

```python
import functools
import math

import numpy as np
import jax
import jax.numpy as jnp
from jax import lax
from jax.experimental import pallas as pl
from jax.experimental.pallas import tpu as pltpu

GRID_W = 64
A_HEADS = 8
A_QK = 32
A_V = 64
C_HEADS = 8
C_HEAD_DIM = 64
NA_KH = 8
NA_KW = 16
D_GROUPS = 8
D_GROUP_W = 64
CONV_W = 3
ROPE_BASE = 10000.0
EPS = 1e-6
LOG2E = 1.4426950408889634
NEG_BIG = -1e30

LANES = 128
HEAD_PAIR_W = 128
KEY_CHUNK = 512
VMEM_LIMIT = 48 * 1024 * 1024

F32 = jnp.float32
BF16 = jnp.bfloat16


def _silu(v):
    return v * jax.nn.sigmoid(v)


def _params(n_axes):
    return pltpu.CompilerParams(dimension_semantics=("arbitrary",) * n_axes,
                                vmem_limit_bytes=VMEM_LIMIT)


def _nt_dot(a, b):
    return lax.dot_general(a, b, (((1,), (1,)), ((), ())), preferred_element_type=F32)


def _mod_kernel(cc_ref, w_ref, b_ref, o_ref):
    s = _silu(cc_ref[...]).astype(BF16)
    o_ref[0] = jnp.dot(s, w_ref[0].astype(BF16), preferred_element_type=F32) + b_ref[0]


def _mod_all(cc, w_mod, b_mod):
    depth, d, d3 = w_mod.shape
    tn = d
    return pl.pallas_call(
        _mod_kernel,
        grid=(depth, d3 // tn),
        in_specs=[pl.BlockSpec((cc.shape[0], d), lambda l, n: (0, 0)),
                  pl.BlockSpec((1, d, tn), lambda l, n: (l, 0, n)),
                  pl.BlockSpec((1, 1, tn), lambda l, n: (l, 0, n))],
        out_specs=pl.BlockSpec((1, cc.shape[0], tn), lambda l, n: (l, 0, n)),
        out_shape=jax.ShapeDtypeStruct((depth, cc.shape[0], d3), F32),
        compiler_params=_params(2), name="adaln_mod",
    )(cc, w_mod, b_mod.reshape(depth, 1, d3))


def _prenorm(x_ref, mod_ref, g_ref):
    d = x_ref.shape[-1]
    x = x_ref[...]
    y = x * lax.rsqrt(jnp.mean(x * x, axis=-1, keepdims=True) + EPS) * g_ref[...]
    m = mod_ref[0]
    h = y * (1.0 + m[:, d:2 * d]) + m[:, 0:d]
    return h.astype(BF16)


def _inproj_even_kernel(*refs, rope, tm):
    if rope:
        (x_ref, mod_ref, g_ref, w_ref, wqt_ref, wvt_ref, wqts_ref, wks_ref,
         cos_ref, sin_ref, cost_ref, sint_ref, qt_ref, k_ref, vt_ref, rest_ref) = refs
    else:
        (x_ref, mod_ref, g_ref, w_ref, wqt_ref, wvt_ref, qt_ref, k_ref, vt_ref, rest_ref) = refs
    bw = qt_ref.shape[1]
    hb = _prenorm(x_ref, mod_ref, g_ref)

    qt = _nt_dot(wqt_ref[...], hb)
    if rope:
        reps = bw // cost_ref.shape[0]
        qt = (qt * jnp.tile(cost_ref[...], (reps, 1))
              + _nt_dot(wqts_ref[...], hb) * jnp.tile(sint_ref[...], (reps, 1)))
    qt_ref[0] = (qt * (A_QK ** -0.5 * LOG2E)).astype(BF16)

    k = jnp.dot(hb, w_ref[:, bw:2 * bw], preferred_element_type=F32)
    if rope:
        reps = bw // cos_ref.shape[1]
        k = (k * jnp.tile(cos_ref[...], (1, reps))
             + jnp.dot(hb, wks_ref[...], preferred_element_type=F32) * jnp.tile(sin_ref[...], (1, reps)))
    k_ref[0] = k.astype(BF16)

    vt_ref[0] = _nt_dot(wvt_ref[...], hb).astype(BF16)

    n_rest = rest_ref.shape[1] // bw
    for c in range(n_rest):
        col = (3 + c) * bw
        rest_ref[:, c * bw:(c + 1) * bw] = jnp.dot(
            hb, w_ref[:, col:col + bw], preferred_element_type=F32).astype(BF16)


def _inproj_even(x2, mod3, mod_row_fn, g_pre, w_bf, seq, rope_tabs, tm):
    n, d = x2.shape
    bw = w_bf.shape[1] // 8
    nb = n // seq
    tps = seq // tm
    rope = rope_tabs is not None
    wqt = w_bf[:, 0:bw].T
    wvt = w_bf[:, 2 * bw:3 * bw].T
    const = lambda i: (0, 0)
    in_specs = [pl.BlockSpec((tm, d), lambda i: (i, 0)),
                pl.BlockSpec((1, 1, mod3.shape[2]), lambda i: (mod_row_fn(i // tps), 0, 0)),
                pl.BlockSpec((1, d), const),
                pl.BlockSpec(w_bf.shape, const),
                pl.BlockSpec(wqt.shape, const),
                pl.BlockSpec(wvt.shape, const)]
    args = [x2, mod3, g_pre.reshape(1, d), w_bf, wqt, wvt]
    if rope:
        cos_n, sin_n, cos_t, sin_t = rope_tabs
        swap = np.arange(bw) ^ 8
        wqts = w_bf[:, 0:bw][:, swap].T
        wks = w_bf[:, bw:2 * bw][:, swap]
        in_specs += [pl.BlockSpec(wqts.shape, const),
                     pl.BlockSpec(wks.shape, const),
                     pl.BlockSpec((tm, cos_n.shape[1]), lambda i: (i % tps, 0)),
                     pl.BlockSpec((tm, cos_n.shape[1]), lambda i: (i % tps, 0)),
                     pl.BlockSpec((cos_t.shape[0], tm), lambda i: (0, i % tps)),
                     pl.BlockSpec((cos_t.shape[0], tm), lambda i: (0, i % tps))]
        args += [wqts, wks, cos_n, sin_n, cos_t, sin_t]
    out_specs = [pl.BlockSpec((1, bw, tm), lambda i: (i // tps, 0, i % tps)),
                 pl.BlockSpec((1, tm, bw), lambda i: (i // tps, i % tps, 0)),
                 pl.BlockSpec((1, bw, tm), lambda i: (i // tps, 0, i % tps)),
                 pl.BlockSpec((tm, 5 * bw), lambda i: (i, 0))]
    out_shape = [jax.ShapeDtypeStruct((nb, bw, seq), BF16),
                 jax.ShapeDtypeStruct((nb, seq, bw), BF16),
                 jax.ShapeDtypeStruct((nb, bw, seq), BF16),
                 jax.ShapeDtypeStruct((n, 5 * bw), BF16)]
    return pl.pallas_call(
        functools.partial(_inproj_even_kernel, rope=rope, tm=tm),
        grid=(n // tm,), in_specs=in_specs, out_specs=out_specs, out_shape=out_shape,
        compiler_params=_params(1), name="inproj_even",
    )(*args)


def _inproj_odd_kernel(x_ref, mod_ref, g_ref, w_ref, cs_ref, q_ref, k_ref, v_ref, rest_ref, p_ref, pq_ref):
    bw = q_ref.shape[2]
    hb = _prenorm(x_ref, mod_ref, g_ref)

    def proj(c):
        return jnp.dot(hb, w_ref[:, c * bw:(c + 1) * bw], preferred_element_type=F32)

    q_ref[0] = (proj(0) * (C_HEAD_DIM ** -0.5 * LOG2E)).astype(BF16)
    k_ref[0] = proj(1).astype(BF16)
    v_ref[0] = proj(2).astype(BF16)
    rest_ref[:, 0:bw] = proj(3).astype(BF16)
    rest_ref[:, bw:2 * bw] = proj(5).astype(BF16)
    pq = jnp.dot(proj(4).astype(BF16), cs_ref[...], preferred_element_type=F32)
    p_ref[0] = pq[:, 0:bw].astype(BF16)
    pq_ref[0] = pq[:, bw:2 * bw].astype(BF16)


def _inproj_odd(x2, mod3, mod_row_fn, g_pre, w_bf, cs_bf, seq, tm):
    n, d = x2.shape
    bw = w_bf.shape[1] // 6
    nb = n // seq
    tps = seq // tm
    const = lambda i: (0, 0)
    tok = pl.BlockSpec((1, tm, bw), lambda i: (i // tps, i % tps, 0))
    tok_shape = jax.ShapeDtypeStruct((nb, seq, bw), BF16)
    return pl.pallas_call(
        _inproj_odd_kernel,
        grid=(n // tm,),
        in_specs=[pl.BlockSpec((tm, d), lambda i: (i, 0)),
                  pl.BlockSpec((1, 1, mod3.shape[2]), lambda i: (mod_row_fn(i // tps), 0, 0)),
                  pl.BlockSpec((1, d), const),
                  pl.BlockSpec(w_bf.shape, const),
                  pl.BlockSpec(cs_bf.shape, const)],
        out_specs=[tok, tok, tok, pl.BlockSpec((tm, 2 * bw), lambda i: (i, 0)), tok, tok],
        out_shape=[tok_shape, tok_shape, tok_shape, jax.ShapeDtypeStruct((n, 2 * bw), BF16),
                   tok_shape, tok_shape],
        compiler_params=_params(1), name="inproj_odd",
    )(x2, mod3, g_pre.reshape(1, d), w_bf, cs_bf)


def _diff_attn_kernel(*refs, has_main, tq, lam_init):
    if has_main:
        (lam_ref, sub_ref, qt_ref, kc_ref, vct_ref, ka_ref, vat_ref, ga_ref, o_ref, sbuf) = refs
        key_refs = [(kc_ref, vct_ref), (ka_ref, vat_ref)]
    else:
        (lam_ref, sub_ref, qt_ref, kc_ref, vct_ref, ga_ref, o_ref, sbuf) = refs
        key_refs = [(kc_ref, vct_ref)]
    chunks = []
    off = 0
    for k_ref, vt_ref in key_refs:
        for st in range(0, k_ref.shape[1], KEY_CHUNK):
            sz = min(KEY_CHUNK, k_ref.shape[1] - st)
            chunks.append((k_ref, vt_ref, st, sz, off))
            off += sz
    lp = lam_ref[...]
    lam = (jnp.exp(jnp.sum(lp[0:1] * lp[1:2], axis=1, keepdims=True))
           - jnp.exp(jnp.sum(lp[2:3] * lp[3:4], axis=1, keepdims=True)) + lam_init)
    qf = qt_ref[0].astype(F32)
    rowid = lax.broadcasted_iota(jnp.int32, qf.shape, 0)
    sub = jnp.tile(sub_ref[...], (1, tq // LANES))

    def fold(v):
        return v.reshape(v.shape[0] // 8, 8, tq)

    combos = [(hh, m) for hh in range(2) for m in range(2)]
    qms, mxs, mrows, accs, ls, maps = {}, {}, {}, {}, {}, {}
    for stage in range(len(combos) + 1):
        t, p = stage, stage - 1
        if t < len(combos):
            lo = combos[t][0] * A_V + combos[t][1] * A_QK
            qms[t] = jnp.where((rowid >= lo) & (rowid < lo + A_QK), qf, 0.0).astype(BF16)
        if p >= 0:
            accs[p] = jnp.zeros((A_V, tq), F32)
            ls[p] = jnp.zeros((8, tq), F32)
        for k_ref, vt_ref, st, sz, off in chunks:
            if t < len(combos):
                s = jnp.dot(k_ref[0, st:st + sz, :], qms[t], preferred_element_type=F32)
                sbuf[t % 2, off:off + sz, :] = s
                cm = jnp.max(fold(s), axis=0)
                mxs[t] = cm if t not in mxs else jnp.maximum(mxs[t], cm)
            if p >= 0:
                hh = combos[p][0]
                e = jnp.exp2(sbuf[p % 2, off:off + sz, :] - mrows[p])
                ls[p] = ls[p] + jnp.sum(fold(e), axis=0)
                accs[p] = accs[p] + jnp.dot(vt_ref[0, hh * A_V:(hh + 1) * A_V, st:st + sz],
                                            e.astype(BF16), preferred_element_type=F32)
        if t < len(combos):
            mrows[t] = jnp.max(mxs[t], axis=0, keepdims=True)
        if p >= 0:
            maps[p] = accs[p] * (1.0 / jnp.sum(ls[p], axis=0, keepdims=True))
    heads = []
    for hh in range(2):
        ot = maps[2 * hh] - lam * maps[2 * hh + 1]
        ot = ot * lax.rsqrt(jnp.mean(ot * ot, axis=0, keepdims=True) + EPS) * sub * (1.0 - lam_init)
        heads.append(ot)
    o = jnp.concatenate(heads, axis=0).T
    o_ref[...] = o * _silu(ga_ref[...].astype(F32))


def _diff_attn(lam_p, subln, qt, kc, vct, ka, vat, rest, lam_init, tq):
    nb, bw, sq = qt.shape
    nq = sq // tq
    n_keys = kc.shape[1] + (0 if ka is None else ka.shape[1])
    hp = HEAD_PAIR_W
    sub_b = jnp.broadcast_to(subln.reshape(A_V, 1), (A_V, LANES))
    in_specs = [pl.BlockSpec(lam_p.shape, lambda b, h, i: (0, 0)),
                pl.BlockSpec(sub_b.shape, lambda b, h, i: (0, 0)),
                pl.BlockSpec((1, hp, tq), lambda b, h, i: (b, h, i)),
                pl.BlockSpec((1, kc.shape[1], hp), lambda b, h, i: (b, 0, h)),
                pl.BlockSpec((1, hp, kc.shape[1]), lambda b, h, i: (b, h, 0))]
    args = [lam_p, sub_b, qt, kc, vct]
    if ka is not None:
        in_specs += [pl.BlockSpec((1, ka.shape[1], hp), lambda b, h, i: (b, 0, h)),
                     pl.BlockSpec((1, hp, ka.shape[1]), lambda b, h, i: (b, h, 0))]
        args += [ka, vat]
    in_specs.append(pl.BlockSpec((tq, hp), lambda b, h, i: (b * nq + i, h)))
    args.append(rest)
    return pl.pallas_call(
        functools.partial(_diff_attn_kernel, has_main=ka is not None, tq=tq, lam_init=lam_init),
        grid=(nb, bw // hp, nq),
        in_specs=in_specs,
        out_specs=pl.BlockSpec((tq, hp), lambda b, h, i: (b * nq + i, h)),
        out_shape=jax.ShapeDtypeStruct((nb * sq, bw), F32),
        scratch_shapes=[pltpu.VMEM((2, n_keys, tq), F32)],
        compiler_params=_params(3), name="diff_attn",
    )(*args)


NA_QROWS = 4
NA_KROWS = 12


def _softmax_pv(q, pieces, gate_ref, o_ref):
    lane = lax.broadcasted_iota(jnp.int32, q.shape, 1)
    qf = q.astype(F32)
    outs = []
    for hh in range(2):
        qm = jnp.where((lane >= hh * C_HEAD_DIM) & (lane < (hh + 1) * C_HEAD_DIM), qf, 0.0).astype(BF16)
        ss = []
        for k, _, bias in pieces:
            s = _nt_dot(qm, k)
            if bias is not None:
                s = s + bias[hh]
            ss.append(s)
        m = functools.reduce(jnp.maximum, [jnp.max(s, axis=1, keepdims=True) for s in ss])
        es = [jnp.exp2(s - m) for s in ss]
        l = functools.reduce(jnp.add, [jnp.sum(e, axis=1, keepdims=True) for e in es])
        pv = functools.reduce(jnp.add, [jnp.dot(e.astype(BF16), v, preferred_element_type=F32)
                                        for e, (_, v, _) in zip(es, pieces)])
        outs.append(pv * (1.0 / l))
    o = jnp.where(lane < C_HEAD_DIM, outs[0], outs[1])
    o_ref[...] = o * _silu(gate_ref[...].astype(F32))


def _na_kernel(q_ref, k_ref, v_ref, kc_ref, vc_ref, bias_ref, gate_ref, o_ref, *, rows):
    g = pl.program_id(2)
    base = jnp.clip(NA_QROWS * g - NA_KH // 2, 0, rows - NA_KROWS)
    start = pl.multiple_of(base * GRID_W, NA_QROWS * GRID_W)
    nk = NA_KROWS * GRID_W
    kw = k_ref[0, pl.ds(start, nk), :]
    vw = v_ref[0, pl.ds(start, nk), :]

    class _Bias:
        def __getitem__(self, hh):
            return bias_ref[0, hh]

    _softmax_pv(q_ref[0], [(kw, vw, _Bias()), (kc_ref[0], vc_ref[0], None)], gate_ref, o_ref)


def _na_case_tables(rows):
    rs = np.clip(np.arange(rows) - NA_KH // 2, 0, rows - NA_KH)
    n_groups = rows // NA_QROWS
    pats = []
    for g in range(n_groups):
        base = int(np.clip(NA_QROWS * g - NA_KH // 2, 0, rows - NA_KROWS))
        pat = np.full((NA_QROWS, NA_KROWS), 2 * NA_KH - 1, np.int32)
        for j in range(NA_QROWS):
            r = NA_QROWS * g + j
            for i in range(NA_KROWS):
                kr = base + i
                if rs[r] <= kr < rs[r] + NA_KH:
                    pat[j, i] = kr - r + NA_KH - 1
            assert (pat[j] != 2 * NA_KH - 1).sum() == NA_KH
        pats.append(pat)
    cases = np.stack([pats[0], pats[1], pats[-1]])
    for g in range(n_groups):
        want = 0 if g == 0 else (2 if g == n_groups - 1 else 1)
        assert (pats[g] == cases[want]).all()
    return cases


def _na_bias(rpb, rows):
    h = rpb.shape[0]
    cq = np.arange(GRID_W)
    cs = np.clip(cq - NA_KW // 2, 0, GRID_W - NA_KW)
    kcol = np.arange(GRID_W)
    col_ok = (kcol[None, :] >= cs[:, None]) & (kcol[None, :] < cs[:, None] + NA_KW)
    col_off = np.clip(kcol[None, :] - cq[:, None] + NA_KW - 1, 0, 2 * NA_KW - 2)
    toep = jnp.where(col_ok[None, None], rpb[:, :, col_off] * LOG2E, NEG_BIG)
    toep = jnp.concatenate([toep, jnp.full((h, 1, GRID_W, GRID_W), NEG_BIG, F32)], axis=1)
    cases = _na_case_tables(rows)
    t = toep[:, cases]
    t = jnp.transpose(t, (1, 0, 2, 4, 3, 5))
    return t.reshape(3, h, NA_QROWS * GRID_W, NA_KROWS * GRID_W)


def _na_attn(q, k, v, kc, vc, bias, rest):
    nb, s, bw = q.shape
    hp = HEAD_PAIR_W
    tq = NA_QROWS * GRID_W
    ng = s // tq
    l = kc.shape[1]
    case = lambda g: jnp.where(g == 0, 0, jnp.where(g == ng - 1, 2, 1))
    return pl.pallas_call(
        functools.partial(_na_kernel, rows=s // GRID_W),
        grid=(nb, bw // hp, ng),
        in_specs=[pl.BlockSpec((1, tq, hp), lambda b, h, g: (b, g, h)),
                  pl.BlockSpec((1, s, hp), lambda b, h, g: (b, 0, h)),
                  pl.BlockSpec((1, s, hp), lambda b, h, g: (b, 0, h)),
                  pl.BlockSpec((1, l, hp), lambda b, h, g: (b, 0, h)),
                  pl.BlockSpec((1, l, hp), lambda b, h, g: (b, 0, h)),
                  pl.BlockSpec((1, 2, tq, NA_KROWS * GRID_W), lambda b, h, g: (case(g), h, 0, 0)),
                  pl.BlockSpec((tq, hp), lambda b, h, g: (b * ng + g, h))],
        out_specs=pl.BlockSpec((tq, hp), lambda b, h, g: (b * ng + g, h)),
        out_shape=jax.ShapeDtypeStruct((nb * s, bw), F32),
        compiler_params=_params(3), name="nbr_attn",
    )(q, k, v, kc, vc, bias, rest)


def _dense_attn_kernel(q_ref, k_ref, v_ref, gate_ref, o_ref):
    _softmax_pv(q_ref[0], [(k_ref[0], v_ref[0], None)], gate_ref, o_ref)


def _dense_attn(q, k, v, rest):
    nb, l, bw = q.shape
    hp = HEAD_PAIR_W
    tok = pl.BlockSpec((1, l, hp), lambda b, h: (b, 0, h))
    return pl.pallas_call(
        _dense_attn_kernel,
        grid=(nb, bw // hp),
        in_specs=[tok, tok, tok, pl.BlockSpec((l, hp), lambda b, h: (b, h))],
        out_specs=pl.BlockSpec((l, hp), lambda b, h: (b, h)),
        out_shape=jax.ShapeDtypeStruct((nb * l, bw), F32),
        compiler_params=_params(2), name="ctx_dense_attn",
    )(q, k, v, rest)


def _dft_kernel(c_ref, s_ref, p_ref, q_ref, o_ref):
    o_ref[0] = (jnp.dot(c_ref[...], p_ref[0], preferred_element_type=F32)
                + jnp.dot(s_ref[...], q_ref[0], preferred_element_type=F32))


def _dft_tables(n):
    idx = jnp.arange(n, dtype=jnp.int32)
    ang = ((idx[:, None] * idx[None, :]) % n).astype(F32) * (2.0 * math.pi / n)
    scale = n ** -0.5
    return (jnp.cos(ang) * scale).astype(BF16), (jnp.sin(ang) * scale).astype(BF16)


def _dft_positions(cn, sn, p, q, tmo):
    nb, n, bw = p.shape
    return pl.pallas_call(
        _dft_kernel,
        grid=(n // tmo, nb),
        in_specs=[pl.BlockSpec((tmo, n), lambda i, b: (i, 0)),
                  pl.BlockSpec((tmo, n), lambda i, b: (i, 0)),
                  pl.BlockSpec((1, n, bw), lambda i, b: (b, 0, 0)),
                  pl.BlockSpec((1, n, bw), lambda i, b: (b, 0, 0))],
        out_specs=pl.BlockSpec((1, tmo, bw), lambda i, b: (b, i, 0)),
        out_shape=jax.ShapeDtypeStruct((nb, n, bw), F32),
        compiler_params=_params(2), name="dft_positions",
    )(cn, sn, p, q)


def _channel_dft_matrix():
    j = np.arange(D_GROUP_W)
    ang = 2.0 * np.pi * ((j[:, None] * j[None, :]) % D_GROUP_W) / D_GROUP_W
    eye = np.eye(D_GROUPS)
    cc = np.kron(eye, np.cos(ang) * D_GROUP_W ** -0.5)
    sc = np.kron(eye, np.sin(ang) * D_GROUP_W ** -0.5)
    return jnp.asarray(np.concatenate([cc, -sc], axis=1), BF16)


def _post_kernel(*refs, even, tm, seq):
    if even:
        (x_ref, a_ref, bb_ref, cb_ref, ub_ref, gb_ref, cbp_ref, ubp_ref, cbn_ref, ubn_ref,
         cw_ref, mod_ref, g_ref, w_ref, o_ref) = refs
    else:
        (x_ref, a_ref, y_ref, gd_ref, mod_ref, g_ref, w_ref, o_ref) = refs
    d = x_ref.shape[-1]
    bw = a_ref.shape[-1]
    if even:
        i = pl.program_id(0)
        t0 = (i * tm) % seq
        z = cb_ref[...].astype(F32) * ub_ref[...].astype(F32)
        zp = cbp_ref[7:8, :].astype(F32) * ubp_ref[7:8, :].astype(F32) * jnp.where(t0 == 0, 0.0, 1.0)
        zn = cbn_ref[0:1, :].astype(F32) * ubn_ref[0:1, :].astype(F32) * jnp.where(t0 + tm == seq, 0.0, 1.0)
        rid = lax.broadcasted_iota(jnp.int32, z.shape, 0)
        z_prev = jnp.where(rid == 0, zp, pltpu.roll(z, 1, 0))
        z_next = jnp.where(rid == tm - 1, zn, pltpu.roll(z, tm - 1, 0))
        cw = cw_ref[...]
        y = cw[0:1] * z_prev + cw[1:2] * z + cw[2:3] * z_next
        second = bb_ref[...].astype(F32) * y * _silu(gb_ref[...].astype(F32))
    else:
        second = y_ref[...] * _silu(gd_ref[...].astype(F32))
    yo = (jnp.dot(a_ref[...].astype(BF16), w_ref[0:bw, :], preferred_element_type=F32)
          + jnp.dot(second.astype(BF16), w_ref[bw:2 * bw, :], preferred_element_type=F32))
    nrm = yo * lax.rsqrt(jnp.mean(yo * yo, axis=-1, keepdims=True) + EPS) * g_ref[...]
    o_ref[...] = x_ref[...] + mod_ref[0][:, 2 * d:3 * d] * nrm


def _post(x2, a, second, rest, conv_w, mod3, mod_row_fn, g_post, w_out_bf, seq, tm, even):
    n, d = x2.shape
    bw = a.shape[1]
    tps = seq // tm
    const = lambda i: (0, 0)
    row = lambda i: (i, 0)
    in_specs = [pl.BlockSpec((tm, d), row), pl.BlockSpec((tm, bw), row)]
    args = [x2, a]
    if even:
        nblk8 = n // 8
        for c in (1, 2, 3, 4):
            in_specs.append(pl.BlockSpec((tm, bw), lambda i, c=c: (i, c)))
            args.append(rest)
        prev = lambda i, c: (jnp.maximum(i * (tm // 8) - 1, 0), c)
        nxt = lambda i, c: (jnp.minimum((i + 1) * (tm // 8), nblk8 - 1), c)
        for fn in (prev, nxt):
            for c in (2, 3):
                in_specs.append(pl.BlockSpec((8, bw), functools.partial(fn, c=c)))
                args.append(rest)
        in_specs.append(pl.BlockSpec(conv_w.shape, const))
        args.append(conv_w)
    else:
        in_specs += [pl.BlockSpec((tm, bw), row), pl.BlockSpec((tm, bw), lambda i: (i, 1))]
        args += [second, rest]
    in_specs += [pl.BlockSpec((1, 1, mod3.shape[2]), lambda i: (mod_row_fn(i // tps), 0, 0)),
                 pl.BlockSpec((1, d), const),
                 pl.BlockSpec(w_out_bf.shape, const)]
    args += [mod3, g_post.reshape(1, d), w_out_bf]
    return pl.pallas_call(
        functools.partial(_post_kernel, even=even, tm=tm, seq=seq),
        grid=(n // tm,), in_specs=in_specs,
        out_specs=pl.BlockSpec((tm, d), row),
        out_shape=jax.ShapeDtypeStruct((n, d), F32),
        compiler_params=_params(1), name="post_even" if even else "post_odd",
    )(*args)


def _rope_tables(seq):
    t = jnp.arange(seq)
    row = (t // GRID_W).astype(F32)
    col = (t % GRID_W).astype(F32)
    nf = A_QK // 4
    inv_freq = ROPE_BASE ** (-jnp.arange(nf, dtype=F32) / nf)
    p = np.arange(A_QK)
    f_idx = p % nf
    use_col = (p % A_QK) >= A_QK // 2
    sign = np.where((p % (2 * nf)) < nf, -1.0, 1.0).astype(np.float32)
    pos = jnp.where(use_col[None, :], col[:, None], row[:, None])
    ang = pos * inv_freq[f_idx][None, :]
    cos = jnp.cos(ang)
    sin = jnp.sin(ang) * sign[None, :]
    reps = LANES // A_QK
    return jnp.tile(cos, (1, reps)), jnp.tile(sin, (1, reps)), cos.T, sin.T


def _lambda_init(layer):
    return 0.8 - 0.6 * math.exp(-0.3 * layer)


def kernel(x, c, ctx, c_ctx, w_mod, b_mod, norm_pre, norm_post, w_in_even, lam_a, subln_a, conv_b,
           w_in_odd, rpb_c, w_out):
    nb, seq, d = x.shape
    lctx = ctx.shape[1]
    depth = w_mod.shape[0]
    assert seq % (NA_QROWS * GRID_W) == 0 and seq // GRID_W >= NA_KROWS
    assert lctx % LANES == 0 and seq % KEY_CHUNK == 0 and nb < 16

    pad = jnp.zeros((16 - nb - 1, d), F32)
    cc = jnp.concatenate([c, c_ctx[None, :], pad], axis=0)
    mods = _mod_all(cc, w_mod, b_mod)

    rope_tabs = _rope_tables(seq)
    cs_bf = _channel_dft_matrix()
    cn_x, sn_x = _dft_tables(seq)
    cn_c, sn_c = _dft_tables(lctx)

    tm_x = 512
    tm_c = lctx
    x_row = lambda b: b
    c_row = lambda b: nb

    x2 = x.reshape(nb * seq, d)
    c2 = ctx.reshape(nb * lctx, d)
    for l in range(depth):
        need_ctx = l < depth - 1
        j = l // 2
        mod3 = mods[l].reshape(16, 1, 3 * d)
        w_out_bf = w_out[l].astype(BF16)
        if l % 2 == 0:
            w_bf = w_in_even[j].astype(BF16)
            li = _lambda_init(l)
            qt, k, vt, rest = _inproj_even(x2, mod3, x_row, norm_pre[l], w_bf, seq, rope_tabs, tm_x)
            qct, kc, vct, rest_c = _inproj_even(c2, mod3, c_row, norm_pre[l], w_bf, lctx, None, tm_c)
            a = _diff_attn(lam_a[j], subln_a[j], qt, kc, vct, k, vt, rest, li, 512)
            x2n = _post(x2, a, None, rest, conv_b[j], mod3, x_row, norm_post[l], w_out_bf, seq, tm_x, True)
            if need_ctx:
                ac = _diff_attn(lam_a[j], subln_a[j], qct, kc, vct, None, None, rest_c, li, lctx)
                c2 = _post(c2, ac, None, rest_c, conv_b[j], mod3, c_row, norm_post[l], w_out_bf,
                           lctx, tm_c, True)
            x2 = x2n
        else:
            w_bf = w_in_odd[j].astype(BF16)
            q, k, v, rest, p, pq = _inproj_odd(x2, mod3, x_row, norm_pre[l], w_bf, cs_bf, seq, tm_x)
            qc, kc, vc, rest_c, p_c, pq_c = _inproj_odd(c2, mod3, c_row, norm_pre[l], w_bf, cs_bf, lctx, tm_c)
            bias = _na_bias(rpb_c[j], seq // GRID_W)
            a = _na_attn(q, k, v, kc, vc, bias, rest)
            y = _dft_positions(cn_x, sn_x, p, pq, 512).reshape(nb * seq, -1)
            x2n = _post(x2, a, y, rest, None, mod3, x_row, norm_post[l], w_out_bf, seq, tm_x, False)
            if need_ctx:
                ac = _dense_attn(qc, kc, vc, rest_c)
                yc = _dft_positions(cn_c, sn_c, p_c, pq_c, lctx).reshape(nb * lctx, -1)
                c2 = _post(c2, ac, yc, rest_c, None, mod3, c_row, norm_post[l], w_out_bf, lctx, tm_c, False)
            x2 = x2n
    return x2.reshape(nb, seq, d)
```

```python
import functools
import math

import numpy as np
import jax
import jax.numpy as jnp
from jax import lax
from jax.experimental import pallas as pl
from jax.experimental.pallas import tpu as pltpu

GRID_W = 64
A_HEADS = 8
A_QK = 32
A_V = 64
C_HEADS = 8
C_HEAD_DIM = 64
NA_KH = 8
NA_KW = 16
D_GROUPS = 8
D_GROUP_W = 64
CONV_W = 3
ROPE_BASE = 10000.0
EPS = 1e-6
LOG2E = 1.4426950408889634
NEG_BIG = -1e30

LANES = 128
HEAD_PAIR_W = 128
KEY_CHUNK = 512
DIFF_SUB_Q = 512
VMEM_LIMIT = 48 * 1024 * 1024

F32 = jnp.float32
BF16 = jnp.bfloat16


def _silu(v):
    return v * jax.nn.sigmoid(v)


def _params(n_axes):
    return pltpu.CompilerParams(dimension_semantics=("arbitrary",) * n_axes,
                                vmem_limit_bytes=VMEM_LIMIT)


def _nt_dot(a, b):
    return lax.dot_general(a, b, (((1,), (1,)), ((), ())), preferred_element_type=F32)


def _mod_kernel(cc_ref, w_ref, b_ref, o_ref):
    s = _silu(cc_ref[...]).astype(BF16)
    o_ref[0] = jnp.dot(s, w_ref[0].astype(BF16), preferred_element_type=F32) + b_ref[0]


def _mod_all(cc, w_mod, b_mod):
    depth, d, d3 = w_mod.shape
    tn = d
    return pl.pallas_call(
        _mod_kernel,
        grid=(depth, d3 // tn),
        in_specs=[pl.BlockSpec((cc.shape[0], d), lambda l, n: (0, 0)),
                  pl.BlockSpec((1, d, tn), lambda l, n: (l, 0, n)),
                  pl.BlockSpec((1, 1, tn), lambda l, n: (l, 0, n))],
        out_specs=pl.BlockSpec((1, cc.shape[0], tn), lambda l, n: (l, 0, n)),
        out_shape=jax.ShapeDtypeStruct((depth, cc.shape[0], d3), F32),
        compiler_params=_params(2), name="adaln_mod",
    )(cc, w_mod, b_mod.reshape(depth, 1, d3))


def _prenorm(x_ref, mod_ref, g_ref):
    d = x_ref.shape[-1]
    x = x_ref[...]
    y = x * lax.rsqrt(jnp.mean(x * x, axis=-1, keepdims=True) + EPS) * g_ref[...]
    m = mod_ref[0]
    h = y * (1.0 + m[:, d:2 * d]) + m[:, 0:d]
    return h.astype(BF16)


def _inproj_even_kernel(*refs, rope, tm):
    if rope:
        (x_ref, mod_ref, g_ref, w_ref, wqt_ref, wvt_ref,
         cos_ref, sin_ref, cost_ref, sint_ref, qt_ref, k_ref, vt_ref, rest_ref) = refs
    else:
        (x_ref, mod_ref, g_ref, w_ref, wqt_ref, wvt_ref, qt_ref, k_ref, vt_ref, rest_ref) = refs
    bw = qt_ref.shape[1]
    hb = _prenorm(x_ref, mod_ref, g_ref)
    pair = A_QK // 4

    def rotary(u, axis, cos, sin):
        idx = lax.broadcasted_iota(jnp.int32, u.shape, axis)
        partner = jnp.where((idx & pair) == 0, pltpu.roll(u, bw - pair, axis), pltpu.roll(u, pair, axis))
        return u * cos + partner * sin

    qt = _nt_dot(wqt_ref[...], hb)
    if rope:
        reps = bw // cost_ref.shape[0]
        qt = rotary(qt, 0, jnp.tile(cost_ref[...], (reps, 1)), jnp.tile(sint_ref[...], (reps, 1)))
    qt_ref[0] = (qt * (A_QK ** -0.5 * LOG2E)).astype(BF16)

    k = jnp.dot(hb, w_ref[:, bw:2 * bw], preferred_element_type=F32)
    if rope:
        reps = bw // cos_ref.shape[1]
        k = rotary(k, 1, jnp.tile(cos_ref[...], (1, reps)), jnp.tile(sin_ref[...], (1, reps)))
    k_ref[0] = k.astype(BF16)

    vt_ref[0] = _nt_dot(wvt_ref[...], hb).astype(BF16)

    n_rest = rest_ref.shape[1] // bw
    for c in range(n_rest):
        col = (3 + c) * bw
        rest_ref[:, c * bw:(c + 1) * bw] = jnp.dot(
            hb, w_ref[:, col:col + bw], preferred_element_type=F32).astype(BF16)


def _inproj_even(x2, mod3, mod_row_fn, g_pre, w_bf, seq, rope_tabs, tm):
    n, d = x2.shape
    bw = w_bf.shape[1] // 8
    nb = n // seq
    tps = seq // tm
    rope = rope_tabs is not None
    wqt = w_bf[:, 0:bw].T
    wvt = w_bf[:, 2 * bw:3 * bw].T
    const = lambda i: (0, 0)
    in_specs = [pl.BlockSpec((tm, d), lambda i: (i, 0)),
                pl.BlockSpec((1, 1, mod3.shape[2]), lambda i: (mod_row_fn(i // tps), 0, 0)),
                pl.BlockSpec((1, d), const),
                pl.BlockSpec(w_bf.shape, const),
                pl.BlockSpec(wqt.shape, const),
                pl.BlockSpec(wvt.shape, const)]
    args = [x2, mod3, g_pre.reshape(1, d), w_bf, wqt, wvt]
    if rope:
        cos_n, sin_n, cos_t, sin_t = rope_tabs
        in_specs += [pl.BlockSpec((tm, cos_n.shape[1]), lambda i: (i % tps, 0)),
                     pl.BlockSpec((tm, cos_n.shape[1]), lambda i: (i % tps, 0)),
                     pl.BlockSpec((cos_t.shape[0], tm), lambda i: (0, i % tps)),
                     pl.BlockSpec((cos_t.shape[0], tm), lambda i: (0, i % tps))]
        args += [cos_n, sin_n, cos_t, sin_t]
    out_specs = [pl.BlockSpec((1, bw, tm), lambda i: (i // tps, 0, i % tps)),
                 pl.BlockSpec((1, tm, bw), lambda i: (i // tps, i % tps, 0)),
                 pl.BlockSpec((1, bw, tm), lambda i: (i // tps, 0, i % tps)),
                 pl.BlockSpec((tm, 5 * bw), lambda i: (i, 0))]
    out_shape = [jax.ShapeDtypeStruct((nb, bw, seq), BF16),
                 jax.ShapeDtypeStruct((nb, seq, bw), BF16),
                 jax.ShapeDtypeStruct((nb, bw, seq), BF16),
                 jax.ShapeDtypeStruct((n, 5 * bw), BF16)]
    return pl.pallas_call(
        functools.partial(_inproj_even_kernel, rope=rope, tm=tm),
        grid=(n // tm,), in_specs=in_specs, out_specs=out_specs, out_shape=out_shape,
        compiler_params=_params(1), name="inproj_even",
    )(*args)


def _inproj_odd_kernel(x_ref, mod_ref, g_ref, w_ref, cs_ref, q_ref, k_ref, v_ref, rest_ref, p_ref, pq_ref):
    bw = q_ref.shape[2]
    hb = _prenorm(x_ref, mod_ref, g_ref)

    def proj(c):
        return jnp.dot(hb, w_ref[:, c * bw:(c + 1) * bw], preferred_element_type=F32)

    q_ref[0] = (proj(0) * (C_HEAD_DIM ** -0.5 * LOG2E)).astype(BF16)
    k_ref[0] = proj(1).astype(BF16)
    v_ref[0] = proj(2).astype(BF16)
    rest_ref[:, 0:bw] = proj(3).astype(BF16)
    rest_ref[:, bw:2 * bw] = proj(5).astype(BF16)
    pq = jnp.dot(proj(4).astype(BF16), cs_ref[...], preferred_element_type=F32)
    p_ref[0] = pq[:, 0:bw].astype(BF16)
    pq_ref[0] = pq[:, bw:2 * bw].astype(BF16)


def _inproj_odd(x2, mod3, mod_row_fn, g_pre, w_bf, cs_bf, seq, tm):
    n, d = x2.shape
    bw = w_bf.shape[1] // 6
    nb = n // seq
    tps = seq // tm
    const = lambda i: (0, 0)
    tok = pl.BlockSpec((1, tm, bw), lambda i: (i // tps, i % tps, 0))
    tok_shape = jax.ShapeDtypeStruct((nb, seq, bw), BF16)
    return pl.pallas_call(
        _inproj_odd_kernel,
        grid=(n // tm,),
        in_specs=[pl.BlockSpec((tm, d), lambda i: (i, 0)),
                  pl.BlockSpec((1, 1, mod3.shape[2]), lambda i: (mod_row_fn(i // tps), 0, 0)),
                  pl.BlockSpec((1, d), const),
                  pl.BlockSpec(w_bf.shape, const),
                  pl.BlockSpec(cs_bf.shape, const)],
        out_specs=[tok, tok, tok, pl.BlockSpec((tm, 2 * bw), lambda i: (i, 0)), tok, tok],
        out_shape=[tok_shape, tok_shape, tok_shape, jax.ShapeDtypeStruct((n, 2 * bw), BF16),
                   tok_shape, tok_shape],
        compiler_params=_params(1), name="inproj_odd",
    )(x2, mod3, g_pre.reshape(1, d), w_bf, cs_bf)


def _diff_attn_kernel(*refs, has_main, tq, lam_init):
    if has_main:
        (lam_ref, sub_ref, qt_ref, kc_ref, vct_ref, ka_ref, vat_ref, ga_ref, o_ref, sbuf) = refs
        key_refs = [(kc_ref, vct_ref), (ka_ref, vat_ref)]
    else:
        (lam_ref, sub_ref, qt_ref, kc_ref, vct_ref, ga_ref, o_ref, sbuf) = refs
        key_refs = [(kc_ref, vct_ref)]
    chunks = []
    off = 0
    for k_ref, vt_ref in key_refs:
        for st in range(0, k_ref.shape[1], KEY_CHUNK):
            sz = min(KEY_CHUNK, k_ref.shape[1] - st)
            chunks.append((k_ref, vt_ref, st, sz, off))
            off += sz
    lp = lam_ref[...]
    lam = (jnp.exp(jnp.sum(lp[0:1] * lp[1:2], axis=1, keepdims=True))
           - jnp.exp(jnp.sum(lp[2:3] * lp[3:4], axis=1, keepdims=True)) + lam_init)
    sq = min(tq, DIFF_SUB_Q)
    qf = qt_ref[0].astype(F32)
    rowid = lax.broadcasted_iota(jnp.int32, (qf.shape[0], sq), 0)
    sub = jnp.tile(sub_ref[...], (1, sq // LANES))

    def fold(v):
        return v.reshape(v.shape[0] // 8, 8, sq)

    items = [(j, hh, m) for j in range(tq // sq) for hh in range(2) for m in range(2)]
    qms, mxs, mrows, accs, ls, maps = {}, {}, {}, {}, {}, {}
    for stage in range(len(items) + 1):
        t, p = stage, stage - 1
        if t < len(items):
            j, hh, m = items[t]
            lo = hh * A_V + m * A_QK
            qms[t] = jnp.where((rowid >= lo) & (rowid < lo + A_QK), qf[:, j * sq:(j + 1) * sq],
                               0.0).astype(BF16)
        if p >= 0:
            accs[p] = jnp.zeros((A_V, sq), F32)
            ls[p] = jnp.zeros((8, sq), F32)
        for k_ref, vt_ref, st, sz, off in chunks:
            if t < len(items):
                s = jnp.dot(k_ref[0, st:st + sz, :], qms[t], preferred_element_type=F32)
                sbuf[t % 2, off:off + sz, :] = s
                cm = jnp.max(fold(s), axis=0)
                mxs[t] = cm if t not in mxs else jnp.maximum(mxs[t], cm)
            if p >= 0:
                hh = items[p][1]
                e = jnp.exp2(sbuf[p % 2, off:off + sz, :] - mrows[p])
                ls[p] = ls[p] + jnp.sum(fold(e), axis=0)
                accs[p] = accs[p] + jnp.dot(vt_ref[0, hh * A_V:(hh + 1) * A_V, st:st + sz],
                                            e.astype(BF16), preferred_element_type=F32)
        if t < len(items):
            mrows[t] = jnp.max(mxs.pop(t), axis=0, keepdims=True)
            del qms[t]
        if p >= 0:
            maps[p] = accs.pop(p) * (1.0 / jnp.sum(ls.pop(p), axis=0, keepdims=True))
    for j in range(tq // sq):
        heads = []
        for hh in range(2):
            ot = maps[4 * j + 2 * hh] - lam * maps[4 * j + 2 * hh + 1]
            ot = ot * lax.rsqrt(jnp.mean(ot * ot, axis=0, keepdims=True) + EPS) * sub * (1.0 - lam_init)
            heads.append(ot)
        o = jnp.concatenate(heads, axis=0).T
        rows = slice(j * sq, (j + 1) * sq)
        o_ref[rows, :] = o * _silu(ga_ref[rows, :].astype(F32))


def _diff_attn(lam_p, subln, qt, kc, vct, ka, vat, rest, lam_init, tq):
    nb, bw, sq = qt.shape
    nq = sq // tq
    n_keys = kc.shape[1] + (0 if ka is None else ka.shape[1])
    hp = HEAD_PAIR_W
    sub_b = jnp.broadcast_to(subln.reshape(A_V, 1), (A_V, LANES))
    in_specs = [pl.BlockSpec(lam_p.shape, lambda b, h, i: (0, 0)),
                pl.BlockSpec(sub_b.shape, lambda b, h, i: (0, 0)),
                pl.BlockSpec((1, hp, tq), lambda b, h, i: (b, h, i)),
                pl.BlockSpec((1, kc.shape[1], hp), lambda b, h, i: (b, 0, h)),
                pl.BlockSpec((1, hp, kc.shape[1]), lambda b, h, i: (b, h, 0))]
    args = [lam_p, sub_b, qt, kc, vct]
    if ka is not None:
        in_specs += [pl.BlockSpec((1, ka.shape[1], hp), lambda b, h, i: (b, 0, h)),
                     pl.BlockSpec((1, hp, ka.shape[1]), lambda b, h, i: (b, h, 0))]
        args += [ka, vat]
    in_specs.append(pl.BlockSpec((tq, hp), lambda b, h, i: (b * nq + i, h)))
    args.append(rest)
    return pl.pallas_call(
        functools.partial(_diff_attn_kernel, has_main=ka is not None, tq=tq, lam_init=lam_init),
        grid=(nb, bw // hp, nq),
        in_specs=in_specs,
        out_specs=pl.BlockSpec((tq, hp), lambda b, h, i: (b * nq + i, h)),
        out_shape=jax.ShapeDtypeStruct((nb * sq, bw), F32),
        scratch_shapes=[pltpu.VMEM((2, n_keys, min(tq, DIFF_SUB_Q)), F32)],
        compiler_params=_params(3), name="diff_attn",
    )(*args)


NA_QROWS = 4
NA_KROWS = 12


def _softmax_pv(q, pieces, gate_ref, o_ref):
    lane = lax.broadcasted_iota(jnp.int32, q.shape, 1)
    qf = q.astype(F32)
    outs = []
    for hh in range(2):
        qm = jnp.where((lane >= hh * C_HEAD_DIM) & (lane < (hh + 1) * C_HEAD_DIM), qf, 0.0).astype(BF16)
        ss = []
        for k, _, bias in pieces:
            s = _nt_dot(qm, k)
            if bias is not None:
                s = s + bias[hh]
            ss.append(s)
        m = functools.reduce(jnp.maximum, [jnp.max(s, axis=1, keepdims=True) for s in ss])
        es = [jnp.exp2(s - m) for s in ss]
        l = functools.reduce(jnp.add, [jnp.sum(e, axis=1, keepdims=True) for e in es])
        pv = functools.reduce(jnp.add, [jnp.dot(e.astype(BF16), v, preferred_element_type=F32)
                                        for e, (_, v, _) in zip(es, pieces)])
        outs.append(pv * (1.0 / l))
    o = jnp.where(lane < C_HEAD_DIM, outs[0], outs[1])
    o_ref[...] = o * _silu(gate_ref[...].astype(F32))


def _na_kernel(q_ref, k_ref, v_ref, kc_ref, vc_ref, bias_ref, gate_ref, o_ref, *, rows):
    g = pl.program_id(2)
    base = jnp.clip(NA_QROWS * g - NA_KH // 2, 0, rows - NA_KROWS)
    start = pl.multiple_of(base * GRID_W, NA_QROWS * GRID_W)
    nk = NA_KROWS * GRID_W
    kw = k_ref[0, pl.ds(start, nk), :]
    vw = v_ref[0, pl.ds(start, nk), :]

    class _Bias:
        def __getitem__(self, hh):
            return bias_ref[0, hh]

    _softmax_pv(q_ref[0], [(kw, vw, _Bias()), (kc_ref[0], vc_ref[0], None)], gate_ref, o_ref)


def _na_case_tables(rows):
    rs = np.clip(np.arange(rows) - NA_KH // 2, 0, rows - NA_KH)
    n_groups = rows // NA_QROWS
    pats = []
    for g in range(n_groups):
        base = int(np.clip(NA_QROWS * g - NA_KH // 2, 0, rows - NA_KROWS))
        pat = np.full((NA_QROWS, NA_KROWS), 2 * NA_KH - 1, np.int32)
        for j in range(NA_QROWS):
            r = NA_QROWS * g + j
            for i in range(NA_KROWS):
                kr = base + i
                if rs[r] <= kr < rs[r] + NA_KH:
                    pat[j, i] = kr - r + NA_KH - 1
            assert (pat[j] != 2 * NA_KH - 1).sum() == NA_KH
        pats.append(pat)
    cases = np.stack([pats[0], pats[1], pats[-1]])
    for g in range(n_groups):
        want = 0 if g == 0 else (2 if g == n_groups - 1 else 1)
        assert (pats[g] == cases[want]).all()
    return cases


def _na_bias(rpb, rows):
    h = rpb.shape[0]
    cq = np.arange(GRID_W)
    cs = np.clip(cq - NA_KW // 2, 0, GRID_W - NA_KW)
    kcol = np.arange(GRID_W)
    col_ok = (kcol[None, :] >= cs[:, None]) & (kcol[None, :] < cs[:, None] + NA_KW)
    col_off = np.clip(kcol[None, :] - cq[:, None] + NA_KW - 1, 0, 2 * NA_KW - 2)
    toep = jnp.where(col_ok[None, None], rpb[:, :, col_off] * LOG2E, NEG_BIG)
    toep = jnp.concatenate([toep, jnp.full((h, 1, GRID_W, GRID_W), NEG_BIG, F32)], axis=1)
    cases = _na_case_tables(rows)
    t = toep[:, cases]
    t = jnp.transpose(t, (1, 0, 2, 4, 3, 5))
    return t.reshape(3, h, NA_QROWS * GRID_W, NA_KROWS * GRID_W)


def _na_attn(q, k, v, kc, vc, bias, rest):
    nb, s, bw = q.shape
    hp = HEAD_PAIR_W
    tq = NA_QROWS * GRID_W
    ng = s // tq
    l = kc.shape[1]
    case = lambda g: jnp.where(g == 0, 0, jnp.where(g == ng - 1, 2, 1))
    return pl.pallas_call(
        functools.partial(_na_kernel, rows=s // GRID_W),
        grid=(nb, bw // hp, ng),
        in_specs=[pl.BlockSpec((1, tq, hp), lambda b, h, g: (b, g, h)),
                  pl.BlockSpec((1, s, hp), lambda b, h, g: (b, 0, h)),
                  pl.BlockSpec((1, s, hp), lambda b, h, g: (b, 0, h)),
                  pl.BlockSpec((1, l, hp), lambda b, h, g: (b, 0, h)),
                  pl.BlockSpec((1, l, hp), lambda b, h, g: (b, 0, h)),
                  pl.BlockSpec((1, 2, tq, NA_KROWS * GRID_W), lambda b, h, g: (case(g), h, 0, 0)),
                  pl.BlockSpec((tq, hp), lambda b, h, g: (b * ng + g, h))],
        out_specs=pl.BlockSpec((tq, hp), lambda b, h, g: (b * ng + g, h)),
        out_shape=jax.ShapeDtypeStruct((nb * s, bw), F32),
        compiler_params=_params(3), name="nbr_attn",
    )(q, k, v, kc, vc, bias, rest)


def _dense_attn_kernel(q_ref, k_ref, v_ref, gate_ref, o_ref):
    _softmax_pv(q_ref[0], [(k_ref[0], v_ref[0], None)], gate_ref, o_ref)


def _dense_attn(q, k, v, rest):
    nb, l, bw = q.shape
    hp = HEAD_PAIR_W
    tok = pl.BlockSpec((1, l, hp), lambda b, h: (b, 0, h))
    return pl.pallas_call(
        _dense_attn_kernel,
        grid=(nb, bw // hp),
        in_specs=[tok, tok, tok, pl.BlockSpec((l, hp), lambda b, h: (b, h))],
        out_specs=pl.BlockSpec((l, hp), lambda b, h: (b, h)),
        out_shape=jax.ShapeDtypeStruct((nb * l, bw), F32),
        compiler_params=_params(2), name="ctx_dense_attn",
    )(q, k, v, rest)


def _dft_kernel(c_ref, s_ref, p_ref, q_ref, o_ref):
    o_ref[0] = (jnp.dot(c_ref[...], p_ref[0], preferred_element_type=F32)
                + jnp.dot(s_ref[...], q_ref[0], preferred_element_type=F32))


def _dft_tables(n):
    idx = np.arange(n)
    ang = ((idx[:, None] * idx[None, :]) % n) * (2.0 * np.pi / n)
    scale = n ** -0.5
    return jnp.asarray(np.cos(ang) * scale, BF16), jnp.asarray(np.sin(ang) * scale, BF16)


def _dft_rows_kernel(m1_ref, m2_ref, vr_ref, vi_ref, a_ref):
    a_ref[0] = (jnp.dot(m1_ref[...], vr_ref[0], preferred_element_type=F32)
                + jnp.dot(m2_ref[...], vi_ref[0], preferred_element_type=F32)).astype(BF16)


def _dft_cols_kernel(g_ref, a_ref, o_ref, *, k1t, bw):
    for j in range(k1t):
        a = jnp.concatenate([a_ref[0, 0, j], a_ref[0, 1, j]], axis=0)
        o_ref[0, :, j * bw:(j + 1) * bw] = jnp.dot(g_ref[j], a, preferred_element_type=F32)


def _dft_factored_tables(n):
    r = n // GRID_W
    i = np.arange(r)
    ang = ((i[:, None] * i[None, :]) % r) * (2.0 * np.pi / r)
    c, s = np.cos(ang) * r ** -0.5, np.sin(ang) * r ** -0.5
    m1 = np.concatenate([c, -s], axis=0)
    m2 = np.concatenate([s, c], axis=0)
    k = i[:, None, None] + r * np.arange(GRID_W)[None, :, None]
    th = ((k * np.arange(GRID_W)[None, None, :]) % n) * (2.0 * np.pi / n)
    g = np.concatenate([np.cos(th), np.sin(th)], axis=2) * GRID_W ** -0.5
    return jnp.asarray(m1, BF16), jnp.asarray(m2, BF16), jnp.asarray(g, BF16)


def _dft_positions_factored(tabs, p, q):
    m1, m2, g = tabs
    nb, n, bw = p.shape
    r = n // GRID_W
    cols = GRID_W * bw
    ct = 4096
    a = pl.pallas_call(
        _dft_rows_kernel,
        grid=(nb, cols // ct),
        in_specs=[pl.BlockSpec(m1.shape, lambda b, c: (0, 0)),
                  pl.BlockSpec(m2.shape, lambda b, c: (0, 0)),
                  pl.BlockSpec((1, r, ct), lambda b, c: (b, 0, c)),
                  pl.BlockSpec((1, r, ct), lambda b, c: (b, 0, c))],
        out_specs=pl.BlockSpec((1, 2 * r, ct), lambda b, c: (b, 0, c)),
        out_shape=jax.ShapeDtypeStruct((nb, 2 * r, cols), BF16),
        compiler_params=_params(2), name="dft_rows",
    )(m1, m2, p.reshape(nb, r, cols), q.reshape(nb, r, cols))
    k1t = 8
    y = pl.pallas_call(
        functools.partial(_dft_cols_kernel, k1t=k1t, bw=bw),
        grid=(nb, r // k1t),
        in_specs=[pl.BlockSpec((k1t, GRID_W, 2 * GRID_W), lambda b, i: (i, 0, 0)),
                  pl.BlockSpec((1, 2, k1t, GRID_W, bw), lambda b, i: (b, 0, i, 0, 0))],
        out_specs=pl.BlockSpec((1, GRID_W, k1t * bw), lambda b, i: (b, 0, i)),
        out_shape=jax.ShapeDtypeStruct((nb, GRID_W, r * bw), F32),
        compiler_params=_params(2), name="dft_cols",
    )(g, a.reshape(nb, 2, r, GRID_W, bw))
    return y.reshape(nb * n, bw)


def _dft_positions(cn, sn, p, q, tmo):
    nb, n, bw = p.shape
    return pl.pallas_call(
        _dft_kernel,
        grid=(n // tmo, nb),
        in_specs=[pl.BlockSpec((tmo, n), lambda i, b: (i, 0)),
                  pl.BlockSpec((tmo, n), lambda i, b: (i, 0)),
                  pl.BlockSpec((1, n, bw), lambda i, b: (b, 0, 0)),
                  pl.BlockSpec((1, n, bw), lambda i, b: (b, 0, 0))],
        out_specs=pl.BlockSpec((1, tmo, bw), lambda i, b: (b, i, 0)),
        out_shape=jax.ShapeDtypeStruct((nb, n, bw), F32),
        compiler_params=_params(2), name="dft_positions",
    )(cn, sn, p, q)


def _channel_dft_matrix():
    j = np.arange(D_GROUP_W)
    ang = 2.0 * np.pi * ((j[:, None] * j[None, :]) % D_GROUP_W) / D_GROUP_W
    eye = np.eye(D_GROUPS)
    cc = np.kron(eye, np.cos(ang) * D_GROUP_W ** -0.5)
    sc = np.kron(eye, np.sin(ang) * D_GROUP_W ** -0.5)
    return jnp.asarray(np.concatenate([cc, -sc], axis=1), BF16)


def _post_kernel(*refs, even, tm, seq):
    if even:
        (x_ref, a_ref, bb_ref, cb_ref, ub_ref, gb_ref, cbp_ref, ubp_ref, cbn_ref, ubn_ref,
         cw_ref, mod_ref, g_ref, w_ref, o_ref) = refs
    else:
        (x_ref, a_ref, y_ref, gd_ref, mod_ref, g_ref, w_ref, o_ref) = refs
    d = x_ref.shape[-1]
    bw = a_ref.shape[-1]
    if even:
        i = pl.program_id(0)
        t0 = (i * tm) % seq
        z = cb_ref[...].astype(F32) * ub_ref[...].astype(F32)
        zp = cbp_ref[7:8, :].astype(F32) * ubp_ref[7:8, :].astype(F32) * jnp.where(t0 == 0, 0.0, 1.0)
        zn = cbn_ref[0:1, :].astype(F32) * ubn_ref[0:1, :].astype(F32) * jnp.where(t0 + tm == seq, 0.0, 1.0)
        rid = lax.broadcasted_iota(jnp.int32, z.shape, 0)
        z_prev = jnp.where(rid == 0, zp, pltpu.roll(z, 1, 0))
        z_next = jnp.where(rid == tm - 1, zn, pltpu.roll(z, tm - 1, 0))
        cw = cw_ref[...]
        y = cw[0:1] * z_prev + cw[1:2] * z + cw[2:3] * z_next
        second = bb_ref[...].astype(F32) * y * _silu(gb_ref[...].astype(F32))
    else:
        second = y_ref[...] * _silu(gd_ref[...].astype(F32))
    yo = (jnp.dot(a_ref[...].astype(BF16), w_ref[0:bw, :], preferred_element_type=F32)
          + jnp.dot(second.astype(BF16), w_ref[bw:2 * bw, :], preferred_element_type=F32))
    nrm = yo * lax.rsqrt(jnp.mean(yo * yo, axis=-1, keepdims=True) + EPS) * g_ref[...]
    o_ref[...] = x_ref[...] + mod_ref[0][:, 2 * d:3 * d] * nrm


def _post(x2, a, second, rest, conv_w, mod3, mod_row_fn, g_post, w_out_bf, seq, tm, even):
    n, d = x2.shape
    bw = a.shape[1]
    tps = seq // tm
    const = lambda i: (0, 0)
    row = lambda i: (i, 0)
    in_specs = [pl.BlockSpec((tm, d), row), pl.BlockSpec((tm, bw), row)]
    args = [x2, a]
    if even:
        nblk8 = n // 8
        for c in (1, 2, 3, 4):
            in_specs.append(pl.BlockSpec((tm, bw), lambda i, c=c: (i, c)))
            args.append(rest)
        prev = lambda i, c: (jnp.maximum(i * (tm // 8) - 1, 0), c)
        nxt = lambda i, c: (jnp.minimum((i + 1) * (tm // 8), nblk8 - 1), c)
        for fn in (prev, nxt):
            for c in (2, 3):
                in_specs.append(pl.BlockSpec((8, bw), functools.partial(fn, c=c)))
                args.append(rest)
        in_specs.append(pl.BlockSpec(conv_w.shape, const))
        args.append(conv_w)
    else:
        in_specs += [pl.BlockSpec((tm, bw), row), pl.BlockSpec((tm, bw), lambda i: (i, 1))]
        args += [second, rest]
    in_specs += [pl.BlockSpec((1, 1, mod3.shape[2]), lambda i: (mod_row_fn(i // tps), 0, 0)),
                 pl.BlockSpec((1, d), const),
                 pl.BlockSpec(w_out_bf.shape, const)]
    args += [mod3, g_post.reshape(1, d), w_out_bf]
    return pl.pallas_call(
        functools.partial(_post_kernel, even=even, tm=tm, seq=seq),
        grid=(n // tm,), in_specs=in_specs,
        out_specs=pl.BlockSpec((tm, d), row),
        out_shape=jax.ShapeDtypeStruct((n, d), F32),
        compiler_params=_params(1), name="post_even" if even else "post_odd",
    )(*args)


def _rope_tables(seq):
    t = np.arange(seq)
    row = (t // GRID_W).astype(np.float32)
    col = (t % GRID_W).astype(np.float32)
    nf = A_QK // 4
    inv_freq = (ROPE_BASE ** (-np.arange(nf, dtype=np.float32) / nf)).astype(np.float32)
    p = np.arange(A_QK)
    f_idx = p % nf
    use_col = (p % A_QK) >= A_QK // 2
    sign = np.where((p % (2 * nf)) < nf, -1.0, 1.0)
    pos = np.where(use_col[None, :], col[:, None], row[:, None])
    ang = (pos * inv_freq[f_idx][None, :]).astype(np.float32).astype(np.float64)
    cos = np.cos(ang).astype(np.float32)
    sin = (np.sin(ang) * sign[None, :]).astype(np.float32)
    reps = LANES // A_QK
    return (jnp.asarray(np.tile(cos, (1, reps))), jnp.asarray(np.tile(sin, (1, reps))),
            jnp.asarray(np.ascontiguousarray(cos.T)), jnp.asarray(np.ascontiguousarray(sin.T)))


def _lambda_init(layer):
    return 0.8 - 0.6 * math.exp(-0.3 * layer)


def kernel(x, c, ctx, c_ctx, w_mod, b_mod, norm_pre, norm_post, w_in_even, lam_a, subln_a, conv_b,
           w_in_odd, rpb_c, w_out):
    nb, seq, d = x.shape
    lctx = ctx.shape[1]
    depth = w_mod.shape[0]
    assert seq % (NA_QROWS * GRID_W) == 0 and seq // GRID_W >= NA_KROWS
    assert lctx % LANES == 0 and seq % KEY_CHUNK == 0 and nb < 16

    pad = jnp.zeros((16 - nb - 1, d), F32)
    cc = jnp.concatenate([c, c_ctx[None, :], pad], axis=0)
    mods = _mod_all(cc, w_mod, b_mod)

    rope_tabs = _rope_tables(seq)
    cs_bf = _channel_dft_matrix()
    dft_tabs = _dft_factored_tables(seq)
    cn_c, sn_c = _dft_tables(lctx)

    tm_x = 512
    tm_c = lctx
    x_row = lambda b: b
    c_row = lambda b: nb

    x2 = x.reshape(nb * seq, d)
    c2 = ctx.reshape(nb * lctx, d)
    for l in range(depth):
        need_ctx = l < depth - 1
        j = l // 2
        mod3 = mods[l].reshape(16, 1, 3 * d)
        w_out_bf = w_out[l].astype(BF16)
        if l % 2 == 0:
            w_bf = w_in_even[j].astype(BF16)
            li = _lambda_init(l)
            qt, k, vt, rest = _inproj_even(x2, mod3, x_row, norm_pre[l], w_bf, seq, rope_tabs, tm_x)
            qct, kc, vct, rest_c = _inproj_even(c2, mod3, c_row, norm_pre[l], w_bf, lctx, None, tm_c)
            a = _diff_attn(lam_a[j], subln_a[j], qt, kc, vct, k, vt, rest, li, 512)
            x2n = _post(x2, a, None, rest, conv_b[j], mod3, x_row, norm_post[l], w_out_bf, seq, tm_x, True)
            if need_ctx:
                ac = _diff_attn(lam_a[j], subln_a[j], qct, kc, vct, None, None, rest_c, li, lctx)
                c2 = _post(c2, ac, None, rest_c, conv_b[j], mod3, c_row, norm_post[l], w_out_bf,
                           lctx, tm_c, True)
            x2 = x2n
        else:
            w_bf = w_in_odd[j].astype(BF16)
            q, k, v, rest, p, pq = _inproj_odd(x2, mod3, x_row, norm_pre[l], w_bf, cs_bf, seq, tm_x)
            qc, kc, vc, rest_c, p_c, pq_c = _inproj_odd(c2, mod3, c_row, norm_pre[l], w_bf, cs_bf, lctx, tm_c)
            bias = _na_bias(rpb_c[j], seq // GRID_W)
            a = _na_attn(q, k, v, kc, vc, bias, rest)
            y = _dft_positions_factored(dft_tabs, p, pq)
            x2n = _post(x2, a, y, rest, None, mod3, x_row, norm_post[l], w_out_bf, seq, tm_x, False)
            if need_ctx:
                ac = _dense_attn(qc, kc, vc, rest_c)
                yc = _dft_positions(cn_c, sn_c, p_c, pq_c, lctx).reshape(nb * lctx, -1)
                c2 = _post(c2, ac, yc, rest_c, None, mod3, c_row, norm_post[l], w_out_bf, lctx, tm_c, False)
            x2 = x2n
    return x2.reshape(nb, seq, d)
```

```python
import functools
import math

import numpy as np
import jax
import jax.numpy as jnp
from jax import lax
from jax.experimental import pallas as pl
from jax.experimental.pallas import tpu as pltpu

GRID_W = 64
A_HEADS = 8
A_QK = 32
A_V = 64
C_HEADS = 8
C_HEAD_DIM = 64
NA_KH = 8
NA_KW = 16
D_GROUPS = 8
D_GROUP_W = 64
CONV_W = 3
ROPE_BASE = 10000.0
EPS = 1e-6
LOG2E = 1.4426950408889634
NEG_BIG = -1e30

LANES = 128
HEAD_PAIR_W = 128
KEY_CHUNK = 512
DIFF_SUB_Q = 512
VMEM_LIMIT = 48 * 1024 * 1024

F32 = jnp.float32
BF16 = jnp.bfloat16


def _silu(v):
    return v * jax.nn.sigmoid(v)


def _params(n_axes):
    return pltpu.CompilerParams(dimension_semantics=("arbitrary",) * n_axes,
                                vmem_limit_bytes=VMEM_LIMIT)


def _nt_dot(a, b):
    return lax.dot_general(a, b, (((1,), (1,)), ((), ())), preferred_element_type=F32)


def _mod_kernel(cc_ref, w_ref, b_ref, o_ref):
    s = _silu(cc_ref[...]).astype(BF16)
    o_ref[0] = jnp.dot(s, w_ref[0].astype(BF16), preferred_element_type=F32) + b_ref[0]


def _mod_all(cc, w_mod, b_mod):
    depth, d, d3 = w_mod.shape
    tn = d
    return pl.pallas_call(
        _mod_kernel,
        grid=(depth, d3 // tn),
        in_specs=[pl.BlockSpec((cc.shape[0], d), lambda l, n: (0, 0)),
                  pl.BlockSpec((1, d, tn), lambda l, n: (l, 0, n)),
                  pl.BlockSpec((1, 1, tn), lambda l, n: (l, 0, n))],
        out_specs=pl.BlockSpec((1, cc.shape[0], tn), lambda l, n: (l, 0, n)),
        out_shape=jax.ShapeDtypeStruct((depth, cc.shape[0], d3), F32),
        compiler_params=_params(2), name="adaln_mod",
    )(cc, w_mod, b_mod.reshape(depth, 1, d3))


def _prenorm(x_ref, mod_ref, g_ref):
    d = x_ref.shape[-1]
    x = x_ref[...]
    y = x * lax.rsqrt(jnp.mean(x * x, axis=-1, keepdims=True) + EPS) * g_ref[...]
    m = mod_ref[0]
    h = y * (1.0 + m[:, d:2 * d]) + m[:, 0:d]
    return h.astype(BF16)


def _inproj_even_kernel(*refs, rope, tm):
    if rope:
        (x_ref, mod_ref, g_ref, w_ref, wqt_ref, wvt_ref,
         cos_ref, sin_ref, cost_ref, sint_ref, qt_ref, k_ref, vt_ref, rest_ref) = refs
    else:
        (x_ref, mod_ref, g_ref, w_ref, wqt_ref, wvt_ref, qt_ref, k_ref, vt_ref, rest_ref) = refs
    bw = qt_ref.shape[1]
    hb = _prenorm(x_ref, mod_ref, g_ref)
    pair = A_QK // 4

    def rotary(u, axis, cos, sin):
        idx = lax.broadcasted_iota(jnp.int32, u.shape, axis)
        partner = jnp.where((idx & pair) == 0, pltpu.roll(u, bw - pair, axis), pltpu.roll(u, pair, axis))
        return u * cos + partner * sin

    qt = _nt_dot(wqt_ref[...], hb)
    if rope:
        reps = bw // cost_ref.shape[0]
        qt = rotary(qt, 0, jnp.tile(cost_ref[...], (reps, 1)), jnp.tile(sint_ref[...], (reps, 1)))
    qt_ref[0] = (qt * (A_QK ** -0.5 * LOG2E)).astype(BF16)

    k = jnp.dot(hb, w_ref[:, bw:2 * bw], preferred_element_type=F32)
    if rope:
        reps = bw // cos_ref.shape[1]
        k = rotary(k, 1, jnp.tile(cos_ref[...], (1, reps)), jnp.tile(sin_ref[...], (1, reps)))
    k_ref[0] = k.astype(BF16)

    vt_ref[0] = _nt_dot(wvt_ref[...], hb).astype(BF16)

    n_rest = rest_ref.shape[1] // bw
    for c in range(n_rest):
        col = (3 + c) * bw
        rest_ref[:, c * bw:(c + 1) * bw] = jnp.dot(
            hb, w_ref[:, col:col + bw], preferred_element_type=F32).astype(BF16)


def _inproj_even(x2, mod3, mod_row_fn, g_pre, w_bf, seq, rope_tabs, tm):
    n, d = x2.shape
    bw = w_bf.shape[1] // 8
    nb = n // seq
    tps = seq // tm
    rope = rope_tabs is not None
    wqt = w_bf[:, 0:bw].T
    wvt = w_bf[:, 2 * bw:3 * bw].T
    const = lambda i: (0, 0)
    in_specs = [pl.BlockSpec((tm, d), lambda i: (i, 0)),
                pl.BlockSpec((1, 1, mod3.shape[2]), lambda i: (mod_row_fn(i // tps), 0, 0)),
                pl.BlockSpec((1, d), const),
                pl.BlockSpec(w_bf.shape, const),
                pl.BlockSpec(wqt.shape, const),
                pl.BlockSpec(wvt.shape, const)]
    args = [x2, mod3, g_pre.reshape(1, d), w_bf, wqt, wvt]
    if rope:
        cos_n, sin_n, cos_t, sin_t = rope_tabs
        in_specs += [pl.BlockSpec((tm, cos_n.shape[1]), lambda i: (i % tps, 0)),
                     pl.BlockSpec((tm, cos_n.shape[1]), lambda i: (i % tps, 0)),
                     pl.BlockSpec((cos_t.shape[0], tm), lambda i: (0, i % tps)),
                     pl.BlockSpec((cos_t.shape[0], tm), lambda i: (0, i % tps))]
        args += [cos_n, sin_n, cos_t, sin_t]
    out_specs = [pl.BlockSpec((1, bw, tm), lambda i: (i // tps, 0, i % tps)),
                 pl.BlockSpec((1, tm, bw), lambda i: (i // tps, i % tps, 0)),
                 pl.BlockSpec((1, bw, tm), lambda i: (i // tps, 0, i % tps)),
                 pl.BlockSpec((tm, 5 * bw), lambda i: (i, 0))]
    out_shape = [jax.ShapeDtypeStruct((nb, bw, seq), BF16),
                 jax.ShapeDtypeStruct((nb, seq, bw), BF16),
                 jax.ShapeDtypeStruct((nb, bw, seq), BF16),
                 jax.ShapeDtypeStruct((n, 5 * bw), BF16)]
    return pl.pallas_call(
        functools.partial(_inproj_even_kernel, rope=rope, tm=tm),
        grid=(n // tm,), in_specs=in_specs, out_specs=out_specs, out_shape=out_shape,
        compiler_params=_params(1), name="inproj_even",
    )(*args)


def _inproj_odd_kernel(x_ref, mod_ref, g_ref, w_ref, cs_ref, q_ref, k_ref, v_ref, rest_ref, p_ref, pq_ref):
    bw = q_ref.shape[2]
    hb = _prenorm(x_ref, mod_ref, g_ref)

    def proj(c):
        return jnp.dot(hb, w_ref[:, c * bw:(c + 1) * bw], preferred_element_type=F32)

    q_ref[0] = (proj(0) * (C_HEAD_DIM ** -0.5 * LOG2E)).astype(BF16)
    k_ref[0] = proj(1).astype(BF16)
    v_ref[0] = proj(2).astype(BF16)
    rest_ref[:, 0:bw] = proj(3).astype(BF16)
    rest_ref[:, bw:2 * bw] = proj(5).astype(BF16)
    pq = jnp.dot(proj(4).astype(BF16), cs_ref[...], preferred_element_type=F32)
    p_ref[0] = pq[:, 0:bw].astype(BF16)
    pq_ref[0] = pq[:, bw:2 * bw].astype(BF16)


def _inproj_odd(x2, mod3, mod_row_fn, g_pre, w_bf, cs_bf, seq, tm):
    n, d = x2.shape
    bw = w_bf.shape[1] // 6
    nb = n // seq
    tps = seq // tm
    const = lambda i: (0, 0)
    tok = pl.BlockSpec((1, tm, bw), lambda i: (i // tps, i % tps, 0))
    tok_shape = jax.ShapeDtypeStruct((nb, seq, bw), BF16)
    return pl.pallas_call(
        _inproj_odd_kernel,
        grid=(n // tm,),
        in_specs=[pl.BlockSpec((tm, d), lambda i: (i, 0)),
                  pl.BlockSpec((1, 1, mod3.shape[2]), lambda i: (mod_row_fn(i // tps), 0, 0)),
                  pl.BlockSpec((1, d), const),
                  pl.BlockSpec(w_bf.shape, const),
                  pl.BlockSpec(cs_bf.shape, const)],
        out_specs=[tok, tok, tok, pl.BlockSpec((tm, 2 * bw), lambda i: (i, 0)), tok, tok],
        out_shape=[tok_shape, tok_shape, tok_shape, jax.ShapeDtypeStruct((n, 2 * bw), BF16),
                   tok_shape, tok_shape],
        compiler_params=_params(1), name="inproj_odd",
    )(x2, mod3, g_pre.reshape(1, d), w_bf, cs_bf)


def _diff_attn_kernel(*refs, has_main, tq, lam_init):
    if has_main:
        (lam_ref, sub_ref, qt_ref, kc_ref, vct_ref, ka_ref, vat_ref, ga_ref, o_ref, sbuf) = refs
        key_refs = [(kc_ref, vct_ref), (ka_ref, vat_ref)]
    else:
        (lam_ref, sub_ref, qt_ref, kc_ref, vct_ref, ga_ref, o_ref, sbuf) = refs
        key_refs = [(kc_ref, vct_ref)]
    chunks = []
    off = 0
    for k_ref, vt_ref in key_refs:
        for st in range(0, k_ref.shape[1], KEY_CHUNK):
            sz = min(KEY_CHUNK, k_ref.shape[1] - st)
            chunks.append((k_ref, vt_ref, st, sz, off))
            off += sz
    lp = lam_ref[...]
    lam = (jnp.exp(jnp.sum(lp[0:1] * lp[1:2], axis=1, keepdims=True))
           - jnp.exp(jnp.sum(lp[2:3] * lp[3:4], axis=1, keepdims=True)) + lam_init)
    sq = min(tq, DIFF_SUB_Q)
    qf = qt_ref[0].astype(F32)
    rowid = lax.broadcasted_iota(jnp.int32, (qf.shape[0], sq), 0)
    sub = jnp.tile(sub_ref[...], (1, sq // LANES))

    def fold(v):
        return v.reshape(v.shape[0] // 8, 8, sq)

    items = [(j, hh, m) for j in range(tq // sq) for hh in range(2) for m in range(2)]
    qms, mxs, mrows, accs, ls, maps = {}, {}, {}, {}, {}, {}
    for stage in range(len(items) + 1):
        t, p = stage, stage - 1
        if t < len(items):
            j, hh, m = items[t]
            lo = hh * A_V + m * A_QK
            qms[t] = jnp.where((rowid >= lo) & (rowid < lo + A_QK), qf[:, j * sq:(j + 1) * sq],
                               0.0).astype(BF16)
        if p >= 0:
            accs[p] = jnp.zeros((A_V, sq), F32)
            ls[p] = jnp.zeros((8, sq), F32)
        for k_ref, vt_ref, st, sz, off in chunks:
            if t < len(items):
                s = jnp.dot(k_ref[0, st:st + sz, :], qms[t], preferred_element_type=F32)
                sbuf[t % 2, off:off + sz, :] = s
                cm = jnp.max(fold(s), axis=0)
                mxs[t] = cm if t not in mxs else jnp.maximum(mxs[t], cm)
            if p >= 0:
                hh = items[p][1]
                e = jnp.exp2(sbuf[p % 2, off:off + sz, :] - mrows[p])
                ls[p] = ls[p] + jnp.sum(fold(e), axis=0)
                accs[p] = accs[p] + jnp.dot(vt_ref[0, hh * A_V:(hh + 1) * A_V, st:st + sz],
                                            e.astype(BF16), preferred_element_type=F32)
        if t < len(items):
            mrows[t] = jnp.max(mxs.pop(t), axis=0, keepdims=True)
            del qms[t]
        if p >= 0:
            maps[p] = accs.pop(p) * (1.0 / jnp.sum(ls.pop(p), axis=0, keepdims=True))
    for j in range(tq // sq):
        heads = []
        for hh in range(2):
            ot = maps[4 * j + 2 * hh] - lam * maps[4 * j + 2 * hh + 1]
            ot = ot * lax.rsqrt(jnp.mean(ot * ot, axis=0, keepdims=True) + EPS) * sub * (1.0 - lam_init)
            heads.append(ot)
        o = jnp.concatenate(heads, axis=0).T
        rows = slice(j * sq, (j + 1) * sq)
        o_ref[rows, :] = o * _silu(ga_ref[rows, :].astype(F32))


def _diff_attn(lam_p, subln, qt, kc, vct, ka, vat, rest, lam_init, tq):
    nb, bw, sq = qt.shape
    nq = sq // tq
    n_keys = kc.shape[1] + (0 if ka is None else ka.shape[1])
    hp = HEAD_PAIR_W
    sub_b = jnp.broadcast_to(subln.reshape(A_V, 1), (A_V, LANES))
    in_specs = [pl.BlockSpec(lam_p.shape, lambda b, h, i: (0, 0)),
                pl.BlockSpec(sub_b.shape, lambda b, h, i: (0, 0)),
                pl.BlockSpec((1, hp, tq), lambda b, h, i: (b, h, i)),
                pl.BlockSpec((1, kc.shape[1], hp), lambda b, h, i: (b, 0, h)),
                pl.BlockSpec((1, hp, kc.shape[1]), lambda b, h, i: (b, h, 0))]
    args = [lam_p, sub_b, qt, kc, vct]
    if ka is not None:
        in_specs += [pl.BlockSpec((1, ka.shape[1], hp), lambda b, h, i: (b, 0, h)),
                     pl.BlockSpec((1, hp, ka.shape[1]), lambda b, h, i: (b, h, 0))]
        args += [ka, vat]
    in_specs.append(pl.BlockSpec((tq, hp), lambda b, h, i: (b * nq + i, h)))
    args.append(rest)
    return pl.pallas_call(
        functools.partial(_diff_attn_kernel, has_main=ka is not None, tq=tq, lam_init=lam_init),
        grid=(nb, bw // hp, nq),
        in_specs=in_specs,
        out_specs=pl.BlockSpec((tq, hp), lambda b, h, i: (b * nq + i, h)),
        out_shape=jax.ShapeDtypeStruct((nb * sq, bw), F32),
        scratch_shapes=[pltpu.VMEM((2, n_keys, min(tq, DIFF_SUB_Q)), F32)],
        compiler_params=_params(3), name="diff_attn",
    )(*args)


NA_QROWS = 4
NA_KROWS = 12
NA_GROUPS_PER_STEP = 2


def _softmax_pv(q, pieces, gate):
    lane = lax.broadcasted_iota(jnp.int32, q.shape, 1)
    qf = q.astype(F32)
    outs = []
    for hh in range(2):
        qm = jnp.where((lane >= hh * C_HEAD_DIM) & (lane < (hh + 1) * C_HEAD_DIM), qf, 0.0).astype(BF16)
        ss = []
        for k, _, bias in pieces:
            s = _nt_dot(qm, k)
            if bias is not None:
                s = s + bias[hh]
            ss.append(s)
        m = functools.reduce(jnp.maximum, [jnp.max(s, axis=1, keepdims=True) for s in ss])
        es = [jnp.exp2(s - m) for s in ss]
        l = functools.reduce(jnp.add, [jnp.sum(e, axis=1, keepdims=True) for e in es])
        pv = functools.reduce(jnp.add, [jnp.dot(e.astype(BF16), v, preferred_element_type=F32)
                                        for e, (_, v, _) in zip(es, pieces)])
        outs.append(pv * (1.0 / l))
    o = jnp.where(lane < C_HEAD_DIM, outs[0], outs[1])
    return o * _silu(gate.astype(F32))


class _BiasOf:
    def __init__(self, ref):
        self.ref = ref

    def __getitem__(self, hh):
        return self.ref[0, hh]


def _na_kernel(q_ref, k_ref, v_ref, kc_ref, vc_ref, *rest, rows, n_sub):
    bias_refs, (gate_ref, o_ref) = rest[:n_sub], rest[n_sub:]
    tq = NA_QROWS * GRID_W
    nk = NA_KROWS * GRID_W
    for u in range(n_sub):
        g = pl.program_id(2) * n_sub + u
        base = jnp.clip(NA_QROWS * g - NA_KH // 2, 0, rows - NA_KROWS)
        start = pl.multiple_of(base * GRID_W, NA_QROWS * GRID_W)
        kw = k_ref[0, pl.ds(start, nk), :]
        vw = v_ref[0, pl.ds(start, nk), :]
        qrows = slice(u * tq, (u + 1) * tq)
        o_ref[qrows, :] = _softmax_pv(
            q_ref[0, qrows, :], [(kw, vw, _BiasOf(bias_refs[u])), (kc_ref[0], vc_ref[0], None)],
            gate_ref[qrows, :])


def _na_case_tables(rows):
    rs = np.clip(np.arange(rows) - NA_KH // 2, 0, rows - NA_KH)
    n_groups = rows // NA_QROWS
    pats = []
    for g in range(n_groups):
        base = int(np.clip(NA_QROWS * g - NA_KH // 2, 0, rows - NA_KROWS))
        pat = np.full((NA_QROWS, NA_KROWS), 2 * NA_KH - 1, np.int32)
        for j in range(NA_QROWS):
            r = NA_QROWS * g + j
            for i in range(NA_KROWS):
                kr = base + i
                if rs[r] <= kr < rs[r] + NA_KH:
                    pat[j, i] = kr - r + NA_KH - 1
            assert (pat[j] != 2 * NA_KH - 1).sum() == NA_KH
        pats.append(pat)
    cases = np.stack([pats[0], pats[1], pats[-1]])
    for g in range(n_groups):
        want = 0 if g == 0 else (2 if g == n_groups - 1 else 1)
        assert (pats[g] == cases[want]).all()
    return cases


def _na_bias(rpb, rows):
    h = rpb.shape[0]
    cq = np.arange(GRID_W)
    cs = np.clip(cq - NA_KW // 2, 0, GRID_W - NA_KW)
    kcol = np.arange(GRID_W)
    col_ok = (kcol[None, :] >= cs[:, None]) & (kcol[None, :] < cs[:, None] + NA_KW)
    col_off = np.clip(kcol[None, :] - cq[:, None] + NA_KW - 1, 0, 2 * NA_KW - 2)
    toep = jnp.where(col_ok[None, None], rpb[:, :, col_off] * LOG2E, NEG_BIG)
    toep = jnp.concatenate([toep, jnp.full((h, 1, GRID_W, GRID_W), NEG_BIG, F32)], axis=1)
    cases = _na_case_tables(rows)
    t = toep[:, cases]
    t = jnp.transpose(t, (1, 0, 2, 4, 3, 5))
    return t.reshape(3, h, NA_QROWS * GRID_W, NA_KROWS * GRID_W)


def _na_attn(q, k, v, kc, vc, bias, rest):
    nb, s, bw = q.shape
    hp = HEAD_PAIR_W
    tq = NA_QROWS * GRID_W
    ng = s // tq
    n_sub = NA_GROUPS_PER_STEP
    nsteps = ng // n_sub
    l = kc.shape[1]

    def bias_spec(u):
        def index(b, h, i):
            g = i * n_sub + u
            return (jnp.where(g == 0, 0, jnp.where(g == ng - 1, 2, 1)), h, 0, 0)
        return pl.BlockSpec((1, 2, tq, NA_KROWS * GRID_W), index)

    return pl.pallas_call(
        functools.partial(_na_kernel, rows=s // GRID_W, n_sub=n_sub),
        grid=(nb, bw // hp, nsteps),
        in_specs=[pl.BlockSpec((1, n_sub * tq, hp), lambda b, h, i: (b, i, h)),
                  pl.BlockSpec((1, s, hp), lambda b, h, i: (b, 0, h)),
                  pl.BlockSpec((1, s, hp), lambda b, h, i: (b, 0, h)),
                  pl.BlockSpec((1, l, hp), lambda b, h, i: (b, 0, h)),
                  pl.BlockSpec((1, l, hp), lambda b, h, i: (b, 0, h))]
                 + [bias_spec(u) for u in range(n_sub)]
                 + [pl.BlockSpec((n_sub * tq, hp), lambda b, h, i: (b * nsteps + i, h))],
        out_specs=pl.BlockSpec((n_sub * tq, hp), lambda b, h, i: (b * nsteps + i, h)),
        out_shape=jax.ShapeDtypeStruct((nb * s, bw), F32),
        compiler_params=_params(3), name="nbr_attn",
    )(q, k, v, kc, vc, *([bias] * n_sub), rest)


def _dense_attn_kernel(q_ref, k_ref, v_ref, gate_ref, o_ref):
    o_ref[...] = _softmax_pv(q_ref[0], [(k_ref[0], v_ref[0], None)], gate_ref[...])


def _dense_attn(q, k, v, rest):
    nb, l, bw = q.shape
    hp = HEAD_PAIR_W
    tok = pl.BlockSpec((1, l, hp), lambda b, h: (b, 0, h))
    return pl.pallas_call(
        _dense_attn_kernel,
        grid=(nb, bw // hp),
        in_specs=[tok, tok, tok, pl.BlockSpec((l, hp), lambda b, h: (b, h))],
        out_specs=pl.BlockSpec((l, hp), lambda b, h: (b, h)),
        out_shape=jax.ShapeDtypeStruct((nb * l, bw), F32),
        compiler_params=_params(2), name="ctx_dense_attn",
    )(q, k, v, rest)


def _dft_kernel(c_ref, s_ref, p_ref, q_ref, o_ref):
    o_ref[0] = (jnp.dot(c_ref[...], p_ref[0], preferred_element_type=F32)
                + jnp.dot(s_ref[...], q_ref[0], preferred_element_type=F32))


def _dft_tables(n):
    idx = np.arange(n)
    ang = ((idx[:, None] * idx[None, :]) % n) * (2.0 * np.pi / n)
    scale = n ** -0.5
    return jnp.asarray(np.cos(ang) * scale, BF16), jnp.asarray(np.sin(ang) * scale, BF16)


def _dft_rows_kernel(m1_ref, m2_ref, vr_ref, vi_ref, a_ref, *, nt):
    r = vr_ref.shape[1]
    vrt = jnp.swapaxes(vr_ref[0].astype(F32), 0, 1)
    vit = jnp.swapaxes(vi_ref[0].astype(F32), 0, 1)
    outs = []
    for j in range(nt):
        outs.append(jnp.dot(m1_ref[...], vrt[j].astype(BF16), preferred_element_type=F32)
                    + jnp.dot(m2_ref[...], vit[j].astype(BF16), preferred_element_type=F32))
    at = jnp.swapaxes(jnp.stack(outs, axis=0), 0, 1)
    a_ref[0] = at.reshape(2, r, nt, LANES)


def _dft_cols_kernel(g_ref, a_ref, o_ref, *, k1t):
    outs = []
    for j in range(k1t):
        a = jnp.concatenate([a_ref[0, 0, j], a_ref[0, 1, j]], axis=0).astype(BF16)
        outs.append(jnp.dot(g_ref[j], a, preferred_element_type=F32))
    o_ref[0] = jnp.swapaxes(jnp.stack(outs, axis=0), 0, 1)


def _dft_factored_tables(n):
    r = n // GRID_W
    i = np.arange(r)
    ang = ((i[:, None] * i[None, :]) % r) * (2.0 * np.pi / r)
    c, s = np.cos(ang) * r ** -0.5, np.sin(ang) * r ** -0.5
    m1 = np.concatenate([c, -s], axis=0)
    m2 = np.concatenate([s, c], axis=0)
    k = i[:, None, None] + r * np.arange(GRID_W)[None, :, None]
    th = ((k * np.arange(GRID_W)[None, None, :]) % n) * (2.0 * np.pi / n)
    g = np.concatenate([np.cos(th), np.sin(th)], axis=2) * GRID_W ** -0.5
    return jnp.asarray(m1, BF16), jnp.asarray(m2, BF16), jnp.asarray(g, BF16)


def _dft_positions_factored(tabs, p, q):
    m1, m2, g = tabs
    nb, n, bw = p.shape
    r = n // GRID_W
    nt = 32
    a = pl.pallas_call(
        functools.partial(_dft_rows_kernel, nt=nt),
        grid=(nb, GRID_W // nt, bw // LANES),
        in_specs=[pl.BlockSpec(m1.shape, lambda b, c, l: (0, 0)),
                  pl.BlockSpec(m2.shape, lambda b, c, l: (0, 0)),
                  pl.BlockSpec((1, r, nt, LANES), lambda b, c, l: (b, 0, c, l)),
                  pl.BlockSpec((1, r, nt, LANES), lambda b, c, l: (b, 0, c, l))],
        out_specs=pl.BlockSpec((1, 2, r, nt, LANES), lambda b, c, l: (b, 0, 0, c, l)),
        out_shape=jax.ShapeDtypeStruct((nb, 2, r, GRID_W, bw), F32),
        compiler_params=_params(3), name="dft_rows",
    )(m1, m2, p.reshape(nb, r, GRID_W, bw), q.reshape(nb, r, GRID_W, bw))
    k1t = min(32, r)
    y = pl.pallas_call(
        functools.partial(_dft_cols_kernel, k1t=k1t),
        grid=(nb, r // k1t, bw // LANES),
        in_specs=[pl.BlockSpec((k1t, GRID_W, 2 * GRID_W), lambda b, i, l: (i, 0, 0)),
                  pl.BlockSpec((1, 2, k1t, GRID_W, LANES), lambda b, i, l: (b, 0, i, 0, l))],
        out_specs=pl.BlockSpec((1, GRID_W, k1t, LANES), lambda b, i, l: (b, 0, i, l)),
        out_shape=jax.ShapeDtypeStruct((nb, GRID_W, r, bw), F32),
        compiler_params=_params(3), name="dft_cols",
    )(g, a)
    return y.reshape(nb * n, bw)


def _dft_positions(cn, sn, p, q, tmo):
    nb, n, bw = p.shape
    return pl.pallas_call(
        _dft_kernel,
        grid=(n // tmo, nb),
        in_specs=[pl.BlockSpec((tmo, n), lambda i, b: (i, 0)),
                  pl.BlockSpec((tmo, n), lambda i, b: (i, 0)),
                  pl.BlockSpec((1, n, bw), lambda i, b: (b, 0, 0)),
                  pl.BlockSpec((1, n, bw), lambda i, b: (b, 0, 0))],
        out_specs=pl.BlockSpec((1, tmo, bw), lambda i, b: (b, i, 0)),
        out_shape=jax.ShapeDtypeStruct((nb, n, bw), F32),
        compiler_params=_params(2), name="dft_positions",
    )(cn, sn, p, q)


def _channel_dft_matrix():
    j = np.arange(D_GROUP_W)
    ang = 2.0 * np.pi * ((j[:, None] * j[None, :]) % D_GROUP_W) / D_GROUP_W
    eye = np.eye(D_GROUPS)
    cc = np.kron(eye, np.cos(ang) * D_GROUP_W ** -0.5)
    sc = np.kron(eye, np.sin(ang) * D_GROUP_W ** -0.5)
    return jnp.asarray(np.concatenate([cc, -sc], axis=1), BF16)


def _post_kernel(*refs, even, tm, seq):
    if even:
        (x_ref, a_ref, bb_ref, cb_ref, ub_ref, gb_ref, cbp_ref, ubp_ref, cbn_ref, ubn_ref,
         cw_ref, mod_ref, g_ref, w_ref, o_ref) = refs
    else:
        (x_ref, a_ref, y_ref, gd_ref, mod_ref, g_ref, w_ref, o_ref) = refs
    d = x_ref.shape[-1]
    bw = a_ref.shape[-1]
    if even:
        i = pl.program_id(0)
        t0 = (i * tm) % seq
        z = cb_ref[...].astype(F32) * ub_ref[...].astype(F32)
        zp = cbp_ref[7:8, :].astype(F32) * ubp_ref[7:8, :].astype(F32) * jnp.where(t0 == 0, 0.0, 1.0)
        zn = cbn_ref[0:1, :].astype(F32) * ubn_ref[0:1, :].astype(F32) * jnp.where(t0 + tm == seq, 0.0, 1.0)
        rid = lax.broadcasted_iota(jnp.int32, z.shape, 0)
        z_prev = jnp.where(rid == 0, zp, pltpu.roll(z, 1, 0))
        z_next = jnp.where(rid == tm - 1, zn, pltpu.roll(z, tm - 1, 0))
        cw = cw_ref[...]
        y = cw[0:1] * z_prev + cw[1:2] * z + cw[2:3] * z_next
        second = bb_ref[...].astype(F32) * y * _silu(gb_ref[...].astype(F32))
    else:
        second = y_ref[...] * _silu(gd_ref[...].astype(F32))
    yo = (jnp.dot(a_ref[...].astype(BF16), w_ref[0:bw, :], preferred_element_type=F32)
          + jnp.dot(second.astype(BF16), w_ref[bw:2 * bw, :], preferred_element_type=F32))
    nrm = yo * lax.rsqrt(jnp.mean(yo * yo, axis=-1, keepdims=True) + EPS) * g_ref[...]
    o_ref[...] = x_ref[...] + mod_ref[0][:, 2 * d:3 * d] * nrm


def _post(x2, a, second, rest, conv_w, mod3, mod_row_fn, g_post, w_out_bf, seq, tm, even):
    n, d = x2.shape
    bw = a.shape[1]
    tps = seq // tm
    const = lambda i: (0, 0)
    row = lambda i: (i, 0)
    in_specs = [pl.BlockSpec((tm, d), row), pl.BlockSpec((tm, bw), row)]
    args = [x2, a]
    if even:
        nblk8 = n // 8
        for c in (1, 2, 3, 4):
            in_specs.append(pl.BlockSpec((tm, bw), lambda i, c=c: (i, c)))
            args.append(rest)
        prev = lambda i, c: (jnp.maximum(i * (tm // 8) - 1, 0), c)
        nxt = lambda i, c: (jnp.minimum((i + 1) * (tm // 8), nblk8 - 1), c)
        for fn in (prev, nxt):
            for c in (2, 3):
                in_specs.append(pl.BlockSpec((8, bw), functools.partial(fn, c=c)))
                args.append(rest)
        in_specs.append(pl.BlockSpec(conv_w.shape, const))
        args.append(conv_w)
    else:
        in_specs += [pl.BlockSpec((tm, bw), row), pl.BlockSpec((tm, bw), lambda i: (i, 1))]
        args += [second, rest]
    in_specs += [pl.BlockSpec((1, 1, mod3.shape[2]), lambda i: (mod_row_fn(i // tps), 0, 0)),
                 pl.BlockSpec((1, d), const),
                 pl.BlockSpec(w_out_bf.shape, const)]
    args += [mod3, g_post.reshape(1, d), w_out_bf]
    return pl.pallas_call(
        functools.partial(_post_kernel, even=even, tm=tm, seq=seq),
        grid=(n // tm,), in_specs=in_specs,
        out_specs=pl.BlockSpec((tm, d), row),
        out_shape=jax.ShapeDtypeStruct((n, d), F32),
        compiler_params=_params(1), name="post_even" if even else "post_odd",
    )(*args)


def _rope_tables(seq):
    t = np.arange(seq)
    row = (t // GRID_W).astype(np.float32)
    col = (t % GRID_W).astype(np.float32)
    nf = A_QK // 4
    inv_freq = (ROPE_BASE ** (-np.arange(nf, dtype=np.float32) / nf)).astype(np.float32)
    p = np.arange(A_QK)
    f_idx = p % nf
    use_col = (p % A_QK) >= A_QK // 2
    sign = np.where((p % (2 * nf)) < nf, -1.0, 1.0)
    pos = np.where(use_col[None, :], col[:, None], row[:, None])
    ang = (pos * inv_freq[f_idx][None, :]).astype(np.float32).astype(np.float64)
    cos = np.cos(ang).astype(np.float32)
    sin = (np.sin(ang) * sign[None, :]).astype(np.float32)
    reps = LANES // A_QK
    return (jnp.asarray(np.tile(cos, (1, reps))), jnp.asarray(np.tile(sin, (1, reps))),
            jnp.asarray(np.ascontiguousarray(cos.T)), jnp.asarray(np.ascontiguousarray(sin.T)))


def _lambda_init(layer):
    return 0.8 - 0.6 * math.exp(-0.3 * layer)


def kernel(x, c, ctx, c_ctx, w_mod, b_mod, norm_pre, norm_post, w_in_even, lam_a, subln_a, conv_b,
           w_in_odd, rpb_c, w_out):
    nb, seq, d = x.shape
    lctx = ctx.shape[1]
    depth = w_mod.shape[0]
    assert seq % (NA_QROWS * GRID_W * NA_GROUPS_PER_STEP) == 0 and seq // GRID_W >= NA_KROWS
    assert lctx % LANES == 0 and seq % KEY_CHUNK == 0 and nb < 16

    pad = jnp.zeros((16 - nb - 1, d), F32)
    cc = jnp.concatenate([c, c_ctx[None, :], pad], axis=0)
    mods = _mod_all(cc, w_mod, b_mod)

    rope_tabs = _rope_tables(seq)
    cs_bf = _channel_dft_matrix()
    dft_tabs = _dft_factored_tables(seq)
    cn_c, sn_c = _dft_tables(lctx)

    tm_x = 512
    tm_c = lctx
    x_row = lambda b: b
    c_row = lambda b: nb

    x2 = x.reshape(nb * seq, d)
    c2 = ctx.reshape(nb * lctx, d)
    for l in range(depth):
        need_ctx = l < depth - 1
        j = l // 2
        mod3 = mods[l].reshape(16, 1, 3 * d)
        w_out_bf = w_out[l].astype(BF16)
        if l % 2 == 0:
            w_bf = w_in_even[j].astype(BF16)
            li = _lambda_init(l)
            qt, k, vt, rest = _inproj_even(x2, mod3, x_row, norm_pre[l], w_bf, seq, rope_tabs, tm_x)
            qct, kc, vct, rest_c = _inproj_even(c2, mod3, c_row, norm_pre[l], w_bf, lctx, None, tm_c)
            a = _diff_attn(lam_a[j], subln_a[j], qt, kc, vct, k, vt, rest, li, 1024)
            x2n = _post(x2, a, None, rest, conv_b[j], mod3, x_row, norm_post[l], w_out_bf, seq, tm_x, True)
            if need_ctx:
                ac = _diff_attn(lam_a[j], subln_a[j], qct, kc, vct, None, None, rest_c, li, lctx)
                c2 = _post(c2, ac, None, rest_c, conv_b[j], mod3, c_row, norm_post[l], w_out_bf,
                           lctx, tm_c, True)
            x2 = x2n
        else:
            w_bf = w_in_odd[j].astype(BF16)
            q, k, v, rest, p, pq = _inproj_odd(x2, mod3, x_row, norm_pre[l], w_bf, cs_bf, seq, tm_x)
            qc, kc, vc, rest_c, p_c, pq_c = _inproj_odd(c2, mod3, c_row, norm_pre[l], w_bf, cs_bf, lctx, tm_c)
            bias = _na_bias(rpb_c[j], seq // GRID_W)
            a = _na_attn(q, k, v, kc, vc, bias, rest)
            y = _dft_positions_factored(dft_tabs, p, pq)
            x2n = _post(x2, a, y, rest, None, mod3, x_row, norm_post[l], w_out_bf, seq, tm_x, False)
            if need_ctx:
                ac = _dense_attn(qc, kc, vc, rest_c)
                yc = _dft_positions(cn_c, sn_c, p_c, pq_c, lctx).reshape(nb * lctx, -1)
                c2 = _post(c2, ac, yc, rest_c, None, mod3, c_row, norm_post[l], w_out_bf, lctx, tm_c, False)
            x2 = x2n
    return x2.reshape(nb, seq, d)
```

```python
import functools
import math

import numpy as np
import jax
import jax.numpy as jnp
from jax import lax
from jax.experimental import pallas as pl
from jax.experimental.pallas import tpu as pltpu

GRID_W = 64
A_HEADS = 8
A_QK = 32
A_V = 64
C_HEADS = 8
C_HEAD_DIM = 64
NA_KH = 8
NA_KW = 16
D_GROUPS = 8
D_GROUP_W = 64
CONV_W = 3
ROPE_BASE = 10000.0
EPS = 1e-6
LOG2E = 1.4426950408889634
NEG_BIG = -1e30

LANES = 128
HEAD_PAIR_W = 128
KEY_CHUNK = 256
DIFF_SUB_Q = 512
SCORE_AHEAD = 3
VMEM_LIMIT = 48 * 1024 * 1024

F32 = jnp.float32
BF16 = jnp.bfloat16


def _silu(v):
    return v * jax.nn.sigmoid(v)


def _params(n_axes):
    return pltpu.CompilerParams(dimension_semantics=("arbitrary",) * n_axes,
                                vmem_limit_bytes=VMEM_LIMIT)


def _nt_dot(a, b):
    return lax.dot_general(a, b, (((1,), (1,)), ((), ())), preferred_element_type=F32)


def _mod_kernel(cc_ref, w_ref, b_ref, o_ref):
    s = _silu(cc_ref[...]).astype(BF16)
    o_ref[0] = jnp.dot(s, w_ref[0].astype(BF16), preferred_element_type=F32) + b_ref[0]


def _mod_all(cc, w_mod, b_mod):
    depth, d, d3 = w_mod.shape
    tn = d
    return pl.pallas_call(
        _mod_kernel,
        grid=(depth, d3 // tn),
        in_specs=[pl.BlockSpec((cc.shape[0], d), lambda l, n: (0, 0)),
                  pl.BlockSpec((1, d, tn), lambda l, n: (l, 0, n)),
                  pl.BlockSpec((1, 1, tn), lambda l, n: (l, 0, n))],
        out_specs=pl.BlockSpec((1, cc.shape[0], tn), lambda l, n: (l, 0, n)),
        out_shape=jax.ShapeDtypeStruct((depth, cc.shape[0], d3), F32),
        compiler_params=_params(2), name="adaln_mod",
    )(cc, w_mod, b_mod.reshape(depth, 1, d3))


def _prenorm(x_ref, mod_ref, g_ref):
    d = x_ref.shape[-1]
    x = x_ref[...]
    y = x * lax.rsqrt(jnp.mean(x * x, axis=-1, keepdims=True) + EPS) * g_ref[...]
    m = mod_ref[0]
    h = y * (1.0 + m[:, d:2 * d]) + m[:, 0:d]
    return h.astype(BF16)


def _inproj_even_kernel(*refs, rope, tm):
    if rope:
        (x_ref, mod_ref, g_ref, w_ref, wqt_ref, wvt_ref,
         cos_ref, sin_ref, cost_ref, sint_ref, qt_ref, k_ref, vt_ref, rest_ref) = refs
    else:
        (x_ref, mod_ref, g_ref, w_ref, wqt_ref, wvt_ref, qt_ref, k_ref, vt_ref, rest_ref) = refs
    bw = qt_ref.shape[1]
    hb = _prenorm(x_ref, mod_ref, g_ref)
    pair = A_QK // 4

    def rotary(u, axis, cos, sin):
        idx = lax.broadcasted_iota(jnp.int32, u.shape, axis)
        partner = jnp.where((idx & pair) == 0, pltpu.roll(u, bw - pair, axis), pltpu.roll(u, pair, axis))
        return u * cos + partner * sin

    qt = _nt_dot(wqt_ref[...], hb)
    if rope:
        reps = bw // cost_ref.shape[0]
        qt = rotary(qt, 0, jnp.tile(cost_ref[...], (reps, 1)), jnp.tile(sint_ref[...], (reps, 1)))
    qt_ref[0] = (qt * (A_QK ** -0.5 * LOG2E)).astype(BF16)

    k = jnp.dot(hb, w_ref[:, bw:2 * bw], preferred_element_type=F32)
    if rope:
        reps = bw // cos_ref.shape[1]
        k = rotary(k, 1, jnp.tile(cos_ref[...], (1, reps)), jnp.tile(sin_ref[...], (1, reps)))
    k_ref[0] = k.astype(BF16)

    vt_ref[0] = _nt_dot(wvt_ref[...], hb).astype(BF16)

    n_rest = rest_ref.shape[1] // bw
    for c in range(n_rest):
        col = (3 + c) * bw
        rest_ref[:, c * bw:(c + 1) * bw] = jnp.dot(
            hb, w_ref[:, col:col + bw], preferred_element_type=F32).astype(BF16)


def _inproj_even(x2, mod3, mod_row_fn, g_pre, w_bf, seq, rope_tabs, tm):
    n, d = x2.shape
    bw = w_bf.shape[1] // 8
    nb = n // seq
    tps = seq // tm
    rope = rope_tabs is not None
    wqt = w_bf[:, 0:bw].T
    wvt = w_bf[:, 2 * bw:3 * bw].T
    const = lambda i: (0, 0)
    in_specs = [pl.BlockSpec((tm, d), lambda i: (i, 0)),
                pl.BlockSpec((1, 1, mod3.shape[2]), lambda i: (mod_row_fn(i // tps), 0, 0)),
                pl.BlockSpec((1, d), const),
                pl.BlockSpec(w_bf.shape, const),
                pl.BlockSpec(wqt.shape, const),
                pl.BlockSpec(wvt.shape, const)]
    args = [x2, mod3, g_pre.reshape(1, d), w_bf, wqt, wvt]
    if rope:
        cos_n, sin_n, cos_t, sin_t = rope_tabs
        in_specs += [pl.BlockSpec((tm, cos_n.shape[1]), lambda i: (i % tps, 0)),
                     pl.BlockSpec((tm, cos_n.shape[1]), lambda i: (i % tps, 0)),
                     pl.BlockSpec((cos_t.shape[0], tm), lambda i: (0, i % tps)),
                     pl.BlockSpec((cos_t.shape[0], tm), lambda i: (0, i % tps))]
        args += [cos_n, sin_n, cos_t, sin_t]
    out_specs = [pl.BlockSpec((1, bw, tm), lambda i: (i // tps, 0, i % tps)),
                 pl.BlockSpec((1, tm, bw), lambda i: (i // tps, i % tps, 0)),
                 pl.BlockSpec((1, bw, tm), lambda i: (i // tps, 0, i % tps)),
                 pl.BlockSpec((tm, 5 * bw), lambda i: (i, 0))]
    out_shape = [jax.ShapeDtypeStruct((nb, bw, seq), BF16),
                 jax.ShapeDtypeStruct((nb, seq, bw), BF16),
                 jax.ShapeDtypeStruct((nb, bw, seq), BF16),
                 jax.ShapeDtypeStruct((n, 5 * bw), BF16)]
    return pl.pallas_call(
        functools.partial(_inproj_even_kernel, rope=rope, tm=tm),
        grid=(n // tm,), in_specs=in_specs, out_specs=out_specs, out_shape=out_shape,
        compiler_params=_params(1), name="inproj_even",
    )(*args)


def _inproj_odd_kernel(x_ref, mod_ref, g_ref, w_ref, cs_ref, q_ref, k_ref, v_ref, rest_ref, p_ref, pq_ref):
    bw = q_ref.shape[2]
    hb = _prenorm(x_ref, mod_ref, g_ref)

    def proj(c):
        return jnp.dot(hb, w_ref[:, c * bw:(c + 1) * bw], preferred_element_type=F32)

    q_ref[0] = (proj(0) * (C_HEAD_DIM ** -0.5 * LOG2E)).astype(BF16)
    k_ref[0] = proj(1).astype(BF16)
    v_ref[0] = proj(2).astype(BF16)
    rest_ref[:, 0:bw] = proj(3).astype(BF16)
    rest_ref[:, bw:2 * bw] = proj(5).astype(BF16)
    pq = jnp.dot(proj(4).astype(BF16), cs_ref[...], preferred_element_type=F32)
    p_ref[0] = pq[:, 0:bw].astype(BF16)
    pq_ref[0] = pq[:, bw:2 * bw].astype(BF16)


def _inproj_odd(x2, mod3, mod_row_fn, g_pre, w_bf, cs_bf, seq, tm):
    n, d = x2.shape
    bw = w_bf.shape[1] // 6
    nb = n // seq
    tps = seq // tm
    const = lambda i: (0, 0)
    tok = pl.BlockSpec((1, tm, bw), lambda i: (i // tps, i % tps, 0))
    tok_shape = jax.ShapeDtypeStruct((nb, seq, bw), BF16)
    return pl.pallas_call(
        _inproj_odd_kernel,
        grid=(n // tm,),
        in_specs=[pl.BlockSpec((tm, d), lambda i: (i, 0)),
                  pl.BlockSpec((1, 1, mod3.shape[2]), lambda i: (mod_row_fn(i // tps), 0, 0)),
                  pl.BlockSpec((1, d), const),
                  pl.BlockSpec(w_bf.shape, const),
                  pl.BlockSpec(cs_bf.shape, const)],
        out_specs=[tok, tok, tok, pl.BlockSpec((tm, 2 * bw), lambda i: (i, 0)), tok, tok],
        out_shape=[tok_shape, tok_shape, tok_shape, jax.ShapeDtypeStruct((n, 2 * bw), BF16),
                   tok_shape, tok_shape],
        compiler_params=_params(1), name="inproj_odd",
    )(x2, mod3, g_pre.reshape(1, d), w_bf, cs_bf)


def _diff_attn_kernel(*refs, has_main, tq, lam_init):
    if has_main:
        (lam_ref, sub_ref, qt_ref, kc_ref, vct_ref, ka_ref, vat_ref, ga_ref, o_ref) = refs
        key_refs = [(kc_ref, vct_ref), (ka_ref, vat_ref)]
    else:
        (lam_ref, sub_ref, qt_ref, kc_ref, vct_ref, ga_ref, o_ref) = refs
        key_refs = [(kc_ref, vct_ref)]
    chunks = []
    for k_ref, vt_ref in key_refs:
        for st in range(0, k_ref.shape[1], KEY_CHUNK):
            chunks.append((k_ref, vt_ref, st, min(KEY_CHUNK, k_ref.shape[1] - st)))
    lp = lam_ref[...]
    lam = (jnp.exp(jnp.sum(lp[0:1] * lp[1:2], axis=1, keepdims=True))
           - jnp.exp(jnp.sum(lp[2:3] * lp[3:4], axis=1, keepdims=True)) + lam_init)
    sq = min(tq, DIFF_SUB_Q)
    qf = qt_ref[0].astype(F32)
    rowid = lax.broadcasted_iota(jnp.int32, (qf.shape[0], sq), 0)
    sub = jnp.tile(sub_ref[...], (1, sq // LANES))

    def fold(v):
        return v.reshape(v.shape[0] // 8, 8, sq)

    items = [(j, hh, m) for j in range(tq // sq) for hh in range(2) for m in range(2)]
    work = [(t, ci) for t in range(len(items)) for ci in range(len(chunks))]
    qms, state, maps, pending = {}, {}, {}, []

    def issue(t, ci):
        j, hh, m = items[t]
        if ci == 0:
            lo = hh * A_V + m * A_QK
            qms[t] = jnp.where((rowid >= lo) & (rowid < lo + A_QK), qf[:, j * sq:(j + 1) * sq],
                               0.0).astype(BF16)
        k_ref, _, st, sz = chunks[ci]
        sc = jnp.dot(k_ref[0, st:st + sz, :], qms[t], preferred_element_type=F32)
        pending.append((t, ci, sc, jnp.max(jnp.max(fold(sc), axis=0), axis=0, keepdims=True)))

    def consume():
        t, ci, s, cm = pending.pop(0)
        hh = items[t][1]
        _, vt_ref, st, sz = chunks[ci]
        if ci == 0:
            state[t] = (jnp.full((1, sq), -jnp.inf, F32), jnp.zeros((8, sq), F32),
                        jnp.zeros((A_V, sq), F32))
        m_run, l, acc = state[t]
        m_new = jnp.maximum(m_run, cm)
        alpha = jnp.exp2(m_run - m_new)
        e = jnp.exp2(s - m_new)
        l = alpha * l + jnp.sum(fold(e), axis=0)
        acc = alpha * acc + jnp.dot(vt_ref[0, hh * A_V:(hh + 1) * A_V, st:st + sz], e.astype(BF16),
                                    preferred_element_type=F32)
        state[t] = (m_new, l, acc)
        if ci == len(chunks) - 1:
            del state[t], qms[t]
            maps[t] = acc * (1.0 / jnp.sum(l, axis=0, keepdims=True))

    for idx in range(len(work) + SCORE_AHEAD):
        if idx < len(work):
            issue(*work[idx])
        if idx >= SCORE_AHEAD:
            consume()
    for j in range(tq // sq):
        heads = []
        for hh in range(2):
            ot = maps[4 * j + 2 * hh] - lam * maps[4 * j + 2 * hh + 1]
            ot = ot * lax.rsqrt(jnp.mean(ot * ot, axis=0, keepdims=True) + EPS) * sub * (1.0 - lam_init)
            heads.append(ot)
        o = jnp.concatenate(heads, axis=0).T
        rows = slice(j * sq, (j + 1) * sq)
        o_ref[rows, :] = o * _silu(ga_ref[rows, :].astype(F32))


def _diff_attn(lam_p, subln, qt, kc, vct, ka, vat, rest, lam_init, tq):
    nb, bw, sq = qt.shape
    nq = sq // tq
    hp = HEAD_PAIR_W
    sub_b = jnp.broadcast_to(subln.reshape(A_V, 1), (A_V, LANES))
    in_specs = [pl.BlockSpec(lam_p.shape, lambda b, h, i: (0, 0)),
                pl.BlockSpec(sub_b.shape, lambda b, h, i: (0, 0)),
                pl.BlockSpec((1, hp, tq), lambda b, h, i: (b, h, i)),
                pl.BlockSpec((1, kc.shape[1], hp), lambda b, h, i: (b, 0, h)),
                pl.BlockSpec((1, hp, kc.shape[1]), lambda b, h, i: (b, h, 0))]
    args = [lam_p, sub_b, qt, kc, vct]
    if ka is not None:
        in_specs += [pl.BlockSpec((1, ka.shape[1], hp), lambda b, h, i: (b, 0, h)),
                     pl.BlockSpec((1, hp, ka.shape[1]), lambda b, h, i: (b, h, 0))]
        args += [ka, vat]
    in_specs.append(pl.BlockSpec((tq, hp), lambda b, h, i: (b * nq + i, h)))
    args.append(rest)
    return pl.pallas_call(
        functools.partial(_diff_attn_kernel, has_main=ka is not None, tq=tq, lam_init=lam_init),
        grid=(nb, bw // hp, nq),
        in_specs=in_specs,
        out_specs=pl.BlockSpec((tq, hp), lambda b, h, i: (b * nq + i, h)),
        out_shape=jax.ShapeDtypeStruct((nb * sq, bw), F32),
        compiler_params=_params(3), name="diff_attn",
    )(*args)


NA_QROWS = 4
NA_KROWS = 12
NA_GROUPS_PER_STEP = 4


def _softmax_pv(problems):
    lane = lax.broadcasted_iota(jnp.int32, problems[0][0].shape, 1)
    chains = [(p, hh) for p in range(len(problems)) for hh in range(2)]
    scores, maxes, exps, outs = {}, {}, {}, {}
    for p, hh in chains:
        q, pieces, _ = problems[p]
        qm = jnp.where((lane >= hh * C_HEAD_DIM) & (lane < (hh + 1) * C_HEAD_DIM), q.astype(F32),
                       0.0).astype(BF16)
        scores[p, hh] = [_nt_dot(qm, k) if bias is None else _nt_dot(qm, k) + bias[hh]
                         for k, _, bias in pieces]
    for c in chains:
        maxes[c] = functools.reduce(jnp.maximum, [jnp.max(s, axis=1, keepdims=True) for s in scores[c]])
    for c in chains:
        exps[c] = [jnp.exp2(s - maxes[c]) for s in scores.pop(c)]
    for p, hh in chains:
        es = exps.pop((p, hh))
        l = functools.reduce(jnp.add, [jnp.sum(e, axis=1, keepdims=True) for e in es])
        pv = functools.reduce(jnp.add, [jnp.dot(e.astype(BF16), v, preferred_element_type=F32)
                                        for e, (_, v, _) in zip(es, problems[p][1])])
        outs[p, hh] = pv * (1.0 / l)
    return [jnp.where(lane < C_HEAD_DIM, outs[p, 0], outs[p, 1]) * _silu(problems[p][2].astype(F32))
            for p in range(len(problems))]


class _BiasOf:
    def __init__(self, ref):
        self.ref = ref

    def __getitem__(self, hh):
        return self.ref[0, hh]


def _na_kernel(q_ref, k_ref, v_ref, kc_ref, vc_ref, *rest, rows, n_sub):
    bias_refs, (gate_ref, o_ref) = rest[:n_sub], rest[n_sub:]
    tq = NA_QROWS * GRID_W
    nk = NA_KROWS * GRID_W
    problems = []
    for u in range(n_sub):
        g = pl.program_id(2) * n_sub + u
        base = jnp.clip(NA_QROWS * g - NA_KH // 2, 0, rows - NA_KROWS)
        start = pl.multiple_of(base * GRID_W, NA_QROWS * GRID_W)
        kw = k_ref[0, pl.ds(start, nk), :]
        vw = v_ref[0, pl.ds(start, nk), :]
        qrows = slice(u * tq, (u + 1) * tq)
        problems.append((q_ref[0, qrows, :],
                         [(kw, vw, _BiasOf(bias_refs[u])), (kc_ref[0], vc_ref[0], None)],
                         gate_ref[qrows, :]))
    for u, o in enumerate(_softmax_pv(problems)):
        o_ref[u * tq:(u + 1) * tq, :] = o


def _na_case_tables(rows):
    rs = np.clip(np.arange(rows) - NA_KH // 2, 0, rows - NA_KH)
    n_groups = rows // NA_QROWS
    pats = []
    for g in range(n_groups):
        base = int(np.clip(NA_QROWS * g - NA_KH // 2, 0, rows - NA_KROWS))
        pat = np.full((NA_QROWS, NA_KROWS), 2 * NA_KH - 1, np.int32)
        for j in range(NA_QROWS):
            r = NA_QROWS * g + j
            for i in range(NA_KROWS):
                kr = base + i
                if rs[r] <= kr < rs[r] + NA_KH:
                    pat[j, i] = kr - r + NA_KH - 1
            assert (pat[j] != 2 * NA_KH - 1).sum() == NA_KH
        pats.append(pat)
    cases = np.stack([pats[0], pats[1], pats[-1]])
    for g in range(n_groups):
        want = 0 if g == 0 else (2 if g == n_groups - 1 else 1)
        assert (pats[g] == cases[want]).all()
    return cases


def _na_bias(rpb, rows):
    h = rpb.shape[0]
    cq = np.arange(GRID_W)
    cs = np.clip(cq - NA_KW // 2, 0, GRID_W - NA_KW)
    kcol = np.arange(GRID_W)
    col_ok = (kcol[None, :] >= cs[:, None]) & (kcol[None, :] < cs[:, None] + NA_KW)
    col_off = np.clip(kcol[None, :] - cq[:, None] + NA_KW - 1, 0, 2 * NA_KW - 2)
    toep = jnp.where(col_ok[None, None], rpb[:, :, col_off] * LOG2E, NEG_BIG)
    toep = jnp.concatenate([toep, jnp.full((h, 1, GRID_W, GRID_W), NEG_BIG, F32)], axis=1)
    cases = _na_case_tables(rows)
    n_case, qr, kr = cases.shape
    return pl.pallas_call(
        functools.partial(_na_bias_kernel, cases=cases),
        grid=(h,),
        in_specs=[pl.BlockSpec((1,) + toep.shape[1:], lambda i: (i, 0, 0, 0))],
        out_specs=pl.BlockSpec((n_case, 1, qr * GRID_W, kr * GRID_W), lambda i: (0, i, 0, 0)),
        out_shape=jax.ShapeDtypeStruct((n_case, h, qr * GRID_W, kr * GRID_W), F32),
        compiler_params=_params(1), name="nbr_bias",
    )(toep)


def _na_bias_kernel(toep_ref, o_ref, *, cases):
    for c in range(cases.shape[0]):
        for j in range(cases.shape[1]):
            o_ref[c, 0, j * GRID_W:(j + 1) * GRID_W, :] = jnp.concatenate(
                [toep_ref[0, int(b)] for b in cases[c, j]], axis=1)


def _na_attn(q, k, v, kc, vc, bias, rest):
    nb, s, bw = q.shape
    hp = HEAD_PAIR_W
    tq = NA_QROWS * GRID_W
    ng = s // tq
    n_sub = NA_GROUPS_PER_STEP
    nsteps = ng // n_sub
    l = kc.shape[1]

    def bias_spec(u):
        def index(b, h, i):
            g = i * n_sub + u
            return (jnp.where(g == 0, 0, jnp.where(g == ng - 1, 2, 1)), h, 0, 0)
        return pl.BlockSpec((1, 2, tq, NA_KROWS * GRID_W), index)

    return pl.pallas_call(
        functools.partial(_na_kernel, rows=s // GRID_W, n_sub=n_sub),
        grid=(nb, bw // hp, nsteps),
        in_specs=[pl.BlockSpec((1, n_sub * tq, hp), lambda b, h, i: (b, i, h)),
                  pl.BlockSpec((1, s, hp), lambda b, h, i: (b, 0, h)),
                  pl.BlockSpec((1, s, hp), lambda b, h, i: (b, 0, h)),
                  pl.BlockSpec((1, l, hp), lambda b, h, i: (b, 0, h)),
                  pl.BlockSpec((1, l, hp), lambda b, h, i: (b, 0, h))]
                 + [bias_spec(u) for u in range(n_sub)]
                 + [pl.BlockSpec((n_sub * tq, hp), lambda b, h, i: (b * nsteps + i, h))],
        out_specs=pl.BlockSpec((n_sub * tq, hp), lambda b, h, i: (b * nsteps + i, h)),
        out_shape=jax.ShapeDtypeStruct((nb * s, bw), F32),
        compiler_params=_params(3), name="nbr_attn",
    )(q, k, v, kc, vc, *([bias] * n_sub), rest)


def _dense_attn_kernel(q_ref, k_ref, v_ref, gate_ref, o_ref):
    o_ref[...] = _softmax_pv([(q_ref[0], [(k_ref[0], v_ref[0], None)], gate_ref[...])])[0]


def _dense_attn(q, k, v, rest):
    nb, l, bw = q.shape
    hp = HEAD_PAIR_W
    tok = pl.BlockSpec((1, l, hp), lambda b, h: (b, 0, h))
    return pl.pallas_call(
        _dense_attn_kernel,
        grid=(nb, bw // hp),
        in_specs=[tok, tok, tok, pl.BlockSpec((l, hp), lambda b, h: (b, h))],
        out_specs=pl.BlockSpec((l, hp), lambda b, h: (b, h)),
        out_shape=jax.ShapeDtypeStruct((nb * l, bw), F32),
        compiler_params=_params(2), name="ctx_dense_attn",
    )(q, k, v, rest)


def _dft_kernel(c_ref, s_ref, p_ref, q_ref, o_ref):
    o_ref[0] = (jnp.dot(c_ref[...], p_ref[0], preferred_element_type=F32)
                + jnp.dot(s_ref[...], q_ref[0], preferred_element_type=F32))


def _dft_tables(n):
    idx = np.arange(n)
    ang = ((idx[:, None] * idx[None, :]) % n) * (2.0 * np.pi / n)
    scale = n ** -0.5
    return jnp.asarray(np.cos(ang) * scale, BF16), jnp.asarray(np.sin(ang) * scale, BF16)


def _dft_rows_kernel(m1_ref, m2_ref, vr_ref, vi_ref, a_ref, *, nt):
    r = vr_ref.shape[1]
    vrt = jnp.swapaxes(vr_ref[0].astype(F32), 0, 1)
    vit = jnp.swapaxes(vi_ref[0].astype(F32), 0, 1)
    outs = []
    for j in range(nt):
        outs.append(jnp.dot(m1_ref[...], vrt[j].astype(BF16), preferred_element_type=F32)
                    + jnp.dot(m2_ref[...], vit[j].astype(BF16), preferred_element_type=F32))
    at = jnp.swapaxes(jnp.stack(outs, axis=0), 0, 1)
    a_ref[0] = at.reshape(2, r, nt, LANES)


def _dft_cols_kernel(g_ref, a_ref, o_ref, *, k1t):
    outs = []
    for j in range(k1t):
        a = jnp.concatenate([a_ref[0, 0, j], a_ref[0, 1, j]], axis=0).astype(BF16)
        outs.append(jnp.dot(g_ref[j], a, preferred_element_type=F32))
    o_ref[0] = jnp.swapaxes(jnp.stack(outs, axis=0), 0, 1)


def _dft_factored_tables(n):
    r = n // GRID_W
    i = np.arange(r)
    ang = ((i[:, None] * i[None, :]) % r) * (2.0 * np.pi / r)
    c, s = np.cos(ang) * r ** -0.5, np.sin(ang) * r ** -0.5
    m1 = np.concatenate([c, -s], axis=0)
    m2 = np.concatenate([s, c], axis=0)
    k = i[:, None, None] + r * np.arange(GRID_W)[None, :, None]
    th = ((k * np.arange(GRID_W)[None, None, :]) % n) * (2.0 * np.pi / n)
    g = np.concatenate([np.cos(th), np.sin(th)], axis=2) * GRID_W ** -0.5
    return jnp.asarray(m1, BF16), jnp.asarray(m2, BF16), jnp.asarray(g, BF16)


def _dft_positions_factored(tabs, p, q):
    m1, m2, g = tabs
    nb, n, bw = p.shape
    r = n // GRID_W
    nt = 32
    a = pl.pallas_call(
        functools.partial(_dft_rows_kernel, nt=nt),
        grid=(nb, GRID_W // nt, bw // LANES),
        in_specs=[pl.BlockSpec(m1.shape, lambda b, c, l: (0, 0)),
                  pl.BlockSpec(m2.shape, lambda b, c, l: (0, 0)),
                  pl.BlockSpec((1, r, nt, LANES), lambda b, c, l: (b, 0, c, l)),
                  pl.BlockSpec((1, r, nt, LANES), lambda b, c, l: (b, 0, c, l))],
        out_specs=pl.BlockSpec((1, 2, r, nt, LANES), lambda b, c, l: (b, 0, 0, c, l)),
        out_shape=jax.ShapeDtypeStruct((nb, 2, r, GRID_W, bw), F32),
        compiler_params=_params(3), name="dft_rows",
    )(m1, m2, p.reshape(nb, r, GRID_W, bw), q.reshape(nb, r, GRID_W, bw))
    k1t = min(32, r)
    y = pl.pallas_call(
        functools.partial(_dft_cols_kernel, k1t=k1t),
        grid=(nb, r // k1t, bw // LANES),
        in_specs=[pl.BlockSpec((k1t, GRID_W, 2 * GRID_W), lambda b, i, l: (i, 0, 0)),
                  pl.BlockSpec((1, 2, k1t, GRID_W, LANES), lambda b, i, l: (b, 0, i, 0, l))],
        out_specs=pl.BlockSpec((1, GRID_W, k1t, LANES), lambda b, i, l: (b, 0, i, l)),
        out_shape=jax.ShapeDtypeStruct((nb, GRID_W, r, bw), F32),
        compiler_params=_params(3), name="dft_cols",
    )(g, a)
    return y.reshape(nb * n, bw)


def _dft_positions(cn, sn, p, q, tmo):
    nb, n, bw = p.shape
    return pl.pallas_call(
        _dft_kernel,
        grid=(n // tmo, nb),
        in_specs=[pl.BlockSpec((tmo, n), lambda i, b: (i, 0)),
                  pl.BlockSpec((tmo, n), lambda i, b: (i, 0)),
                  pl.BlockSpec((1, n, bw), lambda i, b: (b, 0, 0)),
                  pl.BlockSpec((1, n, bw), lambda i, b: (b, 0, 0))],
        out_specs=pl.BlockSpec((1, tmo, bw), lambda i, b: (b, i, 0)),
        out_shape=jax.ShapeDtypeStruct((nb, n, bw), F32),
        compiler_params=_params(2), name="dft_positions",
    )(cn, sn, p, q)


def _channel_dft_matrix():
    j = np.arange(D_GROUP_W)
    ang = 2.0 * np.pi * ((j[:, None] * j[None, :]) % D_GROUP_W) / D_GROUP_W
    eye = np.eye(D_GROUPS)
    cc = np.kron(eye, np.cos(ang) * D_GROUP_W ** -0.5)
    sc = np.kron(eye, np.sin(ang) * D_GROUP_W ** -0.5)
    return jnp.asarray(np.concatenate([cc, -sc], axis=1), BF16)


def _post_kernel(*refs, even, tm, seq):
    if even:
        (x_ref, a_ref, bb_ref, cb_ref, ub_ref, gb_ref, cbp_ref, ubp_ref, cbn_ref, ubn_ref,
         cw_ref, mod_ref, g_ref, w_ref, o_ref) = refs
    else:
        (x_ref, a_ref, y_ref, gd_ref, mod_ref, g_ref, w_ref, o_ref) = refs
    d = x_ref.shape[-1]
    bw = a_ref.shape[-1]
    if even:
        i = pl.program_id(0)
        t0 = (i * tm) % seq
        z = cb_ref[...].astype(F32) * ub_ref[...].astype(F32)
        zp = cbp_ref[7:8, :].astype(F32) * ubp_ref[7:8, :].astype(F32) * jnp.where(t0 == 0, 0.0, 1.0)
        zn = cbn_ref[0:1, :].astype(F32) * ubn_ref[0:1, :].astype(F32) * jnp.where(t0 + tm == seq, 0.0, 1.0)
        rid = lax.broadcasted_iota(jnp.int32, z.shape, 0)
        z_prev = jnp.where(rid == 0, zp, pltpu.roll(z, 1, 0))
        z_next = jnp.where(rid == tm - 1, zn, pltpu.roll(z, tm - 1, 0))
        cw = cw_ref[...]
        y = cw[0:1] * z_prev + cw[1:2] * z + cw[2:3] * z_next
        second = bb_ref[...].astype(F32) * y * _silu(gb_ref[...].astype(F32))
    else:
        second = y_ref[...] * _silu(gd_ref[...].astype(F32))
    yo = (jnp.dot(a_ref[...].astype(BF16), w_ref[0:bw, :], preferred_element_type=F32)
          + jnp.dot(second.astype(BF16), w_ref[bw:2 * bw, :], preferred_element_type=F32))
    nrm = yo * lax.rsqrt(jnp.mean(yo * yo, axis=-1, keepdims=True) + EPS) * g_ref[...]
    o_ref[...] = x_ref[...] + mod_ref[0][:, 2 * d:3 * d] * nrm


def _post(x2, a, second, rest, conv_w, mod3, mod_row_fn, g_post, w_out_bf, seq, tm, even):
    n, d = x2.shape
    bw = a.shape[1]
    tps = seq // tm
    const = lambda i: (0, 0)
    row = lambda i: (i, 0)
    in_specs = [pl.BlockSpec((tm, d), row), pl.BlockSpec((tm, bw), row)]
    args = [x2, a]
    if even:
        nblk8 = n // 8
        for c in (1, 2, 3, 4):
            in_specs.append(pl.BlockSpec((tm, bw), lambda i, c=c: (i, c)))
            args.append(rest)
        prev = lambda i, c: (jnp.maximum(i * (tm // 8) - 1, 0), c)
        nxt = lambda i, c: (jnp.minimum((i + 1) * (tm // 8), nblk8 - 1), c)
        for fn in (prev, nxt):
            for c in (2, 3):
                in_specs.append(pl.BlockSpec((8, bw), functools.partial(fn, c=c)))
                args.append(rest)
        in_specs.append(pl.BlockSpec(conv_w.shape, const))
        args.append(conv_w)
    else:
        in_specs += [pl.BlockSpec((tm, bw), row), pl.BlockSpec((tm, bw), lambda i: (i, 1))]
        args += [second, rest]
    in_specs += [pl.BlockSpec((1, 1, mod3.shape[2]), lambda i: (mod_row_fn(i // tps), 0, 0)),
                 pl.BlockSpec((1, d), const),
                 pl.BlockSpec(w_out_bf.shape, const)]
    args += [mod3, g_post.reshape(1, d), w_out_bf]
    return pl.pallas_call(
        functools.partial(_post_kernel, even=even, tm=tm, seq=seq),
        grid=(n // tm,), in_specs=in_specs,
        out_specs=pl.BlockSpec((tm, d), row),
        out_shape=jax.ShapeDtypeStruct((n, d), F32),
        compiler_params=_params(1), name="post_even" if even else "post_odd",
    )(*args)


def _rope_tables(seq):
    t = np.arange(seq)
    row = (t // GRID_W).astype(np.float32)
    col = (t % GRID_W).astype(np.float32)
    nf = A_QK // 4
    inv_freq = (ROPE_BASE ** (-np.arange(nf, dtype=np.float32) / nf)).astype(np.float32)
    p = np.arange(A_QK)
    f_idx = p % nf
    use_col = (p % A_QK) >= A_QK // 2
    sign = np.where((p % (2 * nf)) < nf, -1.0, 1.0)
    pos = np.where(use_col[None, :], col[:, None], row[:, None])
    ang = (pos * inv_freq[f_idx][None, :]).astype(np.float32).astype(np.float64)
    cos = np.cos(ang).astype(np.float32)
    sin = (np.sin(ang) * sign[None, :]).astype(np.float32)
    reps = LANES // A_QK
    return (jnp.asarray(np.tile(cos, (1, reps))), jnp.asarray(np.tile(sin, (1, reps))),
            jnp.asarray(np.ascontiguousarray(cos.T)), jnp.asarray(np.ascontiguousarray(sin.T)))


def _lambda_init(layer):
    return 0.8 - 0.6 * math.exp(-0.3 * layer)


def kernel(x, c, ctx, c_ctx, w_mod, b_mod, norm_pre, norm_post, w_in_even, lam_a, subln_a, conv_b,
           w_in_odd, rpb_c, w_out):
    nb, seq, d = x.shape
    lctx = ctx.shape[1]
    depth = w_mod.shape[0]
    assert seq % (NA_QROWS * GRID_W * NA_GROUPS_PER_STEP) == 0 and seq // GRID_W >= NA_KROWS
    assert lctx % LANES == 0 and seq % KEY_CHUNK == 0 and nb < 16

    pad = jnp.zeros((16 - nb - 1, d), F32)
    cc = jnp.concatenate([c, c_ctx[None, :], pad], axis=0)
    mods = _mod_all(cc, w_mod, b_mod)

    rope_tabs = _rope_tables(seq)
    cs_bf = _channel_dft_matrix()
    dft_tabs = _dft_factored_tables(seq)
    cn_c, sn_c = _dft_tables(lctx)

    tm_x = 512
    tm_c = lctx
    x_row = lambda b: b
    c_row = lambda b: nb

    x2 = x.reshape(nb * seq, d)
    c2 = ctx.reshape(nb * lctx, d)
    for l in range(depth):
        need_ctx = l < depth - 1
        j = l // 2
        mod3 = mods[l].reshape(16, 1, 3 * d)
        w_out_bf = w_out[l].astype(BF16)
        if l % 2 == 0:
            w_bf = w_in_even[j].astype(BF16)
            li = _lambda_init(l)
            qt, k, vt, rest = _inproj_even(x2, mod3, x_row, norm_pre[l], w_bf, seq, rope_tabs, tm_x)
            qct, kc, vct, rest_c = _inproj_even(c2, mod3, c_row, norm_pre[l], w_bf, lctx, None, tm_c)
            a = _diff_attn(lam_a[j], subln_a[j], qt, kc, vct, k, vt, rest, li, 512)
            x2n = _post(x2, a, None, rest, conv_b[j], mod3, x_row, norm_post[l], w_out_bf, seq, tm_x, True)
            if need_ctx:
                ac = _diff_attn(lam_a[j], subln_a[j], qct, kc, vct, None, None, rest_c, li, lctx)
                c2 = _post(c2, ac, None, rest_c, conv_b[j], mod3, c_row, norm_post[l], w_out_bf,
                           lctx, tm_c, True)
            x2 = x2n
        else:
            w_bf = w_in_odd[j].astype(BF16)
            q, k, v, rest, p, pq = _inproj_odd(x2, mod3, x_row, norm_pre[l], w_bf, cs_bf, seq, tm_x)
            qc, kc, vc, rest_c, p_c, pq_c = _inproj_odd(c2, mod3, c_row, norm_pre[l], w_bf, cs_bf, lctx, tm_c)
            bias = _na_bias(rpb_c[j], seq // GRID_W)
            a = _na_attn(q, k, v, kc, vc, bias, rest)
            y = _dft_positions_factored(dft_tabs, p, pq)
            x2n = _post(x2, a, y, rest, None, mod3, x_row, norm_post[l], w_out_bf, seq, tm_x, False)
            if need_ctx:
                ac = _dense_attn(qc, kc, vc, rest_c)
                yc = _dft_positions(cn_c, sn_c, p_c, pq_c, lctx).reshape(nb * lctx, -1)
                c2 = _post(c2, ac, yc, rest_c, None, mod3, c_row, norm_post[l], w_out_bf, lctx, tm_c, False)
            x2 = x2n
    return x2.reshape(nb, seq, d)
```

```python
import functools
import math

import numpy as np
import jax
import jax.numpy as jnp
from jax import lax
from jax.experimental import pallas as pl
from jax.experimental.pallas import tpu as pltpu

GRID_W = 64
A_HEADS = 8
A_QK = 32
A_V = 64
C_HEADS = 8
C_HEAD_DIM = 64
NA_KH = 8
NA_KW = 16
D_GROUPS = 8
D_GROUP_W = 64
CONV_W = 3
ROPE_BASE = 10000.0
EPS = 1e-6
LOG2E = 1.4426950408889634
NEG_BIG = -1e30

LANES = 128
HEAD_PAIR_W = 128
KEY_CHUNK = 256
DIFF_SUB_Q = 512
SCORE_AHEAD = 3
VMEM_LIMIT = 48 * 1024 * 1024

F32 = jnp.float32
BF16 = jnp.bfloat16


def _silu(v):
    return v * jax.nn.sigmoid(v)


def _params(n_axes):
    return pltpu.CompilerParams(dimension_semantics=("arbitrary",) * n_axes,
                                vmem_limit_bytes=VMEM_LIMIT)


def _nt_dot(a, b):
    return lax.dot_general(a, b, (((1,), (1,)), ((), ())), preferred_element_type=F32)


def _mod_kernel(cc_ref, w_ref, b_ref, o_ref):
    s = _silu(cc_ref[...]).astype(BF16)
    o_ref[0] = jnp.dot(s, w_ref[0].astype(BF16), preferred_element_type=F32) + b_ref[0]


def _mod_all(cc, w_mod, b_mod):
    depth, d, d3 = w_mod.shape
    tn = d
    return pl.pallas_call(
        _mod_kernel,
        grid=(depth, d3 // tn),
        in_specs=[pl.BlockSpec((cc.shape[0], d), lambda l, n: (0, 0)),
                  pl.BlockSpec((1, d, tn), lambda l, n: (l, 0, n)),
                  pl.BlockSpec((1, 1, tn), lambda l, n: (l, 0, n))],
        out_specs=pl.BlockSpec((1, cc.shape[0], tn), lambda l, n: (l, 0, n)),
        out_shape=jax.ShapeDtypeStruct((depth, cc.shape[0], d3), F32),
        compiler_params=_params(2), name="adaln_mod",
    )(cc, w_mod, b_mod.reshape(depth, 1, d3))


def _prenorm(x_ref, mod_ref, g_ref):
    d = x_ref.shape[-1]
    x = x_ref[...]
    y = x * lax.rsqrt(jnp.mean(x * x, axis=-1, keepdims=True) + EPS) * g_ref[...]
    m = mod_ref[0]
    h = y * (1.0 + m[:, d:2 * d]) + m[:, 0:d]
    return h.astype(BF16)


def _inproj_even_kernel(*refs, rope, tm):
    if rope:
        (x_ref, mod_ref, g_ref, w_ref, wqt_ref, wvt_ref,
         cos_ref, sin_ref, cost_ref, sint_ref, qt_ref, k_ref, vt_ref, rest_ref) = refs
    else:
        (x_ref, mod_ref, g_ref, w_ref, wqt_ref, wvt_ref, qt_ref, k_ref, vt_ref, rest_ref) = refs
    bw = qt_ref.shape[1]
    hb = _prenorm(x_ref, mod_ref, g_ref)
    pair = A_QK // 4

    def rotary(u, axis, cos, sin):
        idx = lax.broadcasted_iota(jnp.int32, u.shape, axis)
        partner = jnp.where((idx & pair) == 0, pltpu.roll(u, bw - pair, axis), pltpu.roll(u, pair, axis))
        return u * cos + partner * sin

    qt = _nt_dot(wqt_ref[...], hb)
    if rope:
        reps = bw // cost_ref.shape[0]
        qt = rotary(qt, 0, jnp.tile(cost_ref[...], (reps, 1)), jnp.tile(sint_ref[...], (reps, 1)))
    qt_ref[0] = (qt * (A_QK ** -0.5 * LOG2E)).astype(BF16)

    k = jnp.dot(hb, w_ref[:, bw:2 * bw], preferred_element_type=F32)
    if rope:
        reps = bw // cos_ref.shape[1]
        k = rotary(k, 1, jnp.tile(cos_ref[...], (1, reps)), jnp.tile(sin_ref[...], (1, reps)))
    k_ref[0] = k.astype(BF16)

    vt_ref[0] = _nt_dot(wvt_ref[...], hb).astype(BF16)

    n_rest = rest_ref.shape[1] // bw
    for c in range(n_rest):
        col = (3 + c) * bw
        rest_ref[:, c * bw:(c + 1) * bw] = jnp.dot(
            hb, w_ref[:, col:col + bw], preferred_element_type=F32).astype(BF16)


def _inproj_even(x2, mod3, mod_row_fn, g_pre, w_bf, seq, rope_tabs, tm):
    n, d = x2.shape
    bw = w_bf.shape[1] // 8
    nb = n // seq
    tps = seq // tm
    rope = rope_tabs is not None
    wqt = w_bf[:, 0:bw].T
    wvt = w_bf[:, 2 * bw:3 * bw].T
    const = lambda i: (0, 0)
    in_specs = [pl.BlockSpec((tm, d), lambda i: (i, 0)),
                pl.BlockSpec((1, 1, mod3.shape[2]), lambda i: (mod_row_fn(i // tps), 0, 0)),
                pl.BlockSpec((1, d), const),
                pl.BlockSpec(w_bf.shape, const),
                pl.BlockSpec(wqt.shape, const),
                pl.BlockSpec(wvt.shape, const)]
    args = [x2, mod3, g_pre.reshape(1, d), w_bf, wqt, wvt]
    if rope:
        cos_n, sin_n, cos_t, sin_t = rope_tabs
        in_specs += [pl.BlockSpec((tm, cos_n.shape[1]), lambda i: (i % tps, 0)),
                     pl.BlockSpec((tm, cos_n.shape[1]), lambda i: (i % tps, 0)),
                     pl.BlockSpec((cos_t.shape[0], tm), lambda i: (0, i % tps)),
                     pl.BlockSpec((cos_t.shape[0], tm), lambda i: (0, i % tps))]
        args += [cos_n, sin_n, cos_t, sin_t]
    out_specs = [pl.BlockSpec((1, bw, tm), lambda i: (i // tps, 0, i % tps)),
                 pl.BlockSpec((1, tm, bw), lambda i: (i // tps, i % tps, 0)),
                 pl.BlockSpec((1, bw, tm), lambda i: (i // tps, 0, i % tps)),
                 pl.BlockSpec((tm, 5 * bw), lambda i: (i, 0))]
    out_shape = [jax.ShapeDtypeStruct((nb, bw, seq), BF16),
                 jax.ShapeDtypeStruct((nb, seq, bw), BF16),
                 jax.ShapeDtypeStruct((nb, bw, seq), BF16),
                 jax.ShapeDtypeStruct((n, 5 * bw), BF16)]
    return pl.pallas_call(
        functools.partial(_inproj_even_kernel, rope=rope, tm=tm),
        grid=(n // tm,), in_specs=in_specs, out_specs=out_specs, out_shape=out_shape,
        compiler_params=_params(1), name="inproj_even",
    )(*args)


def _inproj_odd_kernel(x_ref, mod_ref, g_ref, w_ref, cs_ref, q_ref, k_ref, v_ref, rest_ref, p_ref, pq_ref):
    bw = q_ref.shape[2]
    hb = _prenorm(x_ref, mod_ref, g_ref)

    def proj(c):
        return jnp.dot(hb, w_ref[:, c * bw:(c + 1) * bw], preferred_element_type=F32)

    q_ref[0] = (proj(0) * (C_HEAD_DIM ** -0.5 * LOG2E)).astype(BF16)
    k_ref[0] = proj(1).astype(BF16)
    v_ref[0] = proj(2).astype(BF16)
    rest_ref[:, 0:bw] = proj(3).astype(BF16)
    rest_ref[:, bw:2 * bw] = proj(5).astype(BF16)
    pq = jnp.dot(proj(4).astype(BF16), cs_ref[...], preferred_element_type=F32)
    p_ref[0] = pq[:, 0:bw].astype(BF16)
    pq_ref[0] = pq[:, bw:2 * bw].astype(BF16)


def _inproj_odd(x2, mod3, mod_row_fn, g_pre, w_bf, cs_bf, seq, tm):
    n, d = x2.shape
    bw = w_bf.shape[1] // 6
    nb = n // seq
    tps = seq // tm
    const = lambda i: (0, 0)
    tok = pl.BlockSpec((1, tm, bw), lambda i: (i // tps, i % tps, 0))
    tok_shape = jax.ShapeDtypeStruct((nb, seq, bw), BF16)
    return pl.pallas_call(
        _inproj_odd_kernel,
        grid=(n // tm,),
        in_specs=[pl.BlockSpec((tm, d), lambda i: (i, 0)),
                  pl.BlockSpec((1, 1, mod3.shape[2]), lambda i: (mod_row_fn(i // tps), 0, 0)),
                  pl.BlockSpec((1, d), const),
                  pl.BlockSpec(w_bf.shape, const),
                  pl.BlockSpec(cs_bf.shape, const)],
        out_specs=[tok, tok, tok, pl.BlockSpec((tm, 2 * bw), lambda i: (i, 0)), tok, tok],
        out_shape=[tok_shape, tok_shape, tok_shape, jax.ShapeDtypeStruct((n, 2 * bw), BF16),
                   tok_shape, tok_shape],
        compiler_params=_params(1), name="inproj_odd",
    )(x2, mod3, g_pre.reshape(1, d), w_bf, cs_bf)


def _diff_attn_kernel(*refs, has_main, tq, lam_init):
    if has_main:
        (lam_ref, sub_ref, qt_ref, kc_ref, vct_ref, ka_ref, vat_ref, ga_ref, o_ref) = refs
        key_refs = [(kc_ref, vct_ref), (ka_ref, vat_ref)]
    else:
        (lam_ref, sub_ref, qt_ref, kc_ref, vct_ref, ga_ref, o_ref) = refs
        key_refs = [(kc_ref, vct_ref)]
    chunks = []
    for k_ref, vt_ref in key_refs:
        for st in range(0, k_ref.shape[1], KEY_CHUNK):
            chunks.append((k_ref, vt_ref, st, min(KEY_CHUNK, k_ref.shape[1] - st)))
    lp = lam_ref[...]
    lam = (jnp.exp(jnp.sum(lp[0:1] * lp[1:2], axis=1, keepdims=True))
           - jnp.exp(jnp.sum(lp[2:3] * lp[3:4], axis=1, keepdims=True)) + lam_init)
    sq = min(tq, DIFF_SUB_Q)
    qf = qt_ref[0].astype(F32)
    rowid = lax.broadcasted_iota(jnp.int32, (qf.shape[0], sq), 0)
    sub = jnp.tile(sub_ref[...], (1, sq // LANES))

    def fold(v):
        return v.reshape(v.shape[0] // 8, 8, sq)

    items = [(j, hh, m) for j in range(tq // sq) for hh in range(2) for m in range(2)]
    work = [(t, ci) for t in range(len(items)) for ci in range(len(chunks))]
    qms, state, maps, pending = {}, {}, {}, []

    def issue(t, ci):
        j, hh, m = items[t]
        if ci == 0:
            lo = hh * A_V + m * A_QK
            qms[t] = jnp.where((rowid >= lo) & (rowid < lo + A_QK), qf[:, j * sq:(j + 1) * sq],
                               0.0).astype(BF16)
        k_ref, _, st, sz = chunks[ci]
        sc = jnp.dot(k_ref[0, st:st + sz, :], qms[t], preferred_element_type=F32)
        pending.append((t, ci, sc, jnp.max(jnp.max(fold(sc), axis=0), axis=0, keepdims=True)))

    def consume():
        t, ci, s, cm = pending.pop(0)
        hh = items[t][1]
        _, vt_ref, st, sz = chunks[ci]
        if ci == 0:
            state[t] = (jnp.full((1, sq), -jnp.inf, F32), jnp.zeros((8, sq), F32),
                        jnp.zeros((A_V, sq), F32))
        m_run, l, acc = state[t]
        m_new = jnp.maximum(m_run, cm)
        alpha = jnp.exp2(m_run - m_new)
        e = jnp.exp2(s - m_new)
        l = alpha * l + jnp.sum(fold(e), axis=0)
        acc = alpha * acc + jnp.dot(vt_ref[0, hh * A_V:(hh + 1) * A_V, st:st + sz], e.astype(BF16),
                                    preferred_element_type=F32)
        state[t] = (m_new, l, acc)
        if ci == len(chunks) - 1:
            del state[t], qms[t]
            maps[t] = acc * (1.0 / jnp.sum(l, axis=0, keepdims=True))

    for idx in range(len(work) + SCORE_AHEAD):
        if idx < len(work):
            issue(*work[idx])
        if idx >= SCORE_AHEAD:
            consume()
    for j in range(tq // sq):
        heads = []
        for hh in range(2):
            ot = maps[4 * j + 2 * hh] - lam * maps[4 * j + 2 * hh + 1]
            ot = ot * lax.rsqrt(jnp.mean(ot * ot, axis=0, keepdims=True) + EPS) * sub * (1.0 - lam_init)
            heads.append(ot)
        o = jnp.concatenate(heads, axis=0).T
        rows = slice(j * sq, (j + 1) * sq)
        o_ref[rows, :] = o * _silu(ga_ref[rows, :].astype(F32))


def _diff_attn(lam_p, subln, qt, kc, vct, ka, vat, rest, lam_init, tq):
    nb, bw, sq = qt.shape
    nq = sq // tq
    hp = HEAD_PAIR_W
    sub_b = jnp.broadcast_to(subln.reshape(A_V, 1), (A_V, LANES))
    in_specs = [pl.BlockSpec(lam_p.shape, lambda b, h, i: (0, 0)),
                pl.BlockSpec(sub_b.shape, lambda b, h, i: (0, 0)),
                pl.BlockSpec((1, hp, tq), lambda b, h, i: (b, h, i)),
                pl.BlockSpec((1, kc.shape[1], hp), lambda b, h, i: (b, 0, h)),
                pl.BlockSpec((1, hp, kc.shape[1]), lambda b, h, i: (b, h, 0))]
    args = [lam_p, sub_b, qt, kc, vct]
    if ka is not None:
        in_specs += [pl.BlockSpec((1, ka.shape[1], hp), lambda b, h, i: (b, 0, h)),
                     pl.BlockSpec((1, hp, ka.shape[1]), lambda b, h, i: (b, h, 0))]
        args += [ka, vat]
    in_specs.append(pl.BlockSpec((tq, hp), lambda b, h, i: (b * nq + i, h)))
    args.append(rest)
    return pl.pallas_call(
        functools.partial(_diff_attn_kernel, has_main=ka is not None, tq=tq, lam_init=lam_init),
        grid=(nb, bw // hp, nq),
        in_specs=in_specs,
        out_specs=pl.BlockSpec((tq, hp), lambda b, h, i: (b * nq + i, h)),
        out_shape=jax.ShapeDtypeStruct((nb * sq, bw), F32),
        compiler_params=_params(3), name="diff_attn",
    )(*args)


NA_QROWS = 4
NA_KROWS = 12
NA_GROUPS_PER_STEP = 4


def _softmax_pv(problems):
    lane = lax.broadcasted_iota(jnp.int32, problems[0][0].shape, 1)
    chains = [(p, hh) for p in range(len(problems)) for hh in range(2)]
    scores, maxes, exps, outs = {}, {}, {}, {}
    for p, hh in chains:
        q, pieces, _ = problems[p]
        qm = jnp.where((lane >= hh * C_HEAD_DIM) & (lane < (hh + 1) * C_HEAD_DIM), q.astype(F32),
                       0.0).astype(BF16)
        scores[p, hh] = [_nt_dot(qm, k) if bias is None else _nt_dot(qm, k) + bias[hh]
                         for k, _, bias in pieces]
    for c in chains:
        maxes[c] = functools.reduce(jnp.maximum, [jnp.max(s, axis=1, keepdims=True) for s in scores[c]])
    for c in chains:
        exps[c] = [jnp.exp2(s - maxes[c]) for s in scores.pop(c)]
    for p, hh in chains:
        es = exps.pop((p, hh))
        l = functools.reduce(jnp.add, [jnp.sum(e, axis=1, keepdims=True) for e in es])
        pv = functools.reduce(jnp.add, [jnp.dot(e.astype(BF16), v, preferred_element_type=F32)
                                        for e, (_, v, _) in zip(es, problems[p][1])])
        outs[p, hh] = pv * (1.0 / l)
    return [jnp.where(lane < C_HEAD_DIM, outs[p, 0], outs[p, 1]) * _silu(problems[p][2].astype(F32))
            for p in range(len(problems))]


class _BiasOf:
    def __init__(self, ref):
        self.ref = ref

    def __getitem__(self, hh):
        return self.ref[0, hh]


def _na_kernel(q_ref, k_ref, v_ref, kc_ref, vc_ref, *rest, rows, n_sub):
    bias_refs, (gate_ref, o_ref) = rest[:n_sub], rest[n_sub:]
    tq = NA_QROWS * GRID_W
    nk = NA_KROWS * GRID_W
    problems = []
    for u in range(n_sub):
        g = pl.program_id(2) * n_sub + u
        base = jnp.clip(NA_QROWS * g - NA_KH // 2, 0, rows - NA_KROWS)
        start = pl.multiple_of(base * GRID_W, NA_QROWS * GRID_W)
        kw = k_ref[0, pl.ds(start, nk), :]
        vw = v_ref[0, pl.ds(start, nk), :]
        qrows = slice(u * tq, (u + 1) * tq)
        problems.append((q_ref[0, qrows, :],
                         [(kw, vw, _BiasOf(bias_refs[u])), (kc_ref[0], vc_ref[0], None)],
                         gate_ref[qrows, :]))
    for u, o in enumerate(_softmax_pv(problems)):
        o_ref[u * tq:(u + 1) * tq, :] = o


def _na_case_tables(rows):
    rs = np.clip(np.arange(rows) - NA_KH // 2, 0, rows - NA_KH)
    n_groups = rows // NA_QROWS
    pats = []
    for g in range(n_groups):
        base = int(np.clip(NA_QROWS * g - NA_KH // 2, 0, rows - NA_KROWS))
        pat = np.full((NA_QROWS, NA_KROWS), 2 * NA_KH - 1, np.int32)
        for j in range(NA_QROWS):
            r = NA_QROWS * g + j
            for i in range(NA_KROWS):
                kr = base + i
                if rs[r] <= kr < rs[r] + NA_KH:
                    pat[j, i] = kr - r + NA_KH - 1
            assert (pat[j] != 2 * NA_KH - 1).sum() == NA_KH
        pats.append(pat)
    cases = np.stack([pats[0], pats[1], pats[-1]])
    for g in range(n_groups):
        want = 0 if g == 0 else (2 if g == n_groups - 1 else 1)
        assert (pats[g] == cases[want]).all()
    return cases


def _na_bias(rpb, rows):
    h = rpb.shape[0]
    cq = np.arange(GRID_W)
    cs = np.clip(cq - NA_KW // 2, 0, GRID_W - NA_KW)
    kcol = np.arange(GRID_W)
    col_ok = (kcol[None, :] >= cs[:, None]) & (kcol[None, :] < cs[:, None] + NA_KW)
    col_off = np.clip(kcol[None, :] - cq[:, None] + NA_KW - 1, 0, 2 * NA_KW - 2)
    pick = ((col_off[None] == np.arange(2 * NA_KW - 1)[:, None, None]) & col_ok[None]).astype(np.float32)
    toep = jnp.dot((rpb * LOG2E).reshape(-1, 2 * NA_KW - 1), pick.reshape(2 * NA_KW - 1, -1),
                   precision=lax.Precision.HIGHEST).reshape(h, -1, GRID_W, GRID_W)
    toep = toep + np.where(col_ok, 0.0, NEG_BIG).astype(np.float32)
    toep = jnp.concatenate([toep, jnp.full((h, 1, GRID_W, GRID_W), NEG_BIG, F32)], axis=1)
    cases = _na_case_tables(rows)
    n_case, qr, kr = cases.shape
    return pl.pallas_call(
        functools.partial(_na_bias_kernel, cases=cases),
        grid=(h,),
        in_specs=[pl.BlockSpec((1,) + toep.shape[1:], lambda i: (i, 0, 0, 0))],
        out_specs=pl.BlockSpec((n_case, 1, qr * GRID_W, kr * GRID_W), lambda i: (0, i, 0, 0)),
        out_shape=jax.ShapeDtypeStruct((n_case, h, qr * GRID_W, kr * GRID_W), F32),
        compiler_params=_params(1), name="nbr_bias",
    )(toep)


def _na_bias_kernel(toep_ref, o_ref, *, cases):
    for c in range(cases.shape[0]):
        for j in range(cases.shape[1]):
            o_ref[c, 0, j * GRID_W:(j + 1) * GRID_W, :] = jnp.concatenate(
                [toep_ref[0, int(b)] for b in cases[c, j]], axis=1)


def _na_attn(q, k, v, kc, vc, bias, rest):
    nb, s, bw = q.shape
    hp = HEAD_PAIR_W
    tq = NA_QROWS * GRID_W
    ng = s // tq
    n_sub = NA_GROUPS_PER_STEP
    nsteps = ng // n_sub
    l = kc.shape[1]

    def bias_spec(u):
        def index(b, h, i):
            g = i * n_sub + u
            return (jnp.where(g == 0, 0, jnp.where(g == ng - 1, 2, 1)), h, 0, 0)
        return pl.BlockSpec((1, 2, tq, NA_KROWS * GRID_W), index)

    return pl.pallas_call(
        functools.partial(_na_kernel, rows=s // GRID_W, n_sub=n_sub),
        grid=(nb, bw // hp, nsteps),
        in_specs=[pl.BlockSpec((1, n_sub * tq, hp), lambda b, h, i: (b, i, h)),
                  pl.BlockSpec((1, s, hp), lambda b, h, i: (b, 0, h)),
                  pl.BlockSpec((1, s, hp), lambda b, h, i: (b, 0, h)),
                  pl.BlockSpec((1, l, hp), lambda b, h, i: (b, 0, h)),
                  pl.BlockSpec((1, l, hp), lambda b, h, i: (b, 0, h))]
                 + [bias_spec(u) for u in range(n_sub)]
                 + [pl.BlockSpec((n_sub * tq, hp), lambda b, h, i: (b * nsteps + i, h))],
        out_specs=pl.BlockSpec((n_sub * tq, hp), lambda b, h, i: (b * nsteps + i, h)),
        out_shape=jax.ShapeDtypeStruct((nb * s, bw), F32),
        compiler_params=_params(3), name="nbr_attn",
    )(q, k, v, kc, vc, *([bias] * n_sub), rest)


def _dense_attn_kernel(q_ref, k_ref, v_ref, gate_ref, o_ref):
    o_ref[...] = _softmax_pv([(q_ref[0], [(k_ref[0], v_ref[0], None)], gate_ref[...])])[0]


def _dense_attn(q, k, v, rest):
    nb, l, bw = q.shape
    hp = HEAD_PAIR_W
    tok = pl.BlockSpec((1, l, hp), lambda b, h: (b, 0, h))
    return pl.pallas_call(
        _dense_attn_kernel,
        grid=(nb, bw // hp),
        in_specs=[tok, tok, tok, pl.BlockSpec((l, hp), lambda b, h: (b, h))],
        out_specs=pl.BlockSpec((l, hp), lambda b, h: (b, h)),
        out_shape=jax.ShapeDtypeStruct((nb * l, bw), F32),
        compiler_params=_params(2), name="ctx_dense_attn",
    )(q, k, v, rest)


def _dft_kernel(c_ref, s_ref, p_ref, q_ref, o_ref):
    o_ref[0] = (jnp.dot(c_ref[...], p_ref[0], preferred_element_type=F32)
                + jnp.dot(s_ref[...], q_ref[0], preferred_element_type=F32))


def _dft_tables(n):
    idx = np.arange(n)
    ang = ((idx[:, None] * idx[None, :]) % n) * (2.0 * np.pi / n)
    scale = n ** -0.5
    return jnp.asarray(np.cos(ang) * scale, BF16), jnp.asarray(np.sin(ang) * scale, BF16)


def _dft_pos_kernel(m1_ref, m2_ref, g_ref, vr_ref, vi_ref, o_ref, a_s, *, nt):
    r = vr_ref.shape[1]
    for c in range(GRID_W // nt):
        cols = slice(c * nt, (c + 1) * nt)
        vrt = jnp.swapaxes(vr_ref[0, :, cols, :].astype(F32), 0, 1)
        vit = jnp.swapaxes(vi_ref[0, :, cols, :].astype(F32), 0, 1)
        outs = []
        for j in range(nt):
            outs.append(jnp.dot(m1_ref[...], vrt[j].astype(BF16), preferred_element_type=F32)
                        + jnp.dot(m2_ref[...], vit[j].astype(BF16), preferred_element_type=F32))
        at = jnp.swapaxes(jnp.stack(outs, axis=0), 0, 1)
        a_s[:, :, cols, :] = at.reshape(2, r, nt, LANES)
    k1t = min(nt, r)
    for c in range(r // k1t):
        outs = []
        for j in range(c * k1t, (c + 1) * k1t):
            a = jnp.concatenate([a_s[0, j], a_s[1, j]], axis=0).astype(BF16)
            outs.append(jnp.dot(g_ref[j], a, preferred_element_type=F32))
        o_ref[0, :, c * k1t:(c + 1) * k1t, :] = jnp.swapaxes(jnp.stack(outs, axis=0), 0, 1)


def _dft_factored_tables(n):
    r = n // GRID_W
    i = np.arange(r)
    ang = ((i[:, None] * i[None, :]) % r) * (2.0 * np.pi / r)
    c, s = np.cos(ang) * r ** -0.5, np.sin(ang) * r ** -0.5
    m1 = np.concatenate([c, -s], axis=0)
    m2 = np.concatenate([s, c], axis=0)
    k = i[:, None, None] + r * np.arange(GRID_W)[None, :, None]
    th = ((k * np.arange(GRID_W)[None, None, :]) % n) * (2.0 * np.pi / n)
    g = np.concatenate([np.cos(th), np.sin(th)], axis=2) * GRID_W ** -0.5
    return jnp.asarray(m1, BF16), jnp.asarray(m2, BF16), jnp.asarray(g, BF16)


def _dft_positions_factored(tabs, p, q):
    m1, m2, g = tabs
    nb, n, bw = p.shape
    r = n // GRID_W
    tok = pl.BlockSpec((1, r, GRID_W, LANES), lambda b, l: (b, 0, 0, l))
    y = pl.pallas_call(
        functools.partial(_dft_pos_kernel, nt=32),
        grid=(nb, bw // LANES),
        in_specs=[pl.BlockSpec(m1.shape, lambda b, l: (0, 0)),
                  pl.BlockSpec(m2.shape, lambda b, l: (0, 0)),
                  pl.BlockSpec(g.shape, lambda b, l: (0, 0, 0)),
                  tok, tok],
        out_specs=pl.BlockSpec((1, GRID_W, r, LANES), lambda b, l: (b, 0, 0, l)),
        out_shape=jax.ShapeDtypeStruct((nb, GRID_W, r, bw), F32),
        scratch_shapes=[pltpu.VMEM((2, r, GRID_W, LANES), F32)],
        compiler_params=_params(2), name="dft_positions_factored",
    )(m1, m2, g, p.reshape(nb, r, GRID_W, bw), q.reshape(nb, r, GRID_W, bw))
    return y.reshape(nb * n, bw)


def _dft_positions(cn, sn, p, q, tmo):
    nb, n, bw = p.shape
    return pl.pallas_call(
        _dft_kernel,
        grid=(n // tmo, nb),
        in_specs=[pl.BlockSpec((tmo, n), lambda i, b: (i, 0)),
                  pl.BlockSpec((tmo, n), lambda i, b: (i, 0)),
                  pl.BlockSpec((1, n, bw), lambda i, b: (b, 0, 0)),
                  pl.BlockSpec((1, n, bw), lambda i, b: (b, 0, 0))],
        out_specs=pl.BlockSpec((1, tmo, bw), lambda i, b: (b, i, 0)),
        out_shape=jax.ShapeDtypeStruct((nb, n, bw), F32),
        compiler_params=_params(2), name="dft_positions",
    )(cn, sn, p, q)


def _channel_dft_matrix():
    j = np.arange(D_GROUP_W)
    ang = 2.0 * np.pi * ((j[:, None] * j[None, :]) % D_GROUP_W) / D_GROUP_W
    eye = np.eye(D_GROUPS)
    cc = np.kron(eye, np.cos(ang) * D_GROUP_W ** -0.5)
    sc = np.kron(eye, np.sin(ang) * D_GROUP_W ** -0.5)
    return jnp.asarray(np.concatenate([cc, -sc], axis=1), BF16)


def _post_kernel(*refs, even, tm, seq):
    if even:
        (x_ref, a_ref, bb_ref, cb_ref, ub_ref, gb_ref, cbp_ref, ubp_ref, cbn_ref, ubn_ref,
         cw_ref, mod_ref, g_ref, w_ref, o_ref) = refs
    else:
        (x_ref, a_ref, y_ref, gd_ref, mod_ref, g_ref, w_ref, o_ref) = refs
    d = x_ref.shape[-1]
    bw = a_ref.shape[-1]
    if even:
        i = pl.program_id(0)
        t0 = (i * tm) % seq
        z = cb_ref[...].astype(F32) * ub_ref[...].astype(F32)
        zp = cbp_ref[7:8, :].astype(F32) * ubp_ref[7:8, :].astype(F32) * jnp.where(t0 == 0, 0.0, 1.0)
        zn = cbn_ref[0:1, :].astype(F32) * ubn_ref[0:1, :].astype(F32) * jnp.where(t0 + tm == seq, 0.0, 1.0)
        rid = lax.broadcasted_iota(jnp.int32, z.shape, 0)
        z_prev = jnp.where(rid == 0, zp, pltpu.roll(z, 1, 0))
        z_next = jnp.where(rid == tm - 1, zn, pltpu.roll(z, tm - 1, 0))
        cw = cw_ref[...]
        y = cw[0:1] * z_prev + cw[1:2] * z + cw[2:3] * z_next
        second = bb_ref[...].astype(F32) * y * _silu(gb_ref[...].astype(F32))
    else:
        second = y_ref[...] * _silu(gd_ref[...].astype(F32))
    yo = (jnp.dot(a_ref[...].astype(BF16), w_ref[0:bw, :], preferred_element_type=F32)
          + jnp.dot(second.astype(BF16), w_ref[bw:2 * bw, :], preferred_element_type=F32))
    nrm = yo * lax.rsqrt(jnp.mean(yo * yo, axis=-1, keepdims=True) + EPS) * g_ref[...]
    o_ref[...] = x_ref[...] + mod_ref[0][:, 2 * d:3 * d] * nrm


def _post(x2, a, second, rest, conv_w, mod3, mod_row_fn, g_post, w_out_bf, seq, tm, even):
    n, d = x2.shape
    bw = a.shape[1]
    tps = seq // tm
    const = lambda i: (0, 0)
    row = lambda i: (i, 0)
    in_specs = [pl.BlockSpec((tm, d), row), pl.BlockSpec((tm, bw), row)]
    args = [x2, a]
    if even:
        nblk8 = n // 8
        for c in (1, 2, 3, 4):
            in_specs.append(pl.BlockSpec((tm, bw), lambda i, c=c: (i, c)))
            args.append(rest)
        prev = lambda i, c: (jnp.maximum(i * (tm // 8) - 1, 0), c)
        nxt = lambda i, c: (jnp.minimum((i + 1) * (tm // 8), nblk8 - 1), c)
        for fn in (prev, nxt):
            for c in (2, 3):
                in_specs.append(pl.BlockSpec((8, bw), functools.partial(fn, c=c)))
                args.append(rest)
        in_specs.append(pl.BlockSpec(conv_w.shape, const))
        args.append(conv_w)
    else:
        in_specs += [pl.BlockSpec((tm, bw), row), pl.BlockSpec((tm, bw), lambda i: (i, 1))]
        args += [second, rest]
    in_specs += [pl.BlockSpec((1, 1, mod3.shape[2]), lambda i: (mod_row_fn(i // tps), 0, 0)),
                 pl.BlockSpec((1, d), const),
                 pl.BlockSpec(w_out_bf.shape, const)]
    args += [mod3, g_post.reshape(1, d), w_out_bf]
    return pl.pallas_call(
        functools.partial(_post_kernel, even=even, tm=tm, seq=seq),
        grid=(n // tm,), in_specs=in_specs,
        out_specs=pl.BlockSpec((tm, d), row),
        out_shape=jax.ShapeDtypeStruct((n, d), F32),
        compiler_params=_params(1), name="post_even" if even else "post_odd",
    )(*args)


def _rope_tables(seq):
    t = np.arange(seq)
    row = (t // GRID_W).astype(np.float32)
    col = (t % GRID_W).astype(np.float32)
    nf = A_QK // 4
    inv_freq = (ROPE_BASE ** (-np.arange(nf, dtype=np.float32) / nf)).astype(np.float32)
    p = np.arange(A_QK)
    f_idx = p % nf
    use_col = (p % A_QK) >= A_QK // 2
    sign = np.where((p % (2 * nf)) < nf, -1.0, 1.0)
    pos = np.where(use_col[None, :], col[:, None], row[:, None])
    ang = (pos * inv_freq[f_idx][None, :]).astype(np.float32).astype(np.float64)
    cos = np.cos(ang).astype(np.float32)
    sin = (np.sin(ang) * sign[None, :]).astype(np.float32)
    reps = LANES // A_QK
    return (jnp.asarray(np.tile(cos, (1, reps))), jnp.asarray(np.tile(sin, (1, reps))),
            jnp.asarray(np.ascontiguousarray(cos.T)), jnp.asarray(np.ascontiguousarray(sin.T)))


def _lambda_init(layer):
    return 0.8 - 0.6 * math.exp(-0.3 * layer)


def kernel(x, c, ctx, c_ctx, w_mod, b_mod, norm_pre, norm_post, w_in_even, lam_a, subln_a, conv_b,
           w_in_odd, rpb_c, w_out):
    nb, seq, d = x.shape
    lctx = ctx.shape[1]
    depth = w_mod.shape[0]
    assert seq % (NA_QROWS * GRID_W * NA_GROUPS_PER_STEP) == 0 and seq // GRID_W >= NA_KROWS
    assert lctx % LANES == 0 and seq % KEY_CHUNK == 0 and nb < 16

    pad = jnp.zeros((16 - nb - 1, d), F32)
    cc = jnp.concatenate([c, c_ctx[None, :], pad], axis=0)
    mods = _mod_all(cc, w_mod, b_mod)

    rope_tabs = _rope_tables(seq)
    cs_bf = _channel_dft_matrix()
    dft_tabs = _dft_factored_tables(seq)
    cn_c, sn_c = _dft_tables(lctx)

    tm_x = 512
    tm_c = lctx
    x_row = lambda b: b
    c_row = lambda b: nb

    x2 = x.reshape(nb * seq, d)
    c2 = ctx.reshape(nb * lctx, d)
    for l in range(depth):
        need_ctx = l < depth - 1
        j = l // 2
        mod3 = mods[l].reshape(16, 1, 3 * d)
        w_out_bf = w_out[l].astype(BF16)
        if l % 2 == 0:
            w_bf = w_in_even[j].astype(BF16)
            li = _lambda_init(l)
            qt, k, vt, rest = _inproj_even(x2, mod3, x_row, norm_pre[l], w_bf, seq, rope_tabs, tm_x)
            qct, kc, vct, rest_c = _inproj_even(c2, mod3, c_row, norm_pre[l], w_bf, lctx, None, tm_c)
            a = _diff_attn(lam_a[j], subln_a[j], qt, kc, vct, k, vt, rest, li, 512)
            x2n = _post(x2, a, None, rest, conv_b[j], mod3, x_row, norm_post[l], w_out_bf, seq, tm_x, True)
            if need_ctx:
                ac = _diff_attn(lam_a[j], subln_a[j], qct, kc, vct, None, None, rest_c, li, lctx)
                c2 = _post(c2, ac, None, rest_c, conv_b[j], mod3, c_row, norm_post[l], w_out_bf,
                           lctx, tm_c, True)
            x2 = x2n
        else:
            w_bf = w_in_odd[j].astype(BF16)
            q, k, v, rest, p, pq = _inproj_odd(x2, mod3, x_row, norm_pre[l], w_bf, cs_bf, seq, tm_x)
            qc, kc, vc, rest_c, p_c, pq_c = _inproj_odd(c2, mod3, c_row, norm_pre[l], w_bf, cs_bf, lctx, tm_c)
            bias = _na_bias(rpb_c[j], seq // GRID_W)
            a = _na_attn(q, k, v, kc, vc, bias, rest)
            y = _dft_positions_factored(dft_tabs, p, pq)
            x2n = _post(x2, a, y, rest, None, mod3, x_row, norm_post[l], w_out_bf, seq, tm_x, False)
            if need_ctx:
                ac = _dense_attn(qc, kc, vc, rest_c)
                yc = _dft_positions(cn_c, sn_c, p_c, pq_c, lctx).reshape(nb * lctx, -1)
                c2 = _post(c2, ac, yc, rest_c, None, mod3, c_row, norm_post[l], w_out_bf, lctx, tm_c, False)
            x2 = x2n
    return x2.reshape(nb, seq, d)
```

```python
import functools
import math

import numpy as np
import jax
import jax.numpy as jnp
from jax import lax
from jax.experimental import pallas as pl
from jax.experimental.pallas import tpu as pltpu

GRID_W = 64
A_QK = 32
A_V = 64
C_HEAD_DIM = 64
NA_KH = 8
NA_KW = 16
D_GROUPS = 8
D_GROUP_W = 64
ROPE_BASE = 10000.0
EPS = 1e-6
LOG2E = 1.4426950408889634
NEG_BIG = -1e30

LANES = 128
HEAD_PAIR_W = 128
KEY_CHUNK = 256
DIFF_SUB_Q = 512
SCORE_AHEAD = 3
VMEM_LIMIT = 48 * 1024 * 1024

F32 = jnp.float32
BF16 = jnp.bfloat16
MIX_DTYPE = BF16


def _silu(v):
    return v * jax.nn.sigmoid(v)


def _params(n_axes):
    return pltpu.CompilerParams(dimension_semantics=("arbitrary",) * n_axes,
                                vmem_limit_bytes=VMEM_LIMIT)


def _resident(shape):
    return pl.BlockSpec(shape, lambda *_: (0,) * len(shape), pipeline_mode=pl.Buffered(1))


def _nt_dot(a, b):
    return lax.dot_general(a, b, (((1,), (1,)), ((), ())), preferred_element_type=F32)


def _mod_kernel(cc_ref, w_ref, b_ref, o_ref):
    s = _silu(cc_ref[...]).astype(BF16)
    o_ref[0] = jnp.dot(s, w_ref[0].astype(BF16), preferred_element_type=F32) + b_ref[0]


def _mod_all(cc, w_mod, b_mod):
    depth, d, d3 = w_mod.shape
    tn = d
    return pl.pallas_call(
        _mod_kernel,
        grid=(depth, d3 // tn),
        in_specs=[pl.BlockSpec((cc.shape[0], d), lambda l, n: (0, 0)),
                  pl.BlockSpec((1, d, tn), lambda l, n: (l, 0, n)),
                  pl.BlockSpec((1, 1, tn), lambda l, n: (l, 0, n))],
        out_specs=pl.BlockSpec((1, cc.shape[0], tn), lambda l, n: (l, 0, n)),
        out_shape=jax.ShapeDtypeStruct((depth, cc.shape[0], d3), F32),
        compiler_params=_params(2), name="adaln_mod",
    )(cc, w_mod, b_mod.reshape(depth, 1, d3))


def _prenorm(x_ref, mod_ref, g_ref):
    d = x_ref.shape[-1]
    x = x_ref[...]
    y = x * lax.rsqrt(jnp.mean(x * x, axis=-1, keepdims=True) + EPS) * g_ref[...]
    m = mod_ref[0]
    h = y * (1.0 + m[:, d:2 * d]) + m[:, 0:d]
    return h.astype(BF16)


def _inproj_even_kernel(*refs, rope):
    if rope:
        (x_ref, mod_ref, g_ref, w_ref, wqt_ref, wvt_ref,
         cos_ref, sin_ref, cost_ref, sint_ref, qt_ref, k_ref, vt_ref, rest_ref) = refs
    else:
        (x_ref, mod_ref, g_ref, w_ref, wqt_ref, wvt_ref, qt_ref, k_ref, vt_ref, rest_ref) = refs
    bw = qt_ref.shape[1]
    hb = _prenorm(x_ref, mod_ref, g_ref)
    pair = A_QK // 4

    def rotary(u, axis, cos, sin):
        idx = lax.broadcasted_iota(jnp.int32, u.shape, axis)
        partner = jnp.where((idx & pair) == 0, pltpu.roll(u, bw - pair, axis), pltpu.roll(u, pair, axis))
        return u * cos + partner * sin

    qt = _nt_dot(wqt_ref[...], hb)
    if rope:
        reps = bw // cost_ref.shape[0]
        qt = rotary(qt, 0, jnp.tile(cost_ref[...], (reps, 1)), jnp.tile(sint_ref[...], (reps, 1)))
    qt_ref[0] = (qt * (A_QK ** -0.5 * LOG2E)).astype(BF16)

    k = jnp.dot(hb, w_ref[:, bw:2 * bw], preferred_element_type=F32)
    if rope:
        reps = bw // cos_ref.shape[1]
        k = rotary(k, 1, jnp.tile(cos_ref[...], (1, reps)), jnp.tile(sin_ref[...], (1, reps)))
    k_ref[0] = k.astype(BF16)

    vt_ref[0] = _nt_dot(wvt_ref[...], hb).astype(BF16)

    n_rest = rest_ref.shape[1] // bw
    for c in range(n_rest):
        col = (3 + c) * bw
        rest_ref[:, c * bw:(c + 1) * bw] = jnp.dot(
            hb, w_ref[:, col:col + bw], preferred_element_type=F32).astype(BF16)


def _inproj_even(x2, mod3, mod_row_fn, g_pre, w_bf, seq, rope_tabs, tm):
    n, d = x2.shape
    bw = w_bf.shape[1] // 8
    nb = n // seq
    tps = seq // tm
    rope = rope_tabs is not None
    wqt = w_bf[:, 0:bw].T
    wvt = w_bf[:, 2 * bw:3 * bw].T
    const = lambda i: (0, 0)
    in_specs = [pl.BlockSpec((tm, d), lambda i: (i, 0)),
                pl.BlockSpec((1, 1, mod3.shape[2]), lambda i: (mod_row_fn(i // tps), 0, 0)),
                pl.BlockSpec((1, d), const),
                _resident(w_bf.shape), _resident(wqt.shape), _resident(wvt.shape)]
    args = [x2, mod3, g_pre.reshape(1, d), w_bf, wqt, wvt]
    if rope:
        cos_n, sin_n, cos_t, sin_t = rope_tabs
        in_specs += [pl.BlockSpec((tm, cos_n.shape[1]), lambda i: (i % tps, 0)),
                     pl.BlockSpec((tm, cos_n.shape[1]), lambda i: (i % tps, 0)),
                     pl.BlockSpec((cos_t.shape[0], tm), lambda i: (0, i % tps)),
                     pl.BlockSpec((cos_t.shape[0], tm), lambda i: (0, i % tps))]
        args += [cos_n, sin_n, cos_t, sin_t]
    out_specs = [pl.BlockSpec((1, bw, tm), lambda i: (i // tps, 0, i % tps)),
                 pl.BlockSpec((1, tm, bw), lambda i: (i // tps, i % tps, 0)),
                 pl.BlockSpec((1, bw, tm), lambda i: (i // tps, 0, i % tps)),
                 pl.BlockSpec((tm, 5 * bw), lambda i: (i, 0))]
    out_shape = [jax.ShapeDtypeStruct((nb, bw, seq), BF16),
                 jax.ShapeDtypeStruct((nb, seq, bw), BF16),
                 jax.ShapeDtypeStruct((nb, bw, seq), BF16),
                 jax.ShapeDtypeStruct((n, 5 * bw), BF16)]
    return pl.pallas_call(
        functools.partial(_inproj_even_kernel, rope=rope),
        grid=(n // tm,), in_specs=in_specs, out_specs=out_specs, out_shape=out_shape,
        compiler_params=_params(1), name="inproj_even",
    )(*args)


def _inproj_odd_kernel(x_ref, mod_ref, g_ref, w_ref, cs_ref, q_ref, k_ref, v_ref, rest_ref, p_ref, pq_ref):
    bw = q_ref.shape[2]
    hb = _prenorm(x_ref, mod_ref, g_ref)

    def proj(c):
        return jnp.dot(hb, w_ref[:, c * bw:(c + 1) * bw], preferred_element_type=F32)

    q_ref[0] = (proj(0) * (C_HEAD_DIM ** -0.5 * LOG2E)).astype(BF16)
    k_ref[0] = proj(1).astype(BF16)
    v_ref[0] = proj(2).astype(BF16)
    rest_ref[:, 0:bw] = proj(3).astype(BF16)
    rest_ref[:, bw:2 * bw] = proj(5).astype(BF16)
    pq = jnp.dot(proj(4).astype(BF16), cs_ref[...], preferred_element_type=F32)
    p_ref[0] = pq[:, 0:bw].astype(BF16)
    pq_ref[0] = pq[:, bw:2 * bw].astype(BF16)


def _inproj_odd(x2, mod3, mod_row_fn, g_pre, w_bf, cs_bf, seq, tm):
    n, d = x2.shape
    bw = w_bf.shape[1] // 6
    nb = n // seq
    tps = seq // tm
    const = lambda i: (0, 0)
    tok = pl.BlockSpec((1, tm, bw), lambda i: (i // tps, i % tps, 0))
    tok_shape = jax.ShapeDtypeStruct((nb, seq, bw), BF16)
    return pl.pallas_call(
        _inproj_odd_kernel,
        grid=(n // tm,),
        in_specs=[pl.BlockSpec((tm, d), lambda i: (i, 0)),
                  pl.BlockSpec((1, 1, mod3.shape[2]), lambda i: (mod_row_fn(i // tps), 0, 0)),
                  pl.BlockSpec((1, d), const),
                  _resident(w_bf.shape), _resident(cs_bf.shape)],
        out_specs=[tok, tok, tok, pl.BlockSpec((tm, 2 * bw), lambda i: (i, 0)), tok, tok],
        out_shape=[tok_shape, tok_shape, tok_shape, jax.ShapeDtypeStruct((n, 2 * bw), BF16),
                   tok_shape, tok_shape],
        compiler_params=_params(1), name="inproj_odd",
    )(x2, mod3, g_pre.reshape(1, d), w_bf, cs_bf)


def _diff_attn_kernel(*refs, has_main, tq, lam_init):
    if has_main:
        (lam_ref, sub_ref, qt_ref, kc_ref, vct_ref, ka_ref, vat_ref, ga_ref, o_ref) = refs
        key_refs = [(kc_ref, vct_ref), (ka_ref, vat_ref)]
    else:
        (lam_ref, sub_ref, qt_ref, kc_ref, vct_ref, ga_ref, o_ref) = refs
        key_refs = [(kc_ref, vct_ref)]
    chunks = []
    for k_ref, vt_ref in key_refs:
        for st in range(0, k_ref.shape[1], KEY_CHUNK):
            chunks.append((k_ref, vt_ref, st, min(KEY_CHUNK, k_ref.shape[1] - st)))
    lp = lam_ref[...]
    lam = (jnp.exp(jnp.sum(lp[0:1] * lp[1:2], axis=1, keepdims=True))
           - jnp.exp(jnp.sum(lp[2:3] * lp[3:4], axis=1, keepdims=True)) + lam_init)
    sq = min(tq, DIFF_SUB_Q)
    qf = qt_ref[0].astype(F32)
    rowid = lax.broadcasted_iota(jnp.int32, (qf.shape[0], sq), 0)
    sub = jnp.tile(sub_ref[...], (1, sq // LANES))

    def fold(v):
        return v.reshape(v.shape[0] // 8, 8, sq)

    items = [(j, hh, m) for j in range(tq // sq) for hh in range(2) for m in range(2)]
    work = [(t, ci) for t in range(len(items)) for ci in range(len(chunks))]
    qms, state, maps, pending = {}, {}, {}, []

    def issue(t, ci):
        j, hh, m = items[t]
        if ci == 0:
            lo = hh * A_V + m * A_QK
            qms[t] = jnp.where((rowid >= lo) & (rowid < lo + A_QK), qf[:, j * sq:(j + 1) * sq],
                               0.0).astype(BF16)
        k_ref, _, st, sz = chunks[ci]
        sc = jnp.dot(k_ref[0, st:st + sz, :], qms[t], preferred_element_type=F32)
        pending.append((t, ci, sc, jnp.max(jnp.max(fold(sc), axis=0), axis=0, keepdims=True)))

    def consume():
        t, ci, s, cm = pending.pop(0)
        hh = items[t][1]
        _, vt_ref, st, sz = chunks[ci]
        if ci == 0:
            state[t] = (jnp.full((1, sq), -jnp.inf, F32), jnp.zeros((8, sq), F32),
                        jnp.zeros((A_V, sq), F32))
        m_run, l, acc = state[t]
        m_new = jnp.maximum(m_run, cm)
        alpha = jnp.exp2(m_run - m_new)
        e = jnp.exp2(s - m_new)
        l = alpha * l + jnp.sum(fold(e), axis=0)
        acc = alpha * acc + jnp.dot(vt_ref[0, hh * A_V:(hh + 1) * A_V, st:st + sz], e.astype(BF16),
                                    preferred_element_type=F32)
        state[t] = (m_new, l, acc)
        if ci == len(chunks) - 1:
            del state[t], qms[t]
            maps[t] = acc * (1.0 / jnp.sum(l, axis=0, keepdims=True))

    for idx in range(len(work) + SCORE_AHEAD):
        if idx < len(work):
            issue(*work[idx])
        if idx >= SCORE_AHEAD:
            consume()
    for j in range(tq // sq):
        heads = []
        for hh in range(2):
            ot = maps[4 * j + 2 * hh] - lam * maps[4 * j + 2 * hh + 1]
            ot = ot * lax.rsqrt(jnp.mean(ot * ot, axis=0, keepdims=True) + EPS) * sub * (1.0 - lam_init)
            heads.append(ot)
        o = jnp.concatenate(heads, axis=0).T
        rows = slice(j * sq, (j + 1) * sq)
        o_ref[rows, :] = (o * _silu(ga_ref[rows, :].astype(F32))).astype(o_ref.dtype)


def _diff_attn(lam_p, subln, qt, kc, vct, ka, vat, rest, lam_init, tq):
    nb, bw, sq = qt.shape
    nq = sq // tq
    hp = HEAD_PAIR_W
    sub_b = jnp.broadcast_to(subln.reshape(A_V, 1), (A_V, LANES))
    in_specs = [pl.BlockSpec(lam_p.shape, lambda b, h, i: (0, 0)),
                pl.BlockSpec(sub_b.shape, lambda b, h, i: (0, 0)),
                pl.BlockSpec((1, hp, tq), lambda b, h, i: (b, h, i)),
                pl.BlockSpec((1, kc.shape[1], hp), lambda b, h, i: (b, 0, h)),
                pl.BlockSpec((1, hp, kc.shape[1]), lambda b, h, i: (b, h, 0))]
    args = [lam_p, sub_b, qt, kc, vct]
    if ka is not None:
        in_specs += [pl.BlockSpec((1, ka.shape[1], hp), lambda b, h, i: (b, 0, h)),
                     pl.BlockSpec((1, hp, ka.shape[1]), lambda b, h, i: (b, h, 0))]
        args += [ka, vat]
    in_specs.append(pl.BlockSpec((tq, hp), lambda b, h, i: (b * nq + i, h)))
    args.append(rest)
    return pl.pallas_call(
        functools.partial(_diff_attn_kernel, has_main=ka is not None, tq=tq, lam_init=lam_init),
        grid=(nb, bw // hp, nq),
        in_specs=in_specs,
        out_specs=pl.BlockSpec((tq, hp), lambda b, h, i: (b * nq + i, h)),
        out_shape=jax.ShapeDtypeStruct((nb * sq, bw), MIX_DTYPE),
        compiler_params=_params(3), name="diff_attn",
    )(*args)


NA_QROWS = 4
NA_KROWS = 12
NA_GROUPS_PER_STEP = 4


def _softmax_pv(problems):
    lane = lax.broadcasted_iota(jnp.int32, problems[0][0].shape, 1)
    chains = [(p, hh) for p in range(len(problems)) for hh in range(2)]
    scores, maxes, exps, outs = {}, {}, {}, {}
    for p, hh in chains:
        q, pieces, _ = problems[p]
        qm = jnp.where((lane >= hh * C_HEAD_DIM) & (lane < (hh + 1) * C_HEAD_DIM), q.astype(F32),
                       0.0).astype(BF16)
        scores[p, hh] = [_nt_dot(qm, k) if bias is None else _nt_dot(qm, k) + bias[hh]
                         for k, _, bias in pieces]
    for c in chains:
        maxes[c] = functools.reduce(jnp.maximum, [jnp.max(s, axis=1, keepdims=True) for s in scores[c]])
    for c in chains:
        exps[c] = [jnp.exp2(s - maxes[c]) for s in scores.pop(c)]
    for p, hh in chains:
        es = exps.pop((p, hh))
        l = functools.reduce(jnp.add, [jnp.sum(e, axis=1, keepdims=True) for e in es])
        pv = functools.reduce(jnp.add, [jnp.dot(e.astype(BF16), v, preferred_element_type=F32)
                                        for e, (_, v, _) in zip(es, problems[p][1])])
        outs[p, hh] = pv * (1.0 / l)
    return [jnp.where(lane < C_HEAD_DIM, outs[p, 0], outs[p, 1]) * _silu(problems[p][2].astype(F32))
            for p in range(len(problems))]


class _BiasOf:
    def __init__(self, ref):
        self.ref = ref

    def __getitem__(self, hh):
        return self.ref[0, hh]


def _na_kernel(q_ref, k_ref, v_ref, kc_ref, vc_ref, *rest, rows, n_sub):
    bias_refs, (gate_ref, o_ref) = rest[:n_sub], rest[n_sub:]
    tq = NA_QROWS * GRID_W
    nk = NA_KROWS * GRID_W
    problems = []
    for u in range(n_sub):
        g = pl.program_id(2) * n_sub + u
        base = jnp.clip(NA_QROWS * g - NA_KH // 2, 0, rows - NA_KROWS)
        start = pl.multiple_of(base * GRID_W, NA_QROWS * GRID_W)
        kw = k_ref[0, pl.ds(start, nk), :]
        vw = v_ref[0, pl.ds(start, nk), :]
        qrows = slice(u * tq, (u + 1) * tq)
        problems.append((q_ref[0, qrows, :],
                         [(kw, vw, _BiasOf(bias_refs[u])), (kc_ref[0], vc_ref[0], None)],
                         gate_ref[qrows, :]))
    for u, o in enumerate(_softmax_pv(problems)):
        o_ref[u * tq:(u + 1) * tq, :] = o.astype(o_ref.dtype)


def _na_case_tables(rows):
    rs = np.clip(np.arange(rows) - NA_KH // 2, 0, rows - NA_KH)
    n_groups = rows // NA_QROWS
    pats = []
    for g in range(n_groups):
        base = int(np.clip(NA_QROWS * g - NA_KH // 2, 0, rows - NA_KROWS))
        pat = np.full((NA_QROWS, NA_KROWS), 2 * NA_KH - 1, np.int32)
        for j in range(NA_QROWS):
            r = NA_QROWS * g + j
            for i in range(NA_KROWS):
                kr = base + i
                if rs[r] <= kr < rs[r] + NA_KH:
                    pat[j, i] = kr - r + NA_KH - 1
            assert (pat[j] != 2 * NA_KH - 1).sum() == NA_KH
        pats.append(pat)
    cases = np.stack([pats[0], pats[1], pats[-1]])
    for g in range(n_groups):
        want = 0 if g == 0 else (2 if g == n_groups - 1 else 1)
        assert (pats[g] == cases[want]).all()
    return cases


def _na_bias(rpb, rows):
    h = rpb.shape[0]
    cq = np.arange(GRID_W)
    cs = np.clip(cq - NA_KW // 2, 0, GRID_W - NA_KW)
    kcol = np.arange(GRID_W)
    col_ok = (kcol[None, :] >= cs[:, None]) & (kcol[None, :] < cs[:, None] + NA_KW)
    col_off = np.clip(kcol[None, :] - cq[:, None] + NA_KW - 1, 0, 2 * NA_KW - 2)
    pick = ((col_off[None] == np.arange(2 * NA_KW - 1)[:, None, None]) & col_ok[None]).astype(np.float32)
    toep = jnp.dot((rpb * LOG2E).reshape(-1, 2 * NA_KW - 1), pick.reshape(2 * NA_KW - 1, -1),
                   precision=lax.Precision.HIGHEST).reshape(h, -1, GRID_W, GRID_W)
    toep = toep + np.where(col_ok, 0.0, NEG_BIG).astype(np.float32)
    toep = jnp.concatenate([toep, jnp.full((h, 1, GRID_W, GRID_W), NEG_BIG, F32)], axis=1)
    cases = _na_case_tables(rows)
    n_case, qr, kr = cases.shape
    return pl.pallas_call(
        functools.partial(_na_bias_kernel, cases=cases),
        grid=(h,),
        in_specs=[pl.BlockSpec((1,) + toep.shape[1:], lambda i: (i, 0, 0, 0))],
        out_specs=pl.BlockSpec((n_case, 1, qr * GRID_W, kr * GRID_W), lambda i: (0, i, 0, 0)),
        out_shape=jax.ShapeDtypeStruct((n_case, h, qr * GRID_W, kr * GRID_W), F32),
        compiler_params=_params(1), name="nbr_bias",
    )(toep)


def _na_bias_kernel(toep_ref, o_ref, *, cases):
    for c in range(cases.shape[0]):
        for j in range(cases.shape[1]):
            o_ref[c, 0, j * GRID_W:(j + 1) * GRID_W, :] = jnp.concatenate(
                [toep_ref[0, int(b)] for b in cases[c, j]], axis=1)


def _na_attn(q, k, v, kc, vc, bias, rest):
    nb, s, bw = q.shape
    hp = HEAD_PAIR_W
    tq = NA_QROWS * GRID_W
    ng = s // tq
    n_sub = NA_GROUPS_PER_STEP
    nsteps = ng // n_sub
    l = kc.shape[1]

    def bias_spec(u):
        def index(b, h, i):
            g = i * n_sub + u
            return (jnp.where(g == 0, 0, jnp.where(g == ng - 1, 2, 1)), h, 0, 0)
        return pl.BlockSpec((1, 2, tq, NA_KROWS * GRID_W), index)

    return pl.pallas_call(
        functools.partial(_na_kernel, rows=s // GRID_W, n_sub=n_sub),
        grid=(nb, bw // hp, nsteps),
        in_specs=[pl.BlockSpec((1, n_sub * tq, hp), lambda b, h, i: (b, i, h)),
                  pl.BlockSpec((1, s, hp), lambda b, h, i: (b, 0, h)),
                  pl.BlockSpec((1, s, hp), lambda b, h, i: (b, 0, h)),
                  pl.BlockSpec((1, l, hp), lambda b, h, i: (b, 0, h)),
                  pl.BlockSpec((1, l, hp), lambda b, h, i: (b, 0, h))]
                 + [bias_spec(u) for u in range(n_sub)]
                 + [pl.BlockSpec((n_sub * tq, hp), lambda b, h, i: (b * nsteps + i, h))],
        out_specs=pl.BlockSpec((n_sub * tq, hp), lambda b, h, i: (b * nsteps + i, h)),
        out_shape=jax.ShapeDtypeStruct((nb * s, bw), MIX_DTYPE),
        compiler_params=_params(3), name="nbr_attn",
    )(q, k, v, kc, vc, *([bias] * n_sub), rest)


def _dense_attn_kernel(q_ref, k_ref, v_ref, gate_ref, o_ref):
    o = _softmax_pv([(q_ref[0], [(k_ref[0], v_ref[0], None)], gate_ref[...])])[0]
    o_ref[...] = o.astype(o_ref.dtype)


def _dense_attn(q, k, v, rest):
    nb, l, bw = q.shape
    hp = HEAD_PAIR_W
    tok = pl.BlockSpec((1, l, hp), lambda b, h: (b, 0, h))
    return pl.pallas_call(
        _dense_attn_kernel,
        grid=(nb, bw // hp),
        in_specs=[tok, tok, tok, pl.BlockSpec((l, hp), lambda b, h: (b, h))],
        out_specs=pl.BlockSpec((l, hp), lambda b, h: (b, h)),
        out_shape=jax.ShapeDtypeStruct((nb * l, bw), MIX_DTYPE),
        compiler_params=_params(2), name="ctx_dense_attn",
    )(q, k, v, rest)


def _dft_kernel(c_ref, s_ref, p_ref, q_ref, o_ref):
    o_ref[0] = (jnp.dot(c_ref[...], p_ref[0], preferred_element_type=F32)
                + jnp.dot(s_ref[...], q_ref[0], preferred_element_type=F32))


def _dft_tables(n):
    idx = np.arange(n)
    ang = ((idx[:, None] * idx[None, :]) % n) * (2.0 * np.pi / n)
    scale = n ** -0.5
    return jnp.asarray(np.cos(ang) * scale, BF16), jnp.asarray(np.sin(ang) * scale, BF16)


def _dft_pos_kernel(m1_ref, m2_ref, g_ref, vr_ref, vi_ref, o_ref, a_s, *, nt):
    r = vr_ref.shape[1]
    for c in range(GRID_W // nt):
        cols = slice(c * nt, (c + 1) * nt)
        vrt = jnp.swapaxes(vr_ref[0, :, cols, :].astype(F32), 0, 1)
        vit = jnp.swapaxes(vi_ref[0, :, cols, :].astype(F32), 0, 1)
        outs = []
        for j in range(nt):
            outs.append(jnp.dot(m1_ref[...], vrt[j].astype(BF16), preferred_element_type=F32)
                        + jnp.dot(m2_ref[...], vit[j].astype(BF16), preferred_element_type=F32))
        at = jnp.swapaxes(jnp.stack(outs, axis=0), 0, 1)
        a_s[:, :, cols, :] = at.reshape(2, r, nt, LANES)
    k1t = min(nt, r)
    for c in range(r // k1t):
        outs = []
        for j in range(c * k1t, (c + 1) * k1t):
            a = jnp.concatenate([a_s[0, j], a_s[1, j]], axis=0).astype(BF16)
            outs.append(jnp.dot(g_ref[j], a, preferred_element_type=F32))
        o_ref[0, :, c * k1t:(c + 1) * k1t, :] = jnp.swapaxes(jnp.stack(outs, axis=0), 0, 1)


def _dft_factored_tables(n):
    r = n // GRID_W
    i = np.arange(r)
    ang = ((i[:, None] * i[None, :]) % r) * (2.0 * np.pi / r)
    c, s = np.cos(ang) * r ** -0.5, np.sin(ang) * r ** -0.5
    m1 = np.concatenate([c, -s], axis=0)
    m2 = np.concatenate([s, c], axis=0)
    k = i[:, None, None] + r * np.arange(GRID_W)[None, :, None]
    th = ((k * np.arange(GRID_W)[None, None, :]) % n) * (2.0 * np.pi / n)
    g = np.concatenate([np.cos(th), np.sin(th)], axis=2) * GRID_W ** -0.5
    return jnp.asarray(m1, BF16), jnp.asarray(m2, BF16), jnp.asarray(g, BF16)


def _dft_positions_factored(tabs, p, q):
    m1, m2, g = tabs
    nb, n, bw = p.shape
    r = n // GRID_W
    tok = pl.BlockSpec((1, r, GRID_W, LANES), lambda b, l: (b, 0, 0, l))
    y = pl.pallas_call(
        functools.partial(_dft_pos_kernel, nt=32),
        grid=(nb, bw // LANES),
        in_specs=[pl.BlockSpec(m1.shape, lambda b, l: (0, 0)),
                  pl.BlockSpec(m2.shape, lambda b, l: (0, 0)),
                  pl.BlockSpec(g.shape, lambda b, l: (0, 0, 0)),
                  tok, tok],
        out_specs=pl.BlockSpec((1, GRID_W, r, LANES), lambda b, l: (b, 0, 0, l)),
        out_shape=jax.ShapeDtypeStruct((nb, GRID_W, r, bw), F32),
        scratch_shapes=[pltpu.VMEM((2, r, GRID_W, LANES), F32)],
        compiler_params=_params(2), name="dft_positions_factored",
    )(m1, m2, g, p.reshape(nb, r, GRID_W, bw), q.reshape(nb, r, GRID_W, bw))
    return y.reshape(nb * n, bw)


def _dft_positions(cn, sn, p, q, tmo):
    nb, n, bw = p.shape
    return pl.pallas_call(
        _dft_kernel,
        grid=(n // tmo, nb),
        in_specs=[pl.BlockSpec((tmo, n), lambda i, b: (i, 0)),
                  pl.BlockSpec((tmo, n), lambda i, b: (i, 0)),
                  pl.BlockSpec((1, n, bw), lambda i, b: (b, 0, 0)),
                  pl.BlockSpec((1, n, bw), lambda i, b: (b, 0, 0))],
        out_specs=pl.BlockSpec((1, tmo, bw), lambda i, b: (b, i, 0)),
        out_shape=jax.ShapeDtypeStruct((nb, n, bw), F32),
        compiler_params=_params(2), name="dft_positions",
    )(cn, sn, p, q)


def _channel_dft_matrix():
    j = np.arange(D_GROUP_W)
    ang = 2.0 * np.pi * ((j[:, None] * j[None, :]) % D_GROUP_W) / D_GROUP_W
    eye = np.eye(D_GROUPS)
    cc = np.kron(eye, np.cos(ang) * D_GROUP_W ** -0.5)
    sc = np.kron(eye, np.sin(ang) * D_GROUP_W ** -0.5)
    return jnp.asarray(np.concatenate([cc, -sc], axis=1), BF16)


def _post_kernel(*refs, even, tm, seq):
    if even:
        (x_ref, a_ref, bb_ref, cb_ref, ub_ref, gb_ref, cbp_ref, ubp_ref, cbn_ref, ubn_ref,
         cw_ref, mod_ref, g_ref, w_ref, o_ref) = refs
    else:
        (x_ref, a_ref, y_ref, gd_ref, mod_ref, g_ref, w_ref, o_ref) = refs
    d = x_ref.shape[-1]
    bw = a_ref.shape[-1]
    if even:
        i = pl.program_id(0)
        t0 = (i * tm) % seq
        z = cb_ref[...].astype(F32) * ub_ref[...].astype(F32)
        zp = cbp_ref[7:8, :].astype(F32) * ubp_ref[7:8, :].astype(F32) * jnp.where(t0 == 0, 0.0, 1.0)
        zn = cbn_ref[0:1, :].astype(F32) * ubn_ref[0:1, :].astype(F32) * jnp.where(t0 + tm == seq, 0.0, 1.0)
        rid = lax.broadcasted_iota(jnp.int32, z.shape, 0)
        z_prev = jnp.where(rid == 0, zp, pltpu.roll(z, 1, 0))
        z_next = jnp.where(rid == tm - 1, zn, pltpu.roll(z, tm - 1, 0))
        cw = cw_ref[...]
        y = cw[0:1] * z_prev + cw[1:2] * z + cw[2:3] * z_next
        second = bb_ref[...].astype(F32) * y * _silu(gb_ref[...].astype(F32))
    else:
        second = y_ref[...] * _silu(gd_ref[...].astype(F32))
    yo = (jnp.dot(a_ref[...].astype(BF16), w_ref[0:bw, :], preferred_element_type=F32)
          + jnp.dot(second.astype(BF16), w_ref[bw:2 * bw, :], preferred_element_type=F32))
    nrm = yo * lax.rsqrt(jnp.mean(yo * yo, axis=-1, keepdims=True) + EPS) * g_ref[...]
    o_ref[...] = x_ref[...] + mod_ref[0][:, 2 * d:3 * d] * nrm


def _post(x2, a, second, rest, conv_w, mod3, mod_row_fn, g_post, w_out_bf, seq, tm, even):
    n, d = x2.shape
    bw = a.shape[1]
    tps = seq // tm
    const = lambda i: (0, 0)
    row = lambda i: (i, 0)
    in_specs = [pl.BlockSpec((tm, d), row), pl.BlockSpec((tm, bw), row)]
    args = [x2, a]
    if even:
        nblk8 = n // 8
        for c in (1, 2, 3, 4):
            in_specs.append(pl.BlockSpec((tm, bw), lambda i, c=c: (i, c)))
            args.append(rest)
        prev = lambda i, c: (jnp.maximum(i * (tm // 8) - 1, 0), c)
        nxt = lambda i, c: (jnp.minimum((i + 1) * (tm // 8), nblk8 - 1), c)
        for fn in (prev, nxt):
            for c in (2, 3):
                in_specs.append(pl.BlockSpec((8, bw), functools.partial(fn, c=c)))
                args.append(rest)
        in_specs.append(pl.BlockSpec(conv_w.shape, const))
        args.append(conv_w)
    else:
        in_specs += [pl.BlockSpec((tm, bw), row), pl.BlockSpec((tm, bw), lambda i: (i, 1))]
        args += [second, rest]
    in_specs += [pl.BlockSpec((1, 1, mod3.shape[2]), lambda i: (mod_row_fn(i // tps), 0, 0)),
                 pl.BlockSpec((1, d), const),
                 _resident(w_out_bf.shape)]
    args += [mod3, g_post.reshape(1, d), w_out_bf]
    return pl.pallas_call(
        functools.partial(_post_kernel, even=even, tm=tm, seq=seq),
        grid=(n // tm,), in_specs=in_specs,
        out_specs=pl.BlockSpec((tm, d), row),
        out_shape=jax.ShapeDtypeStruct((n, d), F32),
        compiler_params=_params(1), name="post_even" if even else "post_odd",
    )(*args)


def _rope_tables(seq):
    t = np.arange(seq)
    row = (t // GRID_W).astype(np.float32)
    col = (t % GRID_W).astype(np.float32)
    nf = A_QK // 4
    inv_freq = (ROPE_BASE ** (-np.arange(nf, dtype=np.float32) / nf)).astype(np.float32)
    p = np.arange(A_QK)
    f_idx = p % nf
    use_col = (p % A_QK) >= A_QK // 2
    sign = np.where((p % (2 * nf)) < nf, -1.0, 1.0)
    pos = np.where(use_col[None, :], col[:, None], row[:, None])
    ang = (pos * inv_freq[f_idx][None, :]).astype(np.float32).astype(np.float64)
    cos = np.cos(ang).astype(np.float32)
    sin = (np.sin(ang) * sign[None, :]).astype(np.float32)
    reps = LANES // A_QK
    return (jnp.asarray(np.tile(cos, (1, reps))), jnp.asarray(np.tile(sin, (1, reps))),
            jnp.asarray(np.ascontiguousarray(cos.T)), jnp.asarray(np.ascontiguousarray(sin.T)))


def _lambda_init(layer):
    return 0.8 - 0.6 * math.exp(-0.3 * layer)


def kernel(x, c, ctx, c_ctx, w_mod, b_mod, norm_pre, norm_post, w_in_even, lam_a, subln_a, conv_b,
           w_in_odd, rpb_c, w_out):
    nb, seq, d = x.shape
    lctx = ctx.shape[1]
    depth = w_mod.shape[0]
    assert seq % (NA_QROWS * GRID_W * NA_GROUPS_PER_STEP) == 0 and seq // GRID_W >= NA_KROWS
    assert lctx % LANES == 0 and seq % KEY_CHUNK == 0 and nb < 16

    pad = jnp.zeros((16 - nb - 1, d), F32)
    cc = jnp.concatenate([c, c_ctx[None, :], pad], axis=0)
    mods = _mod_all(cc, w_mod, b_mod)

    rope_tabs = _rope_tables(seq)
    cs_bf = _channel_dft_matrix()
    dft_tabs = _dft_factored_tables(seq)
    cn_c, sn_c = _dft_tables(lctx)

    tm_x = 1024
    tm_c = lctx
    x_row = lambda b: b
    c_row = lambda b: nb

    x2 = x.reshape(nb * seq, d)
    c2 = ctx.reshape(nb * lctx, d)
    for l in range(depth):
        need_ctx = l < depth - 1
        j = l // 2
        mod3 = mods[l].reshape(16, 1, 3 * d)
        w_out_bf = w_out[l].astype(BF16)
        if l % 2 == 0:
            w_bf = w_in_even[j].astype(BF16)
            li = _lambda_init(l)
            qt, k, vt, rest = _inproj_even(x2, mod3, x_row, norm_pre[l], w_bf, seq, rope_tabs, tm_x)
            qct, kc, vct, rest_c = _inproj_even(c2, mod3, c_row, norm_pre[l], w_bf, lctx, None, tm_c)
            a = _diff_attn(lam_a[j], subln_a[j], qt, kc, vct, k, vt, rest, li, 512)
            x2n = _post(x2, a, None, rest, conv_b[j], mod3, x_row, norm_post[l], w_out_bf, seq, tm_x, True)
            if need_ctx:
                ac = _diff_attn(lam_a[j], subln_a[j], qct, kc, vct, None, None, rest_c, li, lctx)
                c2 = _post(c2, ac, None, rest_c, conv_b[j], mod3, c_row, norm_post[l], w_out_bf,
                           lctx, tm_c, True)
            x2 = x2n
        else:
            w_bf = w_in_odd[j].astype(BF16)
            q, k, v, rest, p, pq = _inproj_odd(x2, mod3, x_row, norm_pre[l], w_bf, cs_bf, seq, tm_x)
            qc, kc, vc, rest_c, p_c, pq_c = _inproj_odd(c2, mod3, c_row, norm_pre[l], w_bf, cs_bf, lctx, tm_c)
            bias = _na_bias(rpb_c[j], seq // GRID_W)
            a = _na_attn(q, k, v, kc, vc, bias, rest)
            y = _dft_positions_factored(dft_tabs, p, pq)
            x2n = _post(x2, a, y, rest, None, mod3, x_row, norm_post[l], w_out_bf, seq, tm_x, False)
            if need_ctx:
                ac = _dense_attn(qc, kc, vc, rest_c)
                yc = _dft_positions(cn_c, sn_c, p_c, pq_c, lctx).reshape(nb * lctx, -1)
                c2 = _post(c2, ac, yc, rest_c, None, mod3, c_row, norm_post[l], w_out_bf, lctx, tm_c, False)
            x2 = x2n
    return x2.reshape(nb, seq, d)
```

```python
import functools
import math

import numpy as np
import jax
import jax.numpy as jnp
from jax import lax
from jax.experimental import pallas as pl
from jax.experimental.pallas import tpu as pltpu

GRID_W = 64
A_QK = 32
A_V = 64
C_HEAD_DIM = 64
NA_KH = 8
NA_KW = 16
D_GROUPS = 8
D_GROUP_W = 64
ROPE_BASE = 10000.0
EPS = 1e-6
LOG2E = 1.4426950408889634
NEG_BIG = -1e30

LANES = 128
HEAD_PAIR_W = 128
KEY_CHUNK = 256
DIFF_SUB_Q = 512
SCORE_AHEAD = 4
VMEM_LIMIT = 48 * 1024 * 1024

F32 = jnp.float32
BF16 = jnp.bfloat16
MIX_DTYPE = BF16


def _silu(v):
    return v * jax.nn.sigmoid(v)


def _params(n_axes):
    return pltpu.CompilerParams(dimension_semantics=("arbitrary",) * n_axes,
                                vmem_limit_bytes=VMEM_LIMIT)


def _bf16_table(values):
    return jnp.asarray(np.asarray(values, np.float32)).astype(BF16)


def _resident(shape):
    return pl.BlockSpec(shape, lambda *_: (0,) * len(shape), pipeline_mode=pl.Buffered(1))


def _nt_dot(a, b):
    return lax.dot_general(a, b, (((1,), (1,)), ((), ())), preferred_element_type=F32)


def _mod_kernel(cc_ref, w_ref, b_ref, o_ref):
    s = _silu(cc_ref[...]).astype(BF16)
    o_ref[0] = jnp.dot(s, w_ref[0].astype(BF16), preferred_element_type=F32) + b_ref[0]


def _mod_all(cc, w_mod, b_mod):
    depth, d, d3 = w_mod.shape
    tn = d
    return pl.pallas_call(
        _mod_kernel,
        grid=(depth, d3 // tn),
        in_specs=[pl.BlockSpec((cc.shape[0], d), lambda l, n: (0, 0)),
                  pl.BlockSpec((1, d, tn), lambda l, n: (l, 0, n)),
                  pl.BlockSpec((1, 1, tn), lambda l, n: (l, 0, n))],
        out_specs=pl.BlockSpec((1, cc.shape[0], tn), lambda l, n: (l, 0, n)),
        out_shape=jax.ShapeDtypeStruct((depth, cc.shape[0], d3), F32),
        compiler_params=_params(2), name="adaln_mod",
    )(cc, w_mod, b_mod.reshape(depth, 1, d3))


def _prenorm(x_ref, mod_ref, g_ref):
    d = x_ref.shape[-1]
    x = x_ref[...]
    y = x * lax.rsqrt(jnp.mean(x * x, axis=-1, keepdims=True) + EPS) * g_ref[...]
    m = mod_ref[0]
    h = y * (1.0 + m[:, d:2 * d]) + m[:, 0:d]
    return h.astype(BF16)


def _inproj_even_kernel(*refs, rope):
    if rope:
        (x_ref, mod_ref, g_ref, w_ref, wqt_ref, wvt_ref,
         cos_ref, sin_ref, cost_ref, sint_ref, qt_ref, k_ref, vt_ref, rest_ref) = refs
    else:
        (x_ref, mod_ref, g_ref, w_ref, wqt_ref, wvt_ref, qt_ref, k_ref, vt_ref, rest_ref) = refs
    bw = qt_ref.shape[1]
    hb = _prenorm(x_ref, mod_ref, g_ref)
    pair = A_QK // 4

    def rotary(u, axis, cos, sin):
        idx = lax.broadcasted_iota(jnp.int32, u.shape, axis)
        partner = jnp.where((idx & pair) == 0, pltpu.roll(u, bw - pair, axis), pltpu.roll(u, pair, axis))
        return u * cos + partner * sin

    qt = _nt_dot(wqt_ref[...], hb)
    if rope:
        reps = bw // cost_ref.shape[0]
        qt = rotary(qt, 0, jnp.tile(cost_ref[...], (reps, 1)), jnp.tile(sint_ref[...], (reps, 1)))
    qt_ref[0] = (qt * (A_QK ** -0.5 * LOG2E)).astype(BF16)

    k = jnp.dot(hb, w_ref[:, bw:2 * bw], preferred_element_type=F32)
    if rope:
        reps = bw // cos_ref.shape[1]
        k = rotary(k, 1, jnp.tile(cos_ref[...], (1, reps)), jnp.tile(sin_ref[...], (1, reps)))
    k_ref[0] = k.astype(BF16)

    vt_ref[0] = _nt_dot(wvt_ref[...], hb).astype(BF16)

    n_rest = rest_ref.shape[1] // bw
    for c in range(n_rest):
        col = (3 + c) * bw
        rest_ref[:, c * bw:(c + 1) * bw] = jnp.dot(
            hb, w_ref[:, col:col + bw], preferred_element_type=F32).astype(BF16)


def _inproj_even(x2, mod3, mod_row_fn, g_pre, w_bf, seq, rope_tabs, tm):
    n, d = x2.shape
    bw = w_bf.shape[1] // 8
    nb = n // seq
    tps = seq // tm
    rope = rope_tabs is not None
    wqt = w_bf[:, 0:bw].T
    wvt = w_bf[:, 2 * bw:3 * bw].T
    const = lambda i: (0, 0)
    in_specs = [pl.BlockSpec((tm, d), lambda i: (i, 0)),
                pl.BlockSpec((1, 1, mod3.shape[2]), lambda i: (mod_row_fn(i // tps), 0, 0)),
                pl.BlockSpec((1, d), const),
                _resident(w_bf.shape), _resident(wqt.shape), _resident(wvt.shape)]
    args = [x2, mod3, g_pre.reshape(1, d), w_bf, wqt, wvt]
    if rope:
        cos_n, sin_n, cos_t, sin_t = rope_tabs
        in_specs += [pl.BlockSpec((tm, cos_n.shape[1]), lambda i: (i % tps, 0)),
                     pl.BlockSpec((tm, cos_n.shape[1]), lambda i: (i % tps, 0)),
                     pl.BlockSpec((cos_t.shape[0], tm), lambda i: (0, i % tps)),
                     pl.BlockSpec((cos_t.shape[0], tm), lambda i: (0, i % tps))]
        args += [cos_n, sin_n, cos_t, sin_t]
    out_specs = [pl.BlockSpec((1, bw, tm), lambda i: (i // tps, 0, i % tps)),
                 pl.BlockSpec((1, tm, bw), lambda i: (i // tps, i % tps, 0)),
                 pl.BlockSpec((1, bw, tm), lambda i: (i // tps, 0, i % tps)),
                 pl.BlockSpec((tm, 5 * bw), lambda i: (i, 0))]
    out_shape = [jax.ShapeDtypeStruct((nb, bw, seq), BF16),
                 jax.ShapeDtypeStruct((nb, seq, bw), BF16),
                 jax.ShapeDtypeStruct((nb, bw, seq), BF16),
                 jax.ShapeDtypeStruct((n, 5 * bw), BF16)]
    return pl.pallas_call(
        functools.partial(_inproj_even_kernel, rope=rope),
        grid=(n // tm,), in_specs=in_specs, out_specs=out_specs, out_shape=out_shape,
        compiler_params=_params(1), name="inproj_even",
    )(*args)


def _inproj_odd_kernel(x_ref, mod_ref, g_ref, w_ref, wqt_ref, wvt_ref, cs_ref,
                       qt_ref, k_ref, vt_ref, rest_ref, p_ref, pq_ref):
    bw = k_ref.shape[2]
    hb = _prenorm(x_ref, mod_ref, g_ref)

    def proj(c):
        return jnp.dot(hb, w_ref[:, c * bw:(c + 1) * bw], preferred_element_type=F32)

    qt_ref[0] = (_nt_dot(wqt_ref[...], hb) * (C_HEAD_DIM ** -0.5 * LOG2E)).astype(BF16)
    k_ref[0] = proj(1).astype(BF16)
    vt = _nt_dot(wvt_ref[...], hb).astype(BF16)
    for c in range(vt_ref.shape[1]):
        vt_ref[0, c] = vt[:, c * NA_VT_CHUNK:(c + 1) * NA_VT_CHUNK]
    rest_ref[:, 0:bw] = proj(3).astype(BF16)
    rest_ref[:, bw:2 * bw] = proj(5).astype(BF16)
    pq = jnp.dot(proj(4).astype(BF16), cs_ref[...], preferred_element_type=F32)
    p_ref[0] = pq[:, 0:bw].astype(BF16)
    pq_ref[0] = pq[:, bw:2 * bw].astype(BF16)


def _inproj_odd(x2, mod3, mod_row_fn, g_pre, w_bf, cs_bf, seq, tm):
    n, d = x2.shape
    bw = w_bf.shape[1] // 6
    nb = n // seq
    tps = seq // tm
    const = lambda i: (0, 0)
    wqt = w_bf[:, 0:bw].T
    wvt = w_bf[:, 2 * bw:3 * bw].T
    cpt = tm // NA_VT_CHUNK
    tok = pl.BlockSpec((1, tm, bw), lambda i: (i // tps, i % tps, 0))
    tok_shape = jax.ShapeDtypeStruct((nb, seq, bw), BF16)
    return pl.pallas_call(
        _inproj_odd_kernel,
        grid=(n // tm,),
        in_specs=[pl.BlockSpec((tm, d), lambda i: (i, 0)),
                  pl.BlockSpec((1, 1, mod3.shape[2]), lambda i: (mod_row_fn(i // tps), 0, 0)),
                  pl.BlockSpec((1, d), const),
                  _resident(w_bf.shape), _resident(wqt.shape), _resident(wvt.shape),
                  _resident(cs_bf.shape)],
        out_specs=[pl.BlockSpec((1, bw, tm), lambda i: (i // tps, 0, i % tps)),
                   tok,
                   pl.BlockSpec((1, cpt, bw, NA_VT_CHUNK), lambda i: (i // tps, i % tps, 0, 0)),
                   pl.BlockSpec((tm, 2 * bw), lambda i: (i, 0)), tok, tok],
        out_shape=[jax.ShapeDtypeStruct((nb, bw, seq), BF16),
                   tok_shape,
                   jax.ShapeDtypeStruct((nb, seq // NA_VT_CHUNK, bw, NA_VT_CHUNK), BF16),
                   jax.ShapeDtypeStruct((n, 2 * bw), BF16), tok_shape, tok_shape],
        compiler_params=_params(1), name="inproj_odd",
    )(x2, mod3, g_pre.reshape(1, d), w_bf, wqt, wvt, cs_bf)


def _diff_attn_kernel(*refs, has_main, tq, lam_init):
    if has_main:
        (lam_ref, sub_ref, qt_ref, kc_ref, vct_ref, ka_ref, vat_ref, ga_ref, o_ref) = refs
        key_refs = [(kc_ref, vct_ref), (ka_ref, vat_ref)]
    else:
        (lam_ref, sub_ref, qt_ref, kc_ref, vct_ref, ga_ref, o_ref) = refs
        key_refs = [(kc_ref, vct_ref)]
    chunks = []
    for k_ref, vt_ref in key_refs:
        for st in range(0, k_ref.shape[1], KEY_CHUNK):
            chunks.append((k_ref, vt_ref, st, min(KEY_CHUNK, k_ref.shape[1] - st)))
    lp = lam_ref[...]
    lam = (jnp.exp(jnp.sum(lp[0:1] * lp[1:2], axis=1, keepdims=True))
           - jnp.exp(jnp.sum(lp[2:3] * lp[3:4], axis=1, keepdims=True)) + lam_init)
    sq = min(tq, DIFF_SUB_Q)
    qf = qt_ref[0].astype(F32)
    rowid = lax.broadcasted_iota(jnp.int32, (qf.shape[0], sq), 0)
    sub = jnp.tile(sub_ref[...], (1, sq // LANES))

    def fold(v):
        return v.reshape(v.shape[0] // 8, 8, sq)

    items = [(j, hh, m) for j in range(tq // sq) for hh in range(2) for m in range(2)]
    work = [(t0 + m, ci) for t0 in range(0, len(items), 2) for ci in range(len(chunks)) for m in range(2)]
    qms, state, maps, pending = {}, {}, {}, []

    def issue(t, ci):
        j, hh, m = items[t]
        if ci == 0:
            lo = hh * A_V + m * A_QK
            qms[t] = jnp.where((rowid >= lo) & (rowid < lo + A_QK), qf[:, j * sq:(j + 1) * sq],
                               0.0).astype(BF16)
        k_ref, _, st, sz = chunks[ci]
        sc = jnp.dot(k_ref[0, st:st + sz, :], qms[t], preferred_element_type=F32)
        pending.append((t, ci, sc, jnp.max(jnp.max(fold(sc), axis=0), axis=0, keepdims=True)))

    def consume():
        t, ci, s, cm = pending.pop(0)
        hh = items[t][1]
        _, vt_ref, st, sz = chunks[ci]
        if ci == 0:
            state[t] = (jnp.full((1, sq), -jnp.inf, F32), jnp.zeros((8, sq), F32),
                        jnp.zeros((A_V, sq), F32))
        m_run, l, acc = state[t]
        m_new = jnp.maximum(m_run, cm)
        alpha = jnp.exp2(m_run - m_new)
        e = jnp.exp2(s - m_new)
        l = alpha * l + jnp.sum(fold(e), axis=0)
        acc = alpha * acc + jnp.dot(vt_ref[0, hh * A_V:(hh + 1) * A_V, st:st + sz], e.astype(BF16),
                                    preferred_element_type=F32)
        state[t] = (m_new, l, acc)
        if ci == len(chunks) - 1:
            del state[t], qms[t]
            maps[t] = acc * (1.0 / jnp.sum(l, axis=0, keepdims=True))

    for idx in range(len(work) + SCORE_AHEAD):
        if idx < len(work):
            issue(*work[idx])
        if idx >= SCORE_AHEAD:
            consume()
    for j in range(tq // sq):
        heads = []
        for hh in range(2):
            ot = maps[4 * j + 2 * hh] - lam * maps[4 * j + 2 * hh + 1]
            ot = ot * lax.rsqrt(jnp.mean(ot * ot, axis=0, keepdims=True) + EPS) * sub * (1.0 - lam_init)
            heads.append(ot)
        o = jnp.concatenate(heads, axis=0).T
        rows = slice(j * sq, (j + 1) * sq)
        o_ref[rows, :] = (o * _silu(ga_ref[rows, :].astype(F32))).astype(o_ref.dtype)


def _diff_attn(lam_p, subln, qt, kc, vct, ka, vat, rest, lam_init, tq):
    nb, bw, sq = qt.shape
    nq = sq // tq
    hp = HEAD_PAIR_W
    sub_b = jnp.broadcast_to(subln.reshape(A_V, 1), (A_V, LANES))
    in_specs = [pl.BlockSpec(lam_p.shape, lambda b, h, i: (0, 0)),
                pl.BlockSpec(sub_b.shape, lambda b, h, i: (0, 0)),
                pl.BlockSpec((1, hp, tq), lambda b, h, i: (b, h, i)),
                pl.BlockSpec((1, kc.shape[1], hp), lambda b, h, i: (b, 0, h)),
                pl.BlockSpec((1, hp, kc.shape[1]), lambda b, h, i: (b, h, 0))]
    args = [lam_p, sub_b, qt, kc, vct]
    if ka is not None:
        in_specs += [pl.BlockSpec((1, ka.shape[1], hp), lambda b, h, i: (b, 0, h)),
                     pl.BlockSpec((1, hp, ka.shape[1]), lambda b, h, i: (b, h, 0))]
        args += [ka, vat]
    in_specs.append(pl.BlockSpec((tq, hp), lambda b, h, i: (b * nq + i, h)))
    args.append(rest)
    return pl.pallas_call(
        functools.partial(_diff_attn_kernel, has_main=ka is not None, tq=tq, lam_init=lam_init),
        grid=(nb, bw // hp, nq),
        in_specs=in_specs,
        out_specs=pl.BlockSpec((tq, hp), lambda b, h, i: (b * nq + i, h)),
        out_shape=jax.ShapeDtypeStruct((nb * sq, bw), MIX_DTYPE),
        compiler_params=_params(3), name="diff_attn",
    )(*args)


NA_QROWS = 4
NA_KROWS = 12
NA_VT_CHUNK = 256
NA_GROUPS_PER_STEP = 4


def _softmax_pv(problems):
    tq = problems[0][0].shape[1]
    rowid = lax.broadcasted_iota(jnp.int32, problems[0][0].shape, 0)
    chains = [(p, hh) for p in range(len(problems)) for hh in range(2)]
    scores, exps, outs = {}, {}, {}

    def fold(v):
        return v.reshape(v.shape[0] // 8, 8, tq)

    def score(p, hh):
        qt, pieces, _ = problems[p]
        qm = jnp.where((rowid >= hh * C_HEAD_DIM) & (rowid < (hh + 1) * C_HEAD_DIM), qt.astype(F32),
                       0.0).astype(BF16)
        scores[p, hh] = [jnp.dot(k, qm, preferred_element_type=F32) + (0.0 if bias is None else bias[hh])
                         for k, _, bias in pieces]

    def weigh(c):
        ss = scores.pop(c)
        m = jnp.max(functools.reduce(jnp.maximum, [jnp.max(fold(s), axis=0) for s in ss]),
                    axis=0, keepdims=True)
        exps[c] = [jnp.exp2(s - m) for s in ss]

    def accumulate(p, hh):
        es = exps.pop((p, hh))
        l = jnp.sum(functools.reduce(jnp.add, [jnp.sum(fold(e), axis=0) for e in es]), axis=0, keepdims=True)
        pv = jnp.zeros((C_HEAD_DIM, tq), F32)
        for e, (_, vts, _) in zip(es, problems[p][1]):
            eb = e.astype(BF16)
            off = 0
            for vt in vts:
                pv = pv + jnp.dot(vt[hh * C_HEAD_DIM:(hh + 1) * C_HEAD_DIM, :], eb[off:off + vt.shape[1], :],
                                  preferred_element_type=F32)
                off += vt.shape[1]
        outs[p, hh] = pv * (1.0 / l)

    for idx in range(len(chains) + 3):
        if idx < len(chains):
            score(*chains[idx])
        if 2 <= idx < len(chains) + 2:
            weigh(chains[idx - 2])
        if idx >= 3:
            accumulate(*chains[idx - 3])
    return [jnp.concatenate([outs[p, 0], outs[p, 1]], axis=0).T * _silu(problems[p][2].astype(F32))
            for p in range(len(problems))]


class _BiasOf:
    def __init__(self, ref):
        self.ref = ref

    def __getitem__(self, hh):
        return self.ref[0, hh]


def _na_kernel(qt_ref, k_ref, vt_ref, kc_ref, vct_ref, *rest, rows, n_sub):
    bias_refs, (gate_ref, o_ref) = rest[:n_sub], rest[n_sub:]
    tq = NA_QROWS * GRID_W
    nk = NA_KROWS * GRID_W
    ctx_piece = (kc_ref[0], [vct_ref[0, c] for c in range(vct_ref.shape[1])], None)
    problems = []
    for u in range(n_sub):
        g = pl.program_id(2) * n_sub + u
        base = jnp.clip(NA_QROWS * g - NA_KH // 2, 0, rows - NA_KROWS)
        start = pl.multiple_of(base * GRID_W, NA_VT_CHUNK)
        c0 = base // (NA_VT_CHUNK // GRID_W)
        kw = k_ref[0, pl.ds(start, nk), :]
        vts = [vt_ref[0, c0 + c] for c in range(nk // NA_VT_CHUNK)]
        qrows = slice(u * tq, (u + 1) * tq)
        problems.append((qt_ref[0, :, qrows], [(kw, vts, _BiasOf(bias_refs[u])), ctx_piece],
                         gate_ref[qrows, :]))
    for u, o in enumerate(_softmax_pv(problems)):
        o_ref[u * tq:(u + 1) * tq, :] = o.astype(o_ref.dtype)


def _na_case_tables(rows):
    rs = np.clip(np.arange(rows) - NA_KH // 2, 0, rows - NA_KH)
    n_groups = rows // NA_QROWS
    pats = []
    for g in range(n_groups):
        base = int(np.clip(NA_QROWS * g - NA_KH // 2, 0, rows - NA_KROWS))
        pat = np.full((NA_QROWS, NA_KROWS), 2 * NA_KH - 1, np.int32)
        for j in range(NA_QROWS):
            r = NA_QROWS * g + j
            for i in range(NA_KROWS):
                kr = base + i
                if rs[r] <= kr < rs[r] + NA_KH:
                    pat[j, i] = kr - r + NA_KH - 1
            assert (pat[j] != 2 * NA_KH - 1).sum() == NA_KH
        pats.append(pat)
    cases = np.stack([pats[0], pats[1], pats[-1]])
    for g in range(n_groups):
        want = 0 if g == 0 else (2 if g == n_groups - 1 else 1)
        assert (pats[g] == cases[want]).all()
    return cases


def _na_bias(rpb, rows):
    h = rpb.shape[0]
    cq = np.arange(GRID_W)
    cs = np.clip(cq - NA_KW // 2, 0, GRID_W - NA_KW)
    kcol = np.arange(GRID_W)
    col_ok = ((kcol[None, :] >= cs[:, None]) & (kcol[None, :] < cs[:, None] + NA_KW)).T
    col_off = np.clip(kcol[None, :] - cq[:, None] + NA_KW - 1, 0, 2 * NA_KW - 2).T
    pick = ((col_off[None] == np.arange(2 * NA_KW - 1)[:, None, None]) & col_ok[None]).astype(np.float32)
    toep = jnp.dot((rpb * LOG2E).reshape(-1, 2 * NA_KW - 1), pick.reshape(2 * NA_KW - 1, -1),
                   precision=lax.Precision.HIGHEST).reshape(h, -1, GRID_W, GRID_W)
    toep = toep + np.where(col_ok, 0.0, NEG_BIG).astype(np.float32)
    toep = jnp.concatenate([toep, jnp.full((h, 1, GRID_W, GRID_W), NEG_BIG, F32)], axis=1)
    cases = _na_case_tables(rows)
    n_case, qr, kr = cases.shape
    return pl.pallas_call(
        functools.partial(_na_bias_kernel, cases=cases),
        grid=(h,),
        in_specs=[pl.BlockSpec((1,) + toep.shape[1:], lambda i: (i, 0, 0, 0))],
        out_specs=pl.BlockSpec((n_case, 1, kr * GRID_W, qr * GRID_W), lambda i: (0, i, 0, 0)),
        out_shape=jax.ShapeDtypeStruct((n_case, h, kr * GRID_W, qr * GRID_W), F32),
        compiler_params=_params(1), name="nbr_bias",
    )(toep)


def _na_bias_kernel(toep_ref, o_ref, *, cases):
    for c in range(cases.shape[0]):
        for i in range(cases.shape[2]):
            o_ref[c, 0, i * GRID_W:(i + 1) * GRID_W, :] = jnp.concatenate(
                [toep_ref[0, int(b)] for b in cases[c, :, i]], axis=1)


def _na_attn(qt, k, vt, kc, vct, bias, rest):
    nb, bw, s = qt.shape
    hp = HEAD_PAIR_W
    tq = NA_QROWS * GRID_W
    ng = s // tq
    n_sub = NA_GROUPS_PER_STEP
    nsteps = ng // n_sub
    l = kc.shape[1]

    def bias_spec(u):
        def index(b, h, i):
            g = i * n_sub + u
            return (jnp.where(g == 0, 0, jnp.where(g == ng - 1, 2, 1)), h, 0, 0)
        return pl.BlockSpec((1, 2, NA_KROWS * GRID_W, tq), index)

    return pl.pallas_call(
        functools.partial(_na_kernel, rows=s // GRID_W, n_sub=n_sub),
        grid=(nb, bw // hp, nsteps),
        in_specs=[pl.BlockSpec((1, hp, n_sub * tq), lambda b, h, i: (b, h, i)),
                  pl.BlockSpec((1, s, hp), lambda b, h, i: (b, 0, h)),
                  pl.BlockSpec((1, vt.shape[1], hp, NA_VT_CHUNK), lambda b, h, i: (b, 0, h, 0)),
                  pl.BlockSpec((1, l, hp), lambda b, h, i: (b, 0, h)),
                  pl.BlockSpec((1, vct.shape[1], hp, NA_VT_CHUNK), lambda b, h, i: (b, 0, h, 0))]
                 + [bias_spec(u) for u in range(n_sub)]
                 + [pl.BlockSpec((n_sub * tq, hp), lambda b, h, i: (b * nsteps + i, h))],
        out_specs=pl.BlockSpec((n_sub * tq, hp), lambda b, h, i: (b * nsteps + i, h)),
        out_shape=jax.ShapeDtypeStruct((nb * s, bw), MIX_DTYPE),
        compiler_params=_params(3), name="nbr_attn",
    )(qt, k, vt, kc, vct, *([bias] * n_sub), rest)


def _dense_attn_kernel(qt_ref, k_ref, vt_ref, gate_ref, o_ref):
    piece = (k_ref[0], [vt_ref[0, c] for c in range(vt_ref.shape[1])], None)
    o_ref[...] = _softmax_pv([(qt_ref[0], [piece], gate_ref[...])])[0].astype(o_ref.dtype)


def _dense_attn(qt, k, vt, rest):
    nb, bw, l = qt.shape
    hp = HEAD_PAIR_W
    return pl.pallas_call(
        _dense_attn_kernel,
        grid=(nb, bw // hp),
        in_specs=[pl.BlockSpec((1, hp, l), lambda b, h: (b, h, 0)),
                  pl.BlockSpec((1, l, hp), lambda b, h: (b, 0, h)),
                  pl.BlockSpec((1, vt.shape[1], hp, NA_VT_CHUNK), lambda b, h: (b, 0, h, 0)),
                  pl.BlockSpec((l, hp), lambda b, h: (b, h))],
        out_specs=pl.BlockSpec((l, hp), lambda b, h: (b, h)),
        out_shape=jax.ShapeDtypeStruct((nb * l, bw), MIX_DTYPE),
        compiler_params=_params(2), name="ctx_dense_attn",
    )(qt, k, vt, rest)


def _dft_kernel(c_ref, s_ref, p_ref, q_ref, o_ref):
    o_ref[0] = (jnp.dot(c_ref[...], p_ref[0], preferred_element_type=F32)
                + jnp.dot(s_ref[...], q_ref[0], preferred_element_type=F32))


def _dft_tables(n):
    idx = np.arange(n)
    ang = ((idx[:, None] * idx[None, :]) % n) * (2.0 * np.pi / n)
    scale = n ** -0.5
    return _bf16_table(np.cos(ang) * scale), _bf16_table(np.sin(ang) * scale)


def _dft_pos_kernel(m1_ref, m2_ref, g_ref, vr_ref, vi_ref, o_ref, a_s, *, nt):
    r = vr_ref.shape[1]
    for c in range(GRID_W // nt):
        cols = slice(c * nt, (c + 1) * nt)
        vrt = jnp.swapaxes(vr_ref[0, :, cols, :].astype(F32), 0, 1)
        vit = jnp.swapaxes(vi_ref[0, :, cols, :].astype(F32), 0, 1)
        outs = []
        for j in range(nt):
            outs.append(jnp.dot(m1_ref[...], vrt[j].astype(BF16), preferred_element_type=F32)
                        + jnp.dot(m2_ref[...], vit[j].astype(BF16), preferred_element_type=F32))
        at = jnp.swapaxes(jnp.stack(outs, axis=0), 0, 1)
        a_s[:, :, cols, :] = at.reshape(2, r, nt, LANES)
    k1t = min(nt, r)
    for c in range(r // k1t):
        outs = []
        for j in range(c * k1t, (c + 1) * k1t):
            a = jnp.concatenate([a_s[0, j], a_s[1, j]], axis=0).astype(BF16)
            outs.append(jnp.dot(g_ref[j], a, preferred_element_type=F32))
        o_ref[0, :, c * k1t:(c + 1) * k1t, :] = jnp.swapaxes(jnp.stack(outs, axis=0), 0, 1)


def _dft_factored_tables(n):
    r = n // GRID_W
    i = np.arange(r)
    ang = ((i[:, None] * i[None, :]) % r) * (2.0 * np.pi / r)
    c, s = np.cos(ang) * r ** -0.5, np.sin(ang) * r ** -0.5
    m1 = np.concatenate([c, -s], axis=0)
    m2 = np.concatenate([s, c], axis=0)
    k = i[:, None, None] + r * np.arange(GRID_W)[None, :, None]
    th = ((k * np.arange(GRID_W)[None, None, :]) % n) * (2.0 * np.pi / n)
    g = np.concatenate([np.cos(th), np.sin(th)], axis=2) * GRID_W ** -0.5
    return _bf16_table(m1), _bf16_table(m2), _bf16_table(g)


def _dft_positions_factored(tabs, p, q):
    m1, m2, g = tabs
    nb, n, bw = p.shape
    r = n // GRID_W
    tok = pl.BlockSpec((1, r, GRID_W, LANES), lambda b, l: (b, 0, 0, l))
    y = pl.pallas_call(
        functools.partial(_dft_pos_kernel, nt=32),
        grid=(nb, bw // LANES),
        in_specs=[pl.BlockSpec(m1.shape, lambda b, l: (0, 0)),
                  pl.BlockSpec(m2.shape, lambda b, l: (0, 0)),
                  pl.BlockSpec(g.shape, lambda b, l: (0, 0, 0)),
                  tok, tok],
        out_specs=pl.BlockSpec((1, GRID_W, r, LANES), lambda b, l: (b, 0, 0, l)),
        out_shape=jax.ShapeDtypeStruct((nb, GRID_W, r, bw), F32),
        scratch_shapes=[pltpu.VMEM((2, r, GRID_W, LANES), F32)],
        compiler_params=_params(2), name="dft_positions_factored",
    )(m1, m2, g, p.reshape(nb, r, GRID_W, bw), q.reshape(nb, r, GRID_W, bw))
    return y.reshape(nb * n, bw)


def _dft_positions(cn, sn, p, q, tmo):
    nb, n, bw = p.shape
    return pl.pallas_call(
        _dft_kernel,
        grid=(n // tmo, nb),
        in_specs=[pl.BlockSpec((tmo, n), lambda i, b: (i, 0)),
                  pl.BlockSpec((tmo, n), lambda i, b: (i, 0)),
                  pl.BlockSpec((1, n, bw), lambda i, b: (b, 0, 0)),
                  pl.BlockSpec((1, n, bw), lambda i, b: (b, 0, 0))],
        out_specs=pl.BlockSpec((1, tmo, bw), lambda i, b: (b, i, 0)),
        out_shape=jax.ShapeDtypeStruct((nb, n, bw), F32),
        compiler_params=_params(2), name="dft_positions",
    )(cn, sn, p, q)


def _channel_dft_matrix():
    j = np.arange(D_GROUP_W)
    ang = 2.0 * np.pi * ((j[:, None] * j[None, :]) % D_GROUP_W) / D_GROUP_W
    eye = np.eye(D_GROUPS)
    cc = np.kron(eye, np.cos(ang) * D_GROUP_W ** -0.5)
    sc = np.kron(eye, np.sin(ang) * D_GROUP_W ** -0.5)
    return _bf16_table(np.concatenate([cc, -sc], axis=1))


def _post_kernel(*refs, even, tm, seq):
    if even:
        (x_ref, a_ref, bb_ref, cb_ref, ub_ref, gb_ref, cbp_ref, ubp_ref, cbn_ref, ubn_ref,
         cw_ref, mod_ref, g_ref, w_ref, o_ref) = refs
    else:
        (x_ref, a_ref, y_ref, gd_ref, mod_ref, g_ref, w_ref, o_ref) = refs
    d = x_ref.shape[-1]
    bw = a_ref.shape[-1]
    if even:
        i = pl.program_id(0)
        t0 = (i * tm) % seq
        z = cb_ref[...].astype(F32) * ub_ref[...].astype(F32)
        zp = cbp_ref[7:8, :].astype(F32) * ubp_ref[7:8, :].astype(F32) * jnp.where(t0 == 0, 0.0, 1.0)
        zn = cbn_ref[0:1, :].astype(F32) * ubn_ref[0:1, :].astype(F32) * jnp.where(t0 + tm == seq, 0.0, 1.0)
        rid = lax.broadcasted_iota(jnp.int32, z.shape, 0)
        z_prev = jnp.where(rid == 0, zp, pltpu.roll(z, 1, 0))
        z_next = jnp.where(rid == tm - 1, zn, pltpu.roll(z, tm - 1, 0))
        cw = cw_ref[...]
        y = cw[0:1] * z_prev + cw[1:2] * z + cw[2:3] * z_next
        second = bb_ref[...].astype(F32) * y * _silu(gb_ref[...].astype(F32))
    else:
        second = y_ref[...] * _silu(gd_ref[...].astype(F32))
    yo = (jnp.dot(a_ref[...].astype(BF16), w_ref[0:bw, :], preferred_element_type=F32)
          + jnp.dot(second.astype(BF16), w_ref[bw:2 * bw, :], preferred_element_type=F32))
    nrm = yo * lax.rsqrt(jnp.mean(yo * yo, axis=-1, keepdims=True) + EPS) * g_ref[...]
    o_ref[...] = x_ref[...] + mod_ref[0][:, 2 * d:3 * d] * nrm


def _post(x2, a, second, rest, conv_w, mod3, mod_row_fn, g_post, w_out_bf, seq, tm, even):
    n, d = x2.shape
    bw = a.shape[1]
    tps = seq // tm
    const = lambda i: (0, 0)
    row = lambda i: (i, 0)
    in_specs = [pl.BlockSpec((tm, d), row), pl.BlockSpec((tm, bw), row)]
    args = [x2, a]
    if even:
        nblk8 = n // 8
        for c in (1, 2, 3, 4):
            in_specs.append(pl.BlockSpec((tm, bw), lambda i, c=c: (i, c)))
            args.append(rest)
        prev = lambda i, c: (jnp.maximum(i * (tm // 8) - 1, 0), c)
        nxt = lambda i, c: (jnp.minimum((i + 1) * (tm // 8), nblk8 - 1), c)
        for fn in (prev, nxt):
            for c in (2, 3):
                in_specs.append(pl.BlockSpec((8, bw), functools.partial(fn, c=c)))
                args.append(rest)
        in_specs.append(pl.BlockSpec(conv_w.shape, const))
        args.append(conv_w)
    else:
        in_specs += [pl.BlockSpec((tm, bw), row), pl.BlockSpec((tm, bw), lambda i: (i, 1))]
        args += [second, rest]
    in_specs += [pl.BlockSpec((1, 1, mod3.shape[2]), lambda i: (mod_row_fn(i // tps), 0, 0)),
                 pl.BlockSpec((1, d), const),
                 _resident(w_out_bf.shape)]
    args += [mod3, g_post.reshape(1, d), w_out_bf]
    return pl.pallas_call(
        functools.partial(_post_kernel, even=even, tm=tm, seq=seq),
        grid=(n // tm,), in_specs=in_specs,
        out_specs=pl.BlockSpec((tm, d), row),
        out_shape=jax.ShapeDtypeStruct((n, d), F32),
        compiler_params=_params(1), name="post_even" if even else "post_odd",
    )(*args)


def _rope_tables(seq):
    t = np.arange(seq)
    row = (t // GRID_W).astype(np.float64)
    col = (t % GRID_W).astype(np.float64)
    nf = A_QK // 4
    inv_freq = ROPE_BASE ** (-np.arange(nf, dtype=np.float64) / nf)
    p = np.arange(A_QK)
    f_idx = p % nf
    use_col = (p % A_QK) >= A_QK // 2
    sign = np.where((p % (2 * nf)) < nf, -1.0, 1.0)
    pos = np.where(use_col[None, :], col[:, None], row[:, None])
    ang = pos * inv_freq[f_idx][None, :]
    cos = np.cos(ang).astype(np.float32)
    sin = (np.sin(ang) * sign[None, :]).astype(np.float32)
    reps = LANES // A_QK
    return (jnp.asarray(np.tile(cos, (1, reps))), jnp.asarray(np.tile(sin, (1, reps))),
            jnp.asarray(np.ascontiguousarray(cos.T)), jnp.asarray(np.ascontiguousarray(sin.T)))


def _lambda_init(layer):
    return 0.8 - 0.6 * math.exp(-0.3 * layer)


def kernel(x, c, ctx, c_ctx, w_mod, b_mod, norm_pre, norm_post, w_in_even, lam_a, subln_a, conv_b,
           w_in_odd, rpb_c, w_out):
    nb, seq, d = x.shape
    lctx = ctx.shape[1]
    depth = w_mod.shape[0]
    assert seq % (NA_QROWS * GRID_W * NA_GROUPS_PER_STEP) == 0 and seq // GRID_W >= NA_KROWS
    assert lctx % LANES == 0 and seq % KEY_CHUNK == 0 and nb < 16

    pad = jnp.zeros((16 - nb - 1, d), F32)
    cc = jnp.concatenate([c, c_ctx[None, :], pad], axis=0)
    mods = _mod_all(cc, w_mod, b_mod)

    rope_tabs = _rope_tables(seq)
    cs_bf = _channel_dft_matrix()
    dft_tabs = _dft_factored_tables(seq)
    cn_c, sn_c = _dft_tables(lctx)

    tm_x = 1024
    tm_c = lctx
    x_row = lambda b: b
    c_row = lambda b: nb

    x2 = x.reshape(nb * seq, d)
    c2 = ctx.reshape(nb * lctx, d)
    for l in range(depth):
        need_ctx = l < depth - 1
        j = l // 2
        mod3 = mods[l].reshape(16, 1, 3 * d)
        w_out_bf = w_out[l].astype(BF16)
        if l % 2 == 0:
            w_bf = w_in_even[j].astype(BF16)
            li = _lambda_init(l)
            qt, k, vt, rest = _inproj_even(x2, mod3, x_row, norm_pre[l], w_bf, seq, rope_tabs, tm_x)
            qct, kc, vct, rest_c = _inproj_even(c2, mod3, c_row, norm_pre[l], w_bf, lctx, None, tm_c)
            a = _diff_attn(lam_a[j], subln_a[j], qt, kc, vct, k, vt, rest, li, 512)
            x2n = _post(x2, a, None, rest, conv_b[j], mod3, x_row, norm_post[l], w_out_bf, seq, tm_x, True)
            if need_ctx:
                ac = _diff_attn(lam_a[j], subln_a[j], qct, kc, vct, None, None, rest_c, li, lctx)
                c2 = _post(c2, ac, None, rest_c, conv_b[j], mod3, c_row, norm_post[l], w_out_bf,
                           lctx, tm_c, True)
            x2 = x2n
        else:
            w_bf = w_in_odd[j].astype(BF16)
            q, k, v, rest, p, pq = _inproj_odd(x2, mod3, x_row, norm_pre[l], w_bf, cs_bf, seq, tm_x)
            qc, kc, vc, rest_c, p_c, pq_c = _inproj_odd(c2, mod3, c_row, norm_pre[l], w_bf, cs_bf, lctx, tm_c)
            bias = _na_bias(rpb_c[j], seq // GRID_W)
            a = _na_attn(q, k, v, kc, vc, bias, rest)
            y = _dft_positions_factored(dft_tabs, p, pq)
            x2n = _post(x2, a, y, rest, None, mod3, x_row, norm_post[l], w_out_bf, seq, tm_x, False)
            if need_ctx:
                ac = _dense_attn(qc, kc, vc, rest_c)
                yc = _dft_positions(cn_c, sn_c, p_c, pq_c, lctx).reshape(nb * lctx, -1)
                c2 = _post(c2, ac, yc, rest_c, None, mod3, c_row, norm_post[l], w_out_bf, lctx, tm_c, False)
            x2 = x2n
    return x2.reshape(nb, seq, d)
```

```python
import functools
import math

import numpy as np
import jax
import jax.numpy as jnp
from jax import lax
from jax.experimental import pallas as pl
from jax.experimental.pallas import tpu as pltpu

GRID_W = 64
A_QK = 32
A_V = 64
C_HEAD_DIM = 64
NA_KH = 8
NA_KW = 16
D_GROUPS = 8
D_GROUP_W = 64
ROPE_BASE = 10000.0
EPS = 1e-6
LOG2E = 1.4426950408889634
NEG_BIG = -1e30

LANES = 128
HEAD_PAIR_W = 128
KEY_CHUNK = 256
DIFF_SUB_Q = 512
SCORE_AHEAD = 3
VMEM_LIMIT = 48 * 1024 * 1024

F32 = jnp.float32
BF16 = jnp.bfloat16
MIX_DTYPE = BF16


def _silu(v):
    return v * jax.nn.sigmoid(v)


def _params(n_axes):
    return pltpu.CompilerParams(dimension_semantics=("arbitrary",) * n_axes,
                                vmem_limit_bytes=VMEM_LIMIT)


def _bf16_table(values):
    return jnp.asarray(np.asarray(values, np.float32)).astype(BF16)


def _resident(shape):
    return pl.BlockSpec(shape, lambda *_: (0,) * len(shape), pipeline_mode=pl.Buffered(1))


def _nt_dot(a, b):
    return lax.dot_general(a, b, (((1,), (1,)), ((), ())), preferred_element_type=F32)


def _mod_kernel(cc_ref, w_ref, b_ref, o_ref):
    s = _silu(cc_ref[...]).astype(BF16)
    o_ref[0] = jnp.dot(s, w_ref[0].astype(BF16), preferred_element_type=F32) + b_ref[0]


def _mod_all(cc, w_mod, b_mod):
    depth, d, d3 = w_mod.shape
    tn = d
    return pl.pallas_call(
        _mod_kernel,
        grid=(depth, d3 // tn),
        in_specs=[pl.BlockSpec((cc.shape[0], d), lambda l, n: (0, 0)),
                  pl.BlockSpec((1, d, tn), lambda l, n: (l, 0, n)),
                  pl.BlockSpec((1, 1, tn), lambda l, n: (l, 0, n))],
        out_specs=pl.BlockSpec((1, cc.shape[0], tn), lambda l, n: (l, 0, n)),
        out_shape=jax.ShapeDtypeStruct((depth, cc.shape[0], d3), F32),
        compiler_params=_params(2), name="adaln_mod",
    )(cc, w_mod, b_mod.reshape(depth, 1, d3))


def _prenorm(x_ref, mod_ref, g_ref):
    d = x_ref.shape[-1]
    x = x_ref[...]
    y = x * lax.rsqrt(jnp.mean(x * x, axis=-1, keepdims=True) + EPS) * g_ref[...]
    m = mod_ref[0]
    h = y * (1.0 + m[:, d:2 * d]) + m[:, 0:d]
    return h.astype(BF16)


def _inproj_even_kernel(*refs, rope):
    if rope:
        (x_ref, mod_ref, g_ref, w_ref, wqt_ref, wvt_ref,
         cos_ref, sin_ref, cost_ref, sint_ref, qt_ref, k_ref, vt_ref, rest_ref) = refs
    else:
        (x_ref, mod_ref, g_ref, w_ref, wqt_ref, wvt_ref, qt_ref, k_ref, vt_ref, rest_ref) = refs
    bw = qt_ref.shape[1]
    hb = _prenorm(x_ref, mod_ref, g_ref)
    pair = A_QK // 4

    def rotary(u, axis, cos, sin):
        idx = lax.broadcasted_iota(jnp.int32, u.shape, axis)
        partner = jnp.where((idx & pair) == 0, pltpu.roll(u, bw - pair, axis), pltpu.roll(u, pair, axis))
        return u * cos + partner * sin

    qt = _nt_dot(wqt_ref[...], hb)
    if rope:
        reps = bw // cost_ref.shape[0]
        qt = rotary(qt, 0, jnp.tile(cost_ref[...], (reps, 1)), jnp.tile(sint_ref[...], (reps, 1)))
    qt_ref[0] = (qt * (A_QK ** -0.5 * LOG2E)).astype(BF16)

    k = jnp.dot(hb, w_ref[:, bw:2 * bw], preferred_element_type=F32)
    if rope:
        reps = bw // cos_ref.shape[1]
        k = rotary(k, 1, jnp.tile(cos_ref[...], (1, reps)), jnp.tile(sin_ref[...], (1, reps)))
    k_ref[0] = k.astype(BF16)

    vt_ref[0] = _nt_dot(wvt_ref[...], hb).astype(BF16)

    n_rest = rest_ref.shape[1] // bw
    for c in range(n_rest):
        col = (3 + c) * bw
        rest_ref[:, c * bw:(c + 1) * bw] = jnp.dot(
            hb, w_ref[:, col:col + bw], preferred_element_type=F32).astype(BF16)


def _inproj_even(x2, mod3, mod_row_fn, g_pre, w_bf, seq, rope_tabs, tm):
    n, d = x2.shape
    bw = w_bf.shape[1] // 8
    nb = n // seq
    tps = seq // tm
    rope = rope_tabs is not None
    wqt = w_bf[:, 0:bw].T
    wvt = w_bf[:, 2 * bw:3 * bw].T
    const = lambda i: (0, 0)
    in_specs = [pl.BlockSpec((tm, d), lambda i: (i, 0)),
                pl.BlockSpec((1, 1, mod3.shape[2]), lambda i: (mod_row_fn(i // tps), 0, 0)),
                pl.BlockSpec((1, d), const),
                _resident(w_bf.shape), _resident(wqt.shape), _resident(wvt.shape)]
    args = [x2, mod3, g_pre.reshape(1, d), w_bf, wqt, wvt]
    if rope:
        cos_n, sin_n, cos_t, sin_t = rope_tabs
        in_specs += [pl.BlockSpec((tm, cos_n.shape[1]), lambda i: (i % tps, 0)),
                     pl.BlockSpec((tm, cos_n.shape[1]), lambda i: (i % tps, 0)),
                     pl.BlockSpec((cos_t.shape[0], tm), lambda i: (0, i % tps)),
                     pl.BlockSpec((cos_t.shape[0], tm), lambda i: (0, i % tps))]
        args += [cos_n, sin_n, cos_t, sin_t]
    out_specs = [pl.BlockSpec((1, bw, tm), lambda i: (i // tps, 0, i % tps)),
                 pl.BlockSpec((1, tm, bw), lambda i: (i // tps, i % tps, 0)),
                 pl.BlockSpec((1, bw, tm), lambda i: (i // tps, 0, i % tps)),
                 pl.BlockSpec((tm, 5 * bw), lambda i: (i, 0))]
    out_shape = [jax.ShapeDtypeStruct((nb, bw, seq), BF16),
                 jax.ShapeDtypeStruct((nb, seq, bw), BF16),
                 jax.ShapeDtypeStruct((nb, bw, seq), BF16),
                 jax.ShapeDtypeStruct((n, 5 * bw), BF16)]
    return pl.pallas_call(
        functools.partial(_inproj_even_kernel, rope=rope),
        grid=(n // tm,), in_specs=in_specs, out_specs=out_specs, out_shape=out_shape,
        compiler_params=_params(1), name="inproj_even",
    )(*args)


def _inproj_odd_kernel(x_ref, mod_ref, g_ref, w_ref, wqt_ref, wvt_ref, cs_ref,
                       qt_ref, k_ref, vt_ref, rest_ref, p_ref, pq_ref):
    bw = k_ref.shape[2]
    hb = _prenorm(x_ref, mod_ref, g_ref)

    def proj(c):
        return jnp.dot(hb, w_ref[:, c * bw:(c + 1) * bw], preferred_element_type=F32)

    qt_ref[0] = (_nt_dot(wqt_ref[...], hb) * (C_HEAD_DIM ** -0.5 * LOG2E)).astype(BF16)
    k_ref[0] = proj(1).astype(BF16)
    vt = _nt_dot(wvt_ref[...], hb).astype(BF16)
    for c in range(vt_ref.shape[1]):
        vt_ref[0, c] = vt[:, c * NA_VT_CHUNK:(c + 1) * NA_VT_CHUNK]
    rest_ref[:, 0:bw] = proj(3).astype(BF16)
    rest_ref[:, bw:2 * bw] = proj(5).astype(BF16)
    pq = jnp.dot(proj(4).astype(BF16), cs_ref[...], preferred_element_type=F32)
    p_ref[0] = pq[:, 0:bw].astype(BF16)
    pq_ref[0] = pq[:, bw:2 * bw].astype(BF16)


def _inproj_odd(x2, mod3, mod_row_fn, g_pre, w_bf, cs_bf, seq, tm):
    n, d = x2.shape
    bw = w_bf.shape[1] // 6
    nb = n // seq
    tps = seq // tm
    const = lambda i: (0, 0)
    wqt = w_bf[:, 0:bw].T
    wvt = w_bf[:, 2 * bw:3 * bw].T
    cpt = tm // NA_VT_CHUNK
    tok = pl.BlockSpec((1, tm, bw), lambda i: (i // tps, i % tps, 0))
    tok_shape = jax.ShapeDtypeStruct((nb, seq, bw), BF16)
    return pl.pallas_call(
        _inproj_odd_kernel,
        grid=(n // tm,),
        in_specs=[pl.BlockSpec((tm, d), lambda i: (i, 0)),
                  pl.BlockSpec((1, 1, mod3.shape[2]), lambda i: (mod_row_fn(i // tps), 0, 0)),
                  pl.BlockSpec((1, d), const),
                  _resident(w_bf.shape), _resident(wqt.shape), _resident(wvt.shape),
                  _resident(cs_bf.shape)],
        out_specs=[pl.BlockSpec((1, bw, tm), lambda i: (i // tps, 0, i % tps)),
                   tok,
                   pl.BlockSpec((1, cpt, bw, NA_VT_CHUNK), lambda i: (i // tps, i % tps, 0, 0)),
                   pl.BlockSpec((tm, 2 * bw), lambda i: (i, 0)), tok, tok],
        out_shape=[jax.ShapeDtypeStruct((nb, bw, seq), BF16),
                   tok_shape,
                   jax.ShapeDtypeStruct((nb, seq // NA_VT_CHUNK, bw, NA_VT_CHUNK), BF16),
                   jax.ShapeDtypeStruct((n, 2 * bw), BF16), tok_shape, tok_shape],
        compiler_params=_params(1), name="inproj_odd",
    )(x2, mod3, g_pre.reshape(1, d), w_bf, wqt, wvt, cs_bf)


def _diff_attn_kernel(*refs, has_main, tq, lam_init):
    if has_main:
        (lam_ref, sub_ref, qt_ref, kc_ref, vct_ref, ka_ref, vat_ref, ga_ref, o_ref) = refs
        key_refs = [(kc_ref, vct_ref), (ka_ref, vat_ref)]
    else:
        (lam_ref, sub_ref, qt_ref, kc_ref, vct_ref, ga_ref, o_ref) = refs
        key_refs = [(kc_ref, vct_ref)]
    chunks = []
    for k_ref, vt_ref in key_refs:
        for st in range(0, k_ref.shape[1], KEY_CHUNK):
            chunks.append((k_ref, vt_ref, st, min(KEY_CHUNK, k_ref.shape[1] - st)))
    lp = lam_ref[...]
    lam = (jnp.exp(jnp.sum(lp[0:1] * lp[1:2], axis=1, keepdims=True))
           - jnp.exp(jnp.sum(lp[2:3] * lp[3:4], axis=1, keepdims=True)) + lam_init)
    sq = min(tq, DIFF_SUB_Q)
    qf = qt_ref[0].astype(F32)
    rowid = lax.broadcasted_iota(jnp.int32, (qf.shape[0], sq), 0)
    sub = jnp.tile(sub_ref[...], (1, sq // LANES))

    def fold(v):
        return v.reshape(v.shape[0] // 8, 8, sq)

    items = [(j, hh, m) for j in range(tq // sq) for hh in range(2) for m in range(2)]
    work = [(t, ci) for t in range(len(items)) for ci in range(len(chunks))]
    qms, state, maps, pending = {}, {}, {}, []

    def issue(t, ci):
        j, hh, m = items[t]
        if ci == 0:
            lo = hh * A_V + m * A_QK
            qms[t] = jnp.where((rowid >= lo) & (rowid < lo + A_QK), qf[:, j * sq:(j + 1) * sq],
                               0.0).astype(BF16)
        k_ref, _, st, sz = chunks[ci]
        sc = jnp.dot(k_ref[0, st:st + sz, :], qms[t], preferred_element_type=F32)
        pending.append((t, ci, sc, jnp.max(jnp.max(fold(sc), axis=0), axis=0, keepdims=True)))

    def consume():
        t, ci, s, cm = pending.pop(0)
        hh = items[t][1]
        _, vt_ref, st, sz = chunks[ci]
        if ci == 0:
            state[t] = (jnp.full((1, sq), -jnp.inf, F32), jnp.zeros((8, sq), F32),
                        jnp.zeros((A_V, sq), F32))
        m_run, l, acc = state[t]
        m_new = jnp.maximum(m_run, cm)
        alpha = jnp.exp2(m_run - m_new)
        e = jnp.exp2(s - m_new)
        l = alpha * l + jnp.sum(fold(e), axis=0)
        acc = alpha * acc + jnp.dot(vt_ref[0, hh * A_V:(hh + 1) * A_V, st:st + sz], e.astype(BF16),
                                    preferred_element_type=F32)
        state[t] = (m_new, l, acc)
        if ci == len(chunks) - 1:
            del state[t], qms[t]
            maps[t] = acc * (1.0 / jnp.sum(l, axis=0, keepdims=True))

    for idx in range(len(work) + SCORE_AHEAD):
        if idx < len(work):
            issue(*work[idx])
        if idx >= SCORE_AHEAD:
            consume()
    for j in range(tq // sq):
        heads = []
        for hh in range(2):
            ot = maps[4 * j + 2 * hh] - lam * maps[4 * j + 2 * hh + 1]
            ot = ot * lax.rsqrt(jnp.mean(ot * ot, axis=0, keepdims=True) + EPS) * sub * (1.0 - lam_init)
            heads.append(ot)
        o = jnp.concatenate(heads, axis=0).T
        rows = slice(j * sq, (j + 1) * sq)
        o_ref[rows, :] = (o * _silu(ga_ref[rows, :].astype(F32))).astype(o_ref.dtype)


def _diff_attn(lam_p, subln, qt, kc, vct, ka, vat, rest, lam_init, tq):
    nb, bw, sq = qt.shape
    nq = sq // tq
    hp = HEAD_PAIR_W
    sub_b = jnp.broadcast_to(subln.reshape(A_V, 1), (A_V, LANES))
    in_specs = [pl.BlockSpec(lam_p.shape, lambda b, h, i: (0, 0)),
                pl.BlockSpec(sub_b.shape, lambda b, h, i: (0, 0)),
                pl.BlockSpec((1, hp, tq), lambda b, h, i: (b, h, i)),
                pl.BlockSpec((1, kc.shape[1], hp), lambda b, h, i: (b, 0, h)),
                pl.BlockSpec((1, hp, kc.shape[1]), lambda b, h, i: (b, h, 0))]
    args = [lam_p, sub_b, qt, kc, vct]
    if ka is not None:
        in_specs += [pl.BlockSpec((1, ka.shape[1], hp), lambda b, h, i: (b, 0, h)),
                     pl.BlockSpec((1, hp, ka.shape[1]), lambda b, h, i: (b, h, 0))]
        args += [ka, vat]
    in_specs.append(pl.BlockSpec((tq, hp), lambda b, h, i: (b * nq + i, h)))
    args.append(rest)
    return pl.pallas_call(
        functools.partial(_diff_attn_kernel, has_main=ka is not None, tq=tq, lam_init=lam_init),
        grid=(nb, bw // hp, nq),
        in_specs=in_specs,
        out_specs=pl.BlockSpec((tq, hp), lambda b, h, i: (b * nq + i, h)),
        out_shape=jax.ShapeDtypeStruct((nb * sq, bw), MIX_DTYPE),
        compiler_params=_params(3), name="diff_attn",
    )(*args)


NA_QROWS = 4
NA_KROWS = 12
NA_VT_CHUNK = 256
NA_GROUPS_PER_STEP = 8


def _softmax_pv(problems):
    tq = problems[0][0].shape[1]
    rowid = lax.broadcasted_iota(jnp.int32, problems[0][0].shape, 0)
    chains = [(p, hh) for p in range(len(problems)) for hh in range(2)]
    scores, exps, outs = {}, {}, {}

    def fold(v):
        return v.reshape(v.shape[0] // 8, 8, tq)

    def score(p, hh):
        qt, pieces, _ = problems[p]
        qm = jnp.where((rowid >= hh * C_HEAD_DIM) & (rowid < (hh + 1) * C_HEAD_DIM), qt.astype(F32),
                       0.0).astype(BF16)
        scores[p, hh] = [jnp.dot(k, qm, preferred_element_type=F32) + (0.0 if bias is None else bias[hh])
                         for k, _, bias in pieces]

    def weigh(c):
        ss = scores.pop(c)
        m = jnp.max(functools.reduce(jnp.maximum, [jnp.max(fold(s), axis=0) for s in ss]),
                    axis=0, keepdims=True)
        exps[c] = [jnp.exp2(s - m) for s in ss]

    def accumulate(p, hh):
        es = exps.pop((p, hh))
        l = jnp.sum(functools.reduce(jnp.add, [jnp.sum(fold(e), axis=0) for e in es]), axis=0, keepdims=True)
        pv = jnp.zeros((C_HEAD_DIM, tq), F32)
        for e, (_, vts, _) in zip(es, problems[p][1]):
            eb = e.astype(BF16)
            off = 0
            for vt in vts:
                pv = pv + jnp.dot(vt[hh * C_HEAD_DIM:(hh + 1) * C_HEAD_DIM, :], eb[off:off + vt.shape[1], :],
                                  preferred_element_type=F32)
                off += vt.shape[1]
        outs[p, hh] = pv * (1.0 / l)

    for idx in range(len(chains) + 2):
        if idx < len(chains):
            score(*chains[idx])
        if 1 <= idx <= len(chains):
            weigh(chains[idx - 1])
        if idx >= 2:
            accumulate(*chains[idx - 2])
    return [jnp.concatenate([outs[p, 0], outs[p, 1]], axis=0).T * _silu(problems[p][2].astype(F32))
            for p in range(len(problems))]


class _BiasOf:
    def __init__(self, ref):
        self.ref = ref

    def __getitem__(self, hh):
        return self.ref[0, hh]


def _na_kernel(qt_ref, k_ref, vt_ref, kc_ref, vct_ref, *rest, rows, n_sub):
    bias_refs, (gate_ref, o_ref) = rest[:n_sub], rest[n_sub:]
    tq = NA_QROWS * GRID_W
    nk = NA_KROWS * GRID_W
    ctx_piece = (kc_ref[0], [vct_ref[0, c] for c in range(vct_ref.shape[1])], None)
    problems = []
    for u in range(n_sub):
        g = pl.program_id(2) * n_sub + u
        base = jnp.clip(NA_QROWS * g - NA_KH // 2, 0, rows - NA_KROWS)
        start = pl.multiple_of(base * GRID_W, NA_VT_CHUNK)
        c0 = base // (NA_VT_CHUNK // GRID_W)
        kw = k_ref[0, pl.ds(start, nk), :]
        vts = [vt_ref[0, c0 + c] for c in range(nk // NA_VT_CHUNK)]
        qrows = slice(u * tq, (u + 1) * tq)
        problems.append((qt_ref[0, :, qrows], [(kw, vts, _BiasOf(bias_refs[u])), ctx_piece],
                         gate_ref[qrows, :]))
    for u, o in enumerate(_softmax_pv(problems)):
        o_ref[u * tq:(u + 1) * tq, :] = o.astype(o_ref.dtype)


def _na_case_tables(rows):
    rs = np.clip(np.arange(rows) - NA_KH // 2, 0, rows - NA_KH)
    n_groups = rows // NA_QROWS
    pats = []
    for g in range(n_groups):
        base = int(np.clip(NA_QROWS * g - NA_KH // 2, 0, rows - NA_KROWS))
        pat = np.full((NA_QROWS, NA_KROWS), 2 * NA_KH - 1, np.int32)
        for j in range(NA_QROWS):
            r = NA_QROWS * g + j
            for i in range(NA_KROWS):
                kr = base + i
                if rs[r] <= kr < rs[r] + NA_KH:
                    pat[j, i] = kr - r + NA_KH - 1
            assert (pat[j] != 2 * NA_KH - 1).sum() == NA_KH
        pats.append(pat)
    cases = np.stack([pats[0], pats[1], pats[-1]])
    for g in range(n_groups):
        want = 0 if g == 0 else (2 if g == n_groups - 1 else 1)
        assert (pats[g] == cases[want]).all()
    return cases


def _na_bias(rpb, rows):
    h = rpb.shape[0]
    cq = np.arange(GRID_W)
    cs = np.clip(cq - NA_KW // 2, 0, GRID_W - NA_KW)
    kcol = np.arange(GRID_W)
    col_ok = ((kcol[None, :] >= cs[:, None]) & (kcol[None, :] < cs[:, None] + NA_KW)).T
    col_off = np.clip(kcol[None, :] - cq[:, None] + NA_KW - 1, 0, 2 * NA_KW - 2).T
    pick = ((col_off[None] == np.arange(2 * NA_KW - 1)[:, None, None]) & col_ok[None]).astype(np.float32)
    toep = jnp.dot((rpb * LOG2E).reshape(-1, 2 * NA_KW - 1), pick.reshape(2 * NA_KW - 1, -1),
                   precision=lax.Precision.HIGHEST).reshape(h, -1, GRID_W, GRID_W)
    toep = toep + np.where(col_ok, 0.0, NEG_BIG).astype(np.float32)
    toep = jnp.concatenate([toep, jnp.full((h, 1, GRID_W, GRID_W), NEG_BIG, F32)], axis=1)
    cases = _na_case_tables(rows)
    n_case, qr, kr = cases.shape
    return pl.pallas_call(
        functools.partial(_na_bias_kernel, cases=cases),
        grid=(h,),
        in_specs=[pl.BlockSpec((1,) + toep.shape[1:], lambda i: (i, 0, 0, 0))],
        out_specs=pl.BlockSpec((n_case, 1, kr * GRID_W, qr * GRID_W), lambda i: (0, i, 0, 0)),
        out_shape=jax.ShapeDtypeStruct((n_case, h, kr * GRID_W, qr * GRID_W), F32),
        compiler_params=_params(1), name="nbr_bias",
    )(toep)


def _na_bias_kernel(toep_ref, o_ref, *, cases):
    for c in range(cases.shape[0]):
        for i in range(cases.shape[2]):
            o_ref[c, 0, i * GRID_W:(i + 1) * GRID_W, :] = jnp.concatenate(
                [toep_ref[0, int(b)] for b in cases[c, :, i]], axis=1)


def _na_attn(qt, k, vt, kc, vct, bias, rest):
    nb, bw, s = qt.shape
    hp = HEAD_PAIR_W
    tq = NA_QROWS * GRID_W
    ng = s // tq
    n_sub = NA_GROUPS_PER_STEP
    nsteps = ng // n_sub
    l = kc.shape[1]

    def bias_spec(u):
        def index(b, h, i):
            g = i * n_sub + u
            return (jnp.where(g == 0, 0, jnp.where(g == ng - 1, 2, 1)), h, 0, 0)
        return pl.BlockSpec((1, 2, NA_KROWS * GRID_W, tq), index)

    return pl.pallas_call(
        functools.partial(_na_kernel, rows=s // GRID_W, n_sub=n_sub),
        grid=(nb, bw // hp, nsteps),
        in_specs=[pl.BlockSpec((1, hp, n_sub * tq), lambda b, h, i: (b, h, i)),
                  pl.BlockSpec((1, s, hp), lambda b, h, i: (b, 0, h)),
                  pl.BlockSpec((1, vt.shape[1], hp, NA_VT_CHUNK), lambda b, h, i: (b, 0, h, 0)),
                  pl.BlockSpec((1, l, hp), lambda b, h, i: (b, 0, h)),
                  pl.BlockSpec((1, vct.shape[1], hp, NA_VT_CHUNK), lambda b, h, i: (b, 0, h, 0))]
                 + [bias_spec(u) for u in range(n_sub)]
                 + [pl.BlockSpec((n_sub * tq, hp), lambda b, h, i: (b * nsteps + i, h))],
        out_specs=pl.BlockSpec((n_sub * tq, hp), lambda b, h, i: (b * nsteps + i, h)),
        out_shape=jax.ShapeDtypeStruct((nb * s, bw), MIX_DTYPE),
        compiler_params=_params(3), name="nbr_attn",
    )(qt, k, vt, kc, vct, *([bias] * n_sub), rest)


def _dense_attn_kernel(qt_ref, k_ref, vt_ref, gate_ref, o_ref):
    piece = (k_ref[0], [vt_ref[0, c] for c in range(vt_ref.shape[1])], None)
    o_ref[...] = _softmax_pv([(qt_ref[0], [piece], gate_ref[...])])[0].astype(o_ref.dtype)


def _dense_attn(qt, k, vt, rest):
    nb, bw, l = qt.shape
    hp = HEAD_PAIR_W
    return pl.pallas_call(
        _dense_attn_kernel,
        grid=(nb, bw // hp),
        in_specs=[pl.BlockSpec((1, hp, l), lambda b, h: (b, h, 0)),
                  pl.BlockSpec((1, l, hp), lambda b, h: (b, 0, h)),
                  pl.BlockSpec((1, vt.shape[1], hp, NA_VT_CHUNK), lambda b, h: (b, 0, h, 0)),
                  pl.BlockSpec((l, hp), lambda b, h: (b, h))],
        out_specs=pl.BlockSpec((l, hp), lambda b, h: (b, h)),
        out_shape=jax.ShapeDtypeStruct((nb * l, bw), MIX_DTYPE),
        compiler_params=_params(2), name="ctx_dense_attn",
    )(qt, k, vt, rest)


def _dft_kernel(c_ref, s_ref, p_ref, q_ref, o_ref):
    o_ref[0] = (jnp.dot(c_ref[...], p_ref[0], preferred_element_type=F32)
                + jnp.dot(s_ref[...], q_ref[0], preferred_element_type=F32))


def _dft_tables(n):
    idx = np.arange(n)
    ang = ((idx[:, None] * idx[None, :]) % n) * (2.0 * np.pi / n)
    scale = n ** -0.5
    return _bf16_table(np.cos(ang) * scale), _bf16_table(np.sin(ang) * scale)


def _dft_pos_kernel(m1_ref, m2_ref, g_ref, vr_ref, vi_ref, o_ref, a_s, *, nt):
    r = vr_ref.shape[1]
    for c in range(GRID_W // nt):
        cols = slice(c * nt, (c + 1) * nt)
        vrt = jnp.swapaxes(vr_ref[0, :, cols, :].astype(F32), 0, 1)
        vit = jnp.swapaxes(vi_ref[0, :, cols, :].astype(F32), 0, 1)
        outs = []
        for j in range(nt):
            outs.append(jnp.dot(m1_ref[...], vrt[j].astype(BF16), preferred_element_type=F32)
                        + jnp.dot(m2_ref[...], vit[j].astype(BF16), preferred_element_type=F32))
        at = jnp.swapaxes(jnp.stack(outs, axis=0), 0, 1)
        a_s[:, :, cols, :] = at.reshape(2, r, nt, LANES)
    k1t = min(nt, r)
    for c in range(r // k1t):
        outs = []
        for j in range(c * k1t, (c + 1) * k1t):
            a = jnp.concatenate([a_s[0, j], a_s[1, j]], axis=0).astype(BF16)
            outs.append(jnp.dot(g_ref[j], a, preferred_element_type=F32))
        o_ref[0, :, c * k1t:(c + 1) * k1t, :] = jnp.swapaxes(jnp.stack(outs, axis=0), 0, 1)


def _dft_factored_tables(n):
    r = n // GRID_W
    i = np.arange(r)
    ang = ((i[:, None] * i[None, :]) % r) * (2.0 * np.pi / r)
    c, s = np.cos(ang) * r ** -0.5, np.sin(ang) * r ** -0.5
    m1 = np.concatenate([c, -s], axis=0)
    m2 = np.concatenate([s, c], axis=0)
    k = i[:, None, None] + r * np.arange(GRID_W)[None, :, None]
    th = ((k * np.arange(GRID_W)[None, None, :]) % n) * (2.0 * np.pi / n)
    g = np.concatenate([np.cos(th), np.sin(th)], axis=2) * GRID_W ** -0.5
    return _bf16_table(m1), _bf16_table(m2), _bf16_table(g)


def _dft_positions_factored(tabs, p, q):
    m1, m2, g = tabs
    nb, n, bw = p.shape
    r = n // GRID_W
    tok = pl.BlockSpec((1, r, GRID_W, LANES), lambda b, l: (b, 0, 0, l))
    y = pl.pallas_call(
        functools.partial(_dft_pos_kernel, nt=32),
        grid=(nb, bw // LANES),
        in_specs=[pl.BlockSpec(m1.shape, lambda b, l: (0, 0)),
                  pl.BlockSpec(m2.shape, lambda b, l: (0, 0)),
                  pl.BlockSpec(g.shape, lambda b, l: (0, 0, 0)),
                  tok, tok],
        out_specs=pl.BlockSpec((1, GRID_W, r, LANES), lambda b, l: (b, 0, 0, l)),
        out_shape=jax.ShapeDtypeStruct((nb, GRID_W, r, bw), F32),
        scratch_shapes=[pltpu.VMEM((2, r, GRID_W, LANES), F32)],
        compiler_params=_params(2), name="dft_positions_factored",
    )(m1, m2, g, p.reshape(nb, r, GRID_W, bw), q.reshape(nb, r, GRID_W, bw))
    return y.reshape(nb * n, bw)


def _dft_positions(cn, sn, p, q, tmo):
    nb, n, bw = p.shape
    return pl.pallas_call(
        _dft_kernel,
        grid=(n // tmo, nb),
        in_specs=[pl.BlockSpec((tmo, n), lambda i, b: (i, 0)),
                  pl.BlockSpec((tmo, n), lambda i, b: (i, 0)),
                  pl.BlockSpec((1, n, bw), lambda i, b: (b, 0, 0)),
                  pl.BlockSpec((1, n, bw), lambda i, b: (b, 0, 0))],
        out_specs=pl.BlockSpec((1, tmo, bw), lambda i, b: (b, i, 0)),
        out_shape=jax.ShapeDtypeStruct((nb, n, bw), F32),
        compiler_params=_params(2), name="dft_positions",
    )(cn, sn, p, q)


def _channel_dft_matrix():
    j = np.arange(D_GROUP_W)
    ang = 2.0 * np.pi * ((j[:, None] * j[None, :]) % D_GROUP_W) / D_GROUP_W
    eye = np.eye(D_GROUPS)
    cc = np.kron(eye, np.cos(ang) * D_GROUP_W ** -0.5)
    sc = np.kron(eye, np.sin(ang) * D_GROUP_W ** -0.5)
    return _bf16_table(np.concatenate([cc, -sc], axis=1))


def _post_kernel(*refs, even, tm, seq):
    if even:
        (x_ref, a_ref, bb_ref, cb_ref, ub_ref, gb_ref, cbp_ref, ubp_ref, cbn_ref, ubn_ref,
         cw_ref, mod_ref, g_ref, w_ref, o_ref) = refs
    else:
        (x_ref, a_ref, y_ref, gd_ref, mod_ref, g_ref, w_ref, o_ref) = refs
    d = x_ref.shape[-1]
    bw = a_ref.shape[-1]
    if even:
        i = pl.program_id(0)
        t0 = (i * tm) % seq
        z = cb_ref[...].astype(F32) * ub_ref[...].astype(F32)
        zp = cbp_ref[7:8, :].astype(F32) * ubp_ref[7:8, :].astype(F32) * jnp.where(t0 == 0, 0.0, 1.0)
        zn = cbn_ref[0:1, :].astype(F32) * ubn_ref[0:1, :].astype(F32) * jnp.where(t0 + tm == seq, 0.0, 1.0)
        rid = lax.broadcasted_iota(jnp.int32, z.shape, 0)
        z_prev = jnp.where(rid == 0, zp, pltpu.roll(z, 1, 0))
        z_next = jnp.where(rid == tm - 1, zn, pltpu.roll(z, tm - 1, 0))
        cw = cw_ref[...]
        y = cw[0:1] * z_prev + cw[1:2] * z + cw[2:3] * z_next
        second = bb_ref[...].astype(F32) * y * _silu(gb_ref[...].astype(F32))
    else:
        second = y_ref[...] * _silu(gd_ref[...].astype(F32))
    yo = (jnp.dot(a_ref[...].astype(BF16), w_ref[0:bw, :], preferred_element_type=F32)
          + jnp.dot(second.astype(BF16), w_ref[bw:2 * bw, :], preferred_element_type=F32))
    nrm = yo * lax.rsqrt(jnp.mean(yo * yo, axis=-1, keepdims=True) + EPS) * g_ref[...]
    o_ref[...] = x_ref[...] + mod_ref[0][:, 2 * d:3 * d] * nrm


def _post(x2, a, second, rest, conv_w, mod3, mod_row_fn, g_post, w_out_bf, seq, tm, even):
    n, d = x2.shape
    bw = a.shape[1]
    tps = seq // tm
    const = lambda i: (0, 0)
    row = lambda i: (i, 0)
    in_specs = [pl.BlockSpec((tm, d), row), pl.BlockSpec((tm, bw), row)]
    args = [x2, a]
    if even:
        nblk8 = n // 8
        for c in (1, 2, 3, 4):
            in_specs.append(pl.BlockSpec((tm, bw), lambda i, c=c: (i, c)))
            args.append(rest)
        prev = lambda i, c: (jnp.maximum(i * (tm // 8) - 1, 0), c)
        nxt = lambda i, c: (jnp.minimum((i + 1) * (tm // 8), nblk8 - 1), c)
        for fn in (prev, nxt):
            for c in (2, 3):
                in_specs.append(pl.BlockSpec((8, bw), functools.partial(fn, c=c)))
                args.append(rest)
        in_specs.append(pl.BlockSpec(conv_w.shape, const))
        args.append(conv_w)
    else:
        in_specs += [pl.BlockSpec((tm, bw), row), pl.BlockSpec((tm, bw), lambda i: (i, 1))]
        args += [second, rest]
    in_specs += [pl.BlockSpec((1, 1, mod3.shape[2]), lambda i: (mod_row_fn(i // tps), 0, 0)),
                 pl.BlockSpec((1, d), const),
                 _resident(w_out_bf.shape)]
    args += [mod3, g_post.reshape(1, d), w_out_bf]
    return pl.pallas_call(
        functools.partial(_post_kernel, even=even, tm=tm, seq=seq),
        grid=(n // tm,), in_specs=in_specs,
        out_specs=pl.BlockSpec((tm, d), row),
        out_shape=jax.ShapeDtypeStruct((n, d), F32),
        compiler_params=_params(1), name="post_even" if even else "post_odd",
    )(*args)


def _rope_tables(seq):
    t = np.arange(seq)
    row = (t // GRID_W).astype(np.float64)
    col = (t % GRID_W).astype(np.float64)
    nf = A_QK // 4
    inv_freq = ROPE_BASE ** (-np.arange(nf, dtype=np.float64) / nf)
    p = np.arange(A_QK)
    f_idx = p % nf
    use_col = (p % A_QK) >= A_QK // 2
    sign = np.where((p % (2 * nf)) < nf, -1.0, 1.0)
    pos = np.where(use_col[None, :], col[:, None], row[:, None])
    ang = pos * inv_freq[f_idx][None, :]
    cos = np.cos(ang).astype(np.float32)
    sin = (np.sin(ang) * sign[None, :]).astype(np.float32)
    reps = LANES // A_QK
    return (jnp.asarray(np.tile(cos, (1, reps))), jnp.asarray(np.tile(sin, (1, reps))),
            jnp.asarray(np.ascontiguousarray(cos.T)), jnp.asarray(np.ascontiguousarray(sin.T)))


def _lambda_init(layer):
    return 0.8 - 0.6 * math.exp(-0.3 * layer)


def kernel(x, c, ctx, c_ctx, w_mod, b_mod, norm_pre, norm_post, w_in_even, lam_a, subln_a, conv_b,
           w_in_odd, rpb_c, w_out):
    nb, seq, d = x.shape
    lctx = ctx.shape[1]
    depth = w_mod.shape[0]
    assert seq % (NA_QROWS * GRID_W * NA_GROUPS_PER_STEP) == 0 and seq // GRID_W >= NA_KROWS
    assert lctx % LANES == 0 and seq % KEY_CHUNK == 0 and nb < 16

    pad = jnp.zeros((16 - nb - 1, d), F32)
    cc = jnp.concatenate([c, c_ctx[None, :], pad], axis=0)
    mods = _mod_all(cc, w_mod, b_mod)

    rope_tabs = _rope_tables(seq)
    cs_bf = _channel_dft_matrix()
    dft_tabs = _dft_factored_tables(seq)
    cn_c, sn_c = _dft_tables(lctx)

    tm_x = 1024
    tm_c = lctx
    x_row = lambda b: b
    c_row = lambda b: nb

    x2 = x.reshape(nb * seq, d)
    c2 = ctx.reshape(nb * lctx, d)
    for l in range(depth):
        need_ctx = l < depth - 1
        j = l // 2
        mod3 = mods[l].reshape(16, 1, 3 * d)
        w_out_bf = w_out[l].astype(BF16)
        if l % 2 == 0:
            w_bf = w_in_even[j].astype(BF16)
            li = _lambda_init(l)
            qt, k, vt, rest = _inproj_even(x2, mod3, x_row, norm_pre[l], w_bf, seq, rope_tabs, tm_x)
            qct, kc, vct, rest_c = _inproj_even(c2, mod3, c_row, norm_pre[l], w_bf, lctx, None, tm_c)
            a = _diff_attn(lam_a[j], subln_a[j], qt, kc, vct, k, vt, rest, li, 512)
            x2n = _post(x2, a, None, rest, conv_b[j], mod3, x_row, norm_post[l], w_out_bf, seq, tm_x, True)
            if need_ctx:
                ac = _diff_attn(lam_a[j], subln_a[j], qct, kc, vct, None, None, rest_c, li, lctx)
                c2 = _post(c2, ac, None, rest_c, conv_b[j], mod3, c_row, norm_post[l], w_out_bf,
                           lctx, tm_c, True)
            x2 = x2n
        else:
            w_bf = w_in_odd[j].astype(BF16)
            q, k, v, rest, p, pq = _inproj_odd(x2, mod3, x_row, norm_pre[l], w_bf, cs_bf, seq, tm_x)
            qc, kc, vc, rest_c, p_c, pq_c = _inproj_odd(c2, mod3, c_row, norm_pre[l], w_bf, cs_bf, lctx, tm_c)
            bias = _na_bias(rpb_c[j], seq // GRID_W)
            a = _na_attn(q, k, v, kc, vc, bias, rest)
            y = _dft_positions_factored(dft_tabs, p, pq)
            x2n = _post(x2, a, y, rest, None, mod3, x_row, norm_post[l], w_out_bf, seq, tm_x, False)
            if need_ctx:
                ac = _dense_attn(qc, kc, vc, rest_c)
                yc = _dft_positions(cn_c, sn_c, p_c, pq_c, lctx).reshape(nb * lctx, -1)
                c2 = _post(c2, ac, yc, rest_c, None, mod3, c_row, norm_post[l], w_out_bf, lctx, tm_c, False)
            x2 = x2n
    return x2.reshape(nb, seq, d)
```

```python
import functools
import math

import numpy as np
import jax
import jax.numpy as jnp
from jax import lax
from jax.experimental import pallas as pl
from jax.experimental.pallas import tpu as pltpu

GRID_W = 64
A_QK = 32
A_V = 64
C_HEAD_DIM = 64
NA_KH = 8
NA_KW = 16
D_GROUPS = 8
D_GROUP_W = 64
ROPE_BASE = 10000.0
EPS = 1e-6
LOG2E = 1.4426950408889634
NEG_BIG = -1e30

LANES = 128
HEAD_PAIR_W = 128
KEY_CHUNK = 256
DIFF_SUB_Q = 512
SCORE_AHEAD = 3
VMEM_LIMIT = 48 * 1024 * 1024

F32 = jnp.float32
BF16 = jnp.bfloat16
MIX_DTYPE = BF16


def _silu(v):
    return v * jax.nn.sigmoid(v)


def _params(n_axes):
    return pltpu.CompilerParams(dimension_semantics=("arbitrary",) * n_axes,
                                vmem_limit_bytes=VMEM_LIMIT)


def _bf16_table(values):
    return jnp.asarray(np.asarray(values, np.float32)).astype(BF16)


def _resident(shape):
    return pl.BlockSpec(shape, lambda *_: (0,) * len(shape), pipeline_mode=pl.Buffered(1))


def _nt_dot(a, b):
    return lax.dot_general(a, b, (((1,), (1,)), ((), ())), preferred_element_type=F32)


def _mod_kernel(cc_ref, w_ref, b_ref, o_ref):
    s = _silu(cc_ref[...]).astype(BF16)
    o_ref[0] = jnp.dot(s, w_ref[0].astype(BF16), preferred_element_type=F32) + b_ref[0]


def _mod_all(cc, w_mod, b_mod):
    depth, d, d3 = w_mod.shape
    tn = d
    return pl.pallas_call(
        _mod_kernel,
        grid=(depth, d3 // tn),
        in_specs=[pl.BlockSpec((cc.shape[0], d), lambda l, n: (0, 0)),
                  pl.BlockSpec((1, d, tn), lambda l, n: (l, 0, n)),
                  pl.BlockSpec((1, 1, tn), lambda l, n: (l, 0, n))],
        out_specs=pl.BlockSpec((1, cc.shape[0], tn), lambda l, n: (l, 0, n)),
        out_shape=jax.ShapeDtypeStruct((depth, cc.shape[0], d3), F32),
        compiler_params=_params(2), name="adaln_mod",
    )(cc, w_mod, b_mod.reshape(depth, 1, d3))


def _prenorm(x_ref, mod_ref, g_ref):
    d = x_ref.shape[-1]
    x = x_ref[...]
    y = x * lax.rsqrt(jnp.mean(x * x, axis=-1, keepdims=True) + EPS) * g_ref[...]
    m = mod_ref[0]
    h = y * (1.0 + m[:, d:2 * d]) + m[:, 0:d]
    return h.astype(BF16)


def _inproj_even_kernel(*refs, rope):
    if rope:
        (x_ref, mod_ref, g_ref, w_ref, wqt_ref, wvt_ref,
         cos_ref, sin_ref, cost_ref, sint_ref, qt_ref, k_ref, vt_ref, rest_ref) = refs
    else:
        (x_ref, mod_ref, g_ref, w_ref, wqt_ref, wvt_ref, qt_ref, k_ref, vt_ref, rest_ref) = refs
    bw = qt_ref.shape[1]
    hb = _prenorm(x_ref, mod_ref, g_ref)
    pair = A_QK // 4

    def rotary(u, axis, cos, sin):
        idx = lax.broadcasted_iota(jnp.int32, u.shape, axis)
        partner = jnp.where((idx & pair) == 0, pltpu.roll(u, bw - pair, axis), pltpu.roll(u, pair, axis))
        return u * cos + partner * sin

    qt = _nt_dot(wqt_ref[...], hb)
    if rope:
        reps = bw // cost_ref.shape[0]
        qt = rotary(qt, 0, jnp.tile(cost_ref[...], (reps, 1)), jnp.tile(sint_ref[...], (reps, 1)))
    qt_ref[0] = (qt * (A_QK ** -0.5 * LOG2E)).astype(BF16)

    k = jnp.dot(hb, w_ref[:, bw:2 * bw], preferred_element_type=F32)
    if rope:
        reps = bw // cos_ref.shape[1]
        k = rotary(k, 1, jnp.tile(cos_ref[...], (1, reps)), jnp.tile(sin_ref[...], (1, reps)))
    k_ref[0] = k.astype(BF16)

    vt_ref[0] = _nt_dot(wvt_ref[...], hb).astype(BF16)

    n_rest = rest_ref.shape[1] // bw
    for c in range(n_rest):
        col = (3 + c) * bw
        rest_ref[:, c * bw:(c + 1) * bw] = jnp.dot(
            hb, w_ref[:, col:col + bw], preferred_element_type=F32).astype(BF16)


def _inproj_even(x2, mod3, mod_row_fn, g_pre, w_bf, seq, rope_tabs, tm):
    n, d = x2.shape
    bw = w_bf.shape[1] // 8
    nb = n // seq
    tps = seq // tm
    rope = rope_tabs is not None
    wqt = w_bf[:, 0:bw].T
    wvt = w_bf[:, 2 * bw:3 * bw].T
    const = lambda i: (0, 0)
    in_specs = [pl.BlockSpec((tm, d), lambda i: (i, 0)),
                pl.BlockSpec((1, 1, mod3.shape[2]), lambda i: (mod_row_fn(i // tps), 0, 0)),
                pl.BlockSpec((1, d), const),
                _resident(w_bf.shape), _resident(wqt.shape), _resident(wvt.shape)]
    args = [x2, mod3, g_pre.reshape(1, d), w_bf, wqt, wvt]
    if rope:
        cos_n, sin_n, cos_t, sin_t = rope_tabs
        in_specs += [pl.BlockSpec((tm, cos_n.shape[1]), lambda i: (i % tps, 0)),
                     pl.BlockSpec((tm, cos_n.shape[1]), lambda i: (i % tps, 0)),
                     pl.BlockSpec((cos_t.shape[0], tm), lambda i: (0, i % tps)),
                     pl.BlockSpec((cos_t.shape[0], tm), lambda i: (0, i % tps))]
        args += [cos_n, sin_n, cos_t, sin_t]
    out_specs = [pl.BlockSpec((1, bw, tm), lambda i: (i // tps, 0, i % tps)),
                 pl.BlockSpec((1, tm, bw), lambda i: (i // tps, i % tps, 0)),
                 pl.BlockSpec((1, bw, tm), lambda i: (i // tps, 0, i % tps)),
                 pl.BlockSpec((tm, 5 * bw), lambda i: (i, 0))]
    out_shape = [jax.ShapeDtypeStruct((nb, bw, seq), BF16),
                 jax.ShapeDtypeStruct((nb, seq, bw), BF16),
                 jax.ShapeDtypeStruct((nb, bw, seq), BF16),
                 jax.ShapeDtypeStruct((n, 5 * bw), BF16)]
    return pl.pallas_call(
        functools.partial(_inproj_even_kernel, rope=rope),
        grid=(n // tm,), in_specs=in_specs, out_specs=out_specs, out_shape=out_shape,
        compiler_params=_params(1), name="inproj_even",
    )(*args)


def _inproj_odd_kernel(x_ref, mod_ref, g_ref, w_ref, wqt_ref, wvt_ref, cs_ref,
                       qt_ref, k_ref, vt_ref, rest_ref, p_ref, pq_ref):
    bw = k_ref.shape[2]
    hb = _prenorm(x_ref, mod_ref, g_ref)

    def proj(c):
        return jnp.dot(hb, w_ref[:, c * bw:(c + 1) * bw], preferred_element_type=F32)

    qt_ref[0] = (_nt_dot(wqt_ref[...], hb) * (C_HEAD_DIM ** -0.5 * LOG2E)).astype(BF16)
    k_ref[0] = proj(1).astype(BF16)
    vt = _nt_dot(wvt_ref[...], hb).astype(BF16)
    for c in range(vt_ref.shape[1]):
        vt_ref[0, c] = vt[:, c * NA_VT_CHUNK:(c + 1) * NA_VT_CHUNK]
    rest_ref[:, 0:bw] = proj(3).astype(BF16)
    rest_ref[:, bw:2 * bw] = proj(5).astype(BF16)
    pq = jnp.dot(proj(4).astype(BF16), cs_ref[...], preferred_element_type=F32)
    p_ref[0] = pq[:, 0:bw].astype(BF16)
    pq_ref[0] = pq[:, bw:2 * bw].astype(BF16)


def _inproj_odd(x2, mod3, mod_row_fn, g_pre, w_bf, cs_bf, seq, tm):
    n, d = x2.shape
    bw = w_bf.shape[1] // 6
    nb = n // seq
    tps = seq // tm
    const = lambda i: (0, 0)
    wqt = w_bf[:, 0:bw].T
    wvt = w_bf[:, 2 * bw:3 * bw].T
    cpt = tm // NA_VT_CHUNK
    tok = pl.BlockSpec((1, tm, bw), lambda i: (i // tps, i % tps, 0))
    tok_shape = jax.ShapeDtypeStruct((nb, seq, bw), BF16)
    return pl.pallas_call(
        _inproj_odd_kernel,
        grid=(n // tm,),
        in_specs=[pl.BlockSpec((tm, d), lambda i: (i, 0)),
                  pl.BlockSpec((1, 1, mod3.shape[2]), lambda i: (mod_row_fn(i // tps), 0, 0)),
                  pl.BlockSpec((1, d), const),
                  _resident(w_bf.shape), _resident(wqt.shape), _resident(wvt.shape),
                  _resident(cs_bf.shape)],
        out_specs=[pl.BlockSpec((1, bw, tm), lambda i: (i // tps, 0, i % tps)),
                   tok,
                   pl.BlockSpec((1, cpt, bw, NA_VT_CHUNK), lambda i: (i // tps, i % tps, 0, 0)),
                   pl.BlockSpec((tm, 2 * bw), lambda i: (i, 0)), tok, tok],
        out_shape=[jax.ShapeDtypeStruct((nb, bw, seq), BF16),
                   tok_shape,
                   jax.ShapeDtypeStruct((nb, seq // NA_VT_CHUNK, bw, NA_VT_CHUNK), BF16),
                   jax.ShapeDtypeStruct((n, 2 * bw), BF16), tok_shape, tok_shape],
        compiler_params=_params(1), name="inproj_odd",
    )(x2, mod3, g_pre.reshape(1, d), w_bf, wqt, wvt, cs_bf)


def _diff_attn_kernel(*refs, has_main, tq, lam_init):
    if has_main:
        (lam_ref, sub_ref, qt_ref, kc_ref, vct_ref, ka_ref, vat_ref, ga_ref, o_ref) = refs
        key_refs = [(kc_ref, vct_ref), (ka_ref, vat_ref)]
    else:
        (lam_ref, sub_ref, qt_ref, kc_ref, vct_ref, ga_ref, o_ref) = refs
        key_refs = [(kc_ref, vct_ref)]
    chunks = []
    for k_ref, vt_ref in key_refs:
        for st in range(0, k_ref.shape[1], KEY_CHUNK):
            chunks.append((k_ref, vt_ref, st, min(KEY_CHUNK, k_ref.shape[1] - st)))
    lp = lam_ref[...]
    lam = (jnp.exp(jnp.sum(lp[0:1] * lp[1:2], axis=1, keepdims=True))
           - jnp.exp(jnp.sum(lp[2:3] * lp[3:4], axis=1, keepdims=True)) + lam_init)
    sq = min(tq, DIFF_SUB_Q)
    qf = qt_ref[0].astype(F32)
    rowid = lax.broadcasted_iota(jnp.int32, (qf.shape[0], sq), 0)
    sub = jnp.tile(sub_ref[...], (1, sq // LANES))

    def fold(v):
        return v.reshape(v.shape[0] // 8, 8, sq)

    items = [(j, hh, m) for j in range(tq // sq) for hh in range(2) for m in range(2)]
    work = [(t, ci) for t in range(len(items)) for ci in range(len(chunks))]
    qms, state, maps, pending = {}, {}, {}, []

    def issue(t, ci):
        j, hh, m = items[t]
        if ci == 0:
            lo = hh * A_V + m * A_QK
            qms[t] = jnp.where((rowid >= lo) & (rowid < lo + A_QK), qf[:, j * sq:(j + 1) * sq],
                               0.0).astype(BF16)
        k_ref, _, st, sz = chunks[ci]
        sc = jnp.dot(k_ref[0, st:st + sz, :], qms[t], preferred_element_type=F32)
        pending.append((t, ci, sc, jnp.max(jnp.max(fold(sc), axis=0), axis=0, keepdims=True)))

    def consume():
        t, ci, s, cm = pending.pop(0)
        hh = items[t][1]
        _, vt_ref, st, sz = chunks[ci]
        if ci == 0:
            state[t] = (jnp.full((1, sq), -jnp.inf, F32), jnp.zeros((8, sq), F32),
                        jnp.zeros((A_V, sq), F32))
        m_run, l, acc = state[t]
        m_new = jnp.maximum(m_run, cm)
        alpha = jnp.exp2(m_run - m_new)
        e = jnp.exp2(s - m_new)
        l = alpha * l + jnp.sum(fold(e), axis=0)
        acc = alpha * acc + jnp.dot(vt_ref[0, hh * A_V:(hh + 1) * A_V, st:st + sz], e.astype(BF16),
                                    preferred_element_type=F32)
        state[t] = (m_new, l, acc)
        if ci == len(chunks) - 1:
            del state[t], qms[t]
            maps[t] = acc * (1.0 / jnp.sum(l, axis=0, keepdims=True))

    for idx in range(len(work) + SCORE_AHEAD):
        if idx < len(work):
            issue(*work[idx])
        if idx >= SCORE_AHEAD:
            consume()
    for j in range(tq // sq):
        heads = []
        for hh in range(2):
            ot = maps[4 * j + 2 * hh] - lam * maps[4 * j + 2 * hh + 1]
            ot = ot * lax.rsqrt(jnp.mean(ot * ot, axis=0, keepdims=True) + EPS) * sub * (1.0 - lam_init)
            heads.append(ot)
        o = jnp.concatenate(heads, axis=0).T
        rows = slice(j * sq, (j + 1) * sq)
        o_ref[rows, :] = (o * _silu(ga_ref[rows, :].astype(F32))).astype(o_ref.dtype)


def _diff_attn(lam_p, subln, qt, kc, vct, ka, vat, rest, lam_init, tq):
    nb, bw, sq = qt.shape
    nq = sq // tq
    hp = HEAD_PAIR_W
    sub_b = jnp.broadcast_to(subln.reshape(A_V, 1), (A_V, LANES))
    in_specs = [pl.BlockSpec(lam_p.shape, lambda b, h, i: (0, 0)),
                pl.BlockSpec(sub_b.shape, lambda b, h, i: (0, 0)),
                pl.BlockSpec((1, hp, tq), lambda b, h, i: (b, h, i)),
                pl.BlockSpec((1, kc.shape[1], hp), lambda b, h, i: (b, 0, h)),
                pl.BlockSpec((1, hp, kc.shape[1]), lambda b, h, i: (b, h, 0))]
    args = [lam_p, sub_b, qt, kc, vct]
    if ka is not None:
        in_specs += [pl.BlockSpec((1, ka.shape[1], hp), lambda b, h, i: (b, 0, h)),
                     pl.BlockSpec((1, hp, ka.shape[1]), lambda b, h, i: (b, h, 0))]
        args += [ka, vat]
    in_specs.append(pl.BlockSpec((tq, hp), lambda b, h, i: (b * nq + i, h)))
    args.append(rest)
    return pl.pallas_call(
        functools.partial(_diff_attn_kernel, has_main=ka is not None, tq=tq, lam_init=lam_init),
        grid=(nb, bw // hp, nq),
        in_specs=in_specs,
        out_specs=pl.BlockSpec((tq, hp), lambda b, h, i: (b * nq + i, h)),
        out_shape=jax.ShapeDtypeStruct((nb * sq, bw), MIX_DTYPE),
        compiler_params=_params(3), name="diff_attn",
    )(*args)


NA_QROWS = 4
NA_KROWS = 12
NA_VT_CHUNK = 256
NA_GROUPS_PER_STEP = 8


def _softmax_pv(problems):
    tq = problems[0][0].shape[1]
    rowid = lax.broadcasted_iota(jnp.int32, problems[0][0].shape, 0)
    chains = [(p, hh) for p in range(len(problems)) for hh in range(2)]
    scores, exps, outs = {}, {}, {}

    def fold(v):
        return v.reshape(v.shape[0] // 8, 8, tq)

    def score(p, hh):
        qt, pieces, _ = problems[p]
        qm = jnp.where((rowid >= hh * C_HEAD_DIM) & (rowid < (hh + 1) * C_HEAD_DIM), qt.astype(F32),
                       0.0).astype(BF16)
        scores[p, hh] = [jnp.dot(k, qm, preferred_element_type=F32) + (0.0 if bias is None else bias[hh])
                         for k, _, bias in pieces]

    def weigh(c):
        ss = scores.pop(c)
        m = jnp.max(functools.reduce(jnp.maximum, [jnp.max(fold(s), axis=0) for s in ss]),
                    axis=0, keepdims=True)
        exps[c] = [jnp.exp2(s - m) for s in ss]

    def accumulate(p, hh):
        es = exps.pop((p, hh))
        l = jnp.sum(functools.reduce(jnp.add, [jnp.sum(fold(e), axis=0) for e in es]), axis=0, keepdims=True)
        pv = jnp.zeros((C_HEAD_DIM, tq), F32)
        for e, (_, vts, _) in zip(es, problems[p][1]):
            eb = e.astype(BF16)
            off = 0
            for vt in vts:
                pv = pv + jnp.dot(vt[hh * C_HEAD_DIM:(hh + 1) * C_HEAD_DIM, :], eb[off:off + vt.shape[1], :],
                                  preferred_element_type=F32)
                off += vt.shape[1]
        outs[p, hh] = pv * (1.0 / l)

    for idx in range(len(chains) + 2):
        if idx < len(chains):
            score(*chains[idx])
        if 1 <= idx <= len(chains):
            weigh(chains[idx - 1])
        if idx >= 2:
            accumulate(*chains[idx - 2])
    return [jnp.concatenate([outs[p, 0], outs[p, 1]], axis=0).T * _silu(problems[p][2].astype(F32))
            for p in range(len(problems))]


class _BiasOf:
    def __init__(self, ref):
        self.ref = ref

    def __getitem__(self, hh):
        return self.ref[0, hh]


def _na_kernel(qt_ref, k_ref, vt_ref, kc_ref, vct_ref, *rest, rows, n_sub):
    bias_refs, (gate_ref, o_ref) = rest[:n_sub], rest[n_sub:]
    tq = NA_QROWS * GRID_W
    nk = NA_KROWS * GRID_W
    ctx_piece = (kc_ref[0], [vct_ref[0, c] for c in range(vct_ref.shape[1])], None)
    problems = []
    for u in range(n_sub):
        g = pl.program_id(2) * n_sub + u
        base = jnp.clip(NA_QROWS * g - NA_KH // 2, 0, rows - NA_KROWS)
        start = pl.multiple_of(base * GRID_W, NA_VT_CHUNK)
        c0 = base // (NA_VT_CHUNK // GRID_W)
        kw = k_ref[0, pl.ds(start, nk), :]
        vts = [vt_ref[0, c0 + c] for c in range(nk // NA_VT_CHUNK)]
        qrows = slice(u * tq, (u + 1) * tq)
        problems.append((qt_ref[0, :, qrows], [(kw, vts, _BiasOf(bias_refs[u])), ctx_piece],
                         gate_ref[qrows, :]))
    for u, o in enumerate(_softmax_pv(problems)):
        o_ref[u * tq:(u + 1) * tq, :] = o.astype(o_ref.dtype)


def _na_case_tables(rows):
    rs = np.clip(np.arange(rows) - NA_KH // 2, 0, rows - NA_KH)
    n_groups = rows // NA_QROWS
    pats = []
    for g in range(n_groups):
        base = int(np.clip(NA_QROWS * g - NA_KH // 2, 0, rows - NA_KROWS))
        pat = np.full((NA_QROWS, NA_KROWS), 2 * NA_KH - 1, np.int32)
        for j in range(NA_QROWS):
            r = NA_QROWS * g + j
            for i in range(NA_KROWS):
                kr = base + i
                if rs[r] <= kr < rs[r] + NA_KH:
                    pat[j, i] = kr - r + NA_KH - 1
            assert (pat[j] != 2 * NA_KH - 1).sum() == NA_KH
        pats.append(pat)
    cases = np.stack([pats[0], pats[1], pats[-1]])
    for g in range(n_groups):
        want = 0 if g == 0 else (2 if g == n_groups - 1 else 1)
        assert (pats[g] == cases[want]).all()
    return cases


def _na_bias(rpb, rows):
    h = rpb.shape[0]
    cq = np.arange(GRID_W)
    cs = np.clip(cq - NA_KW // 2, 0, GRID_W - NA_KW)
    kcol = np.arange(GRID_W)
    col_ok = ((kcol[None, :] >= cs[:, None]) & (kcol[None, :] < cs[:, None] + NA_KW)).T
    col_off = np.clip(kcol[None, :] - cq[:, None] + NA_KW - 1, 0, 2 * NA_KW - 2).T
    pick = ((col_off[None] == np.arange(2 * NA_KW - 1)[:, None, None]) & col_ok[None]).astype(np.float32)
    toep = jnp.dot((rpb * LOG2E).reshape(-1, 2 * NA_KW - 1), pick.reshape(2 * NA_KW - 1, -1),
                   precision=lax.Precision.HIGHEST).reshape(h, -1, GRID_W, GRID_W)
    toep = toep + np.where(col_ok, 0.0, NEG_BIG).astype(np.float32)
    toep = jnp.concatenate([toep, jnp.full((h, 1, GRID_W, GRID_W), NEG_BIG, F32)], axis=1)
    cases = _na_case_tables(rows)
    n_case, qr, kr = cases.shape
    return pl.pallas_call(
        functools.partial(_na_bias_kernel, cases=cases),
        grid=(h,),
        in_specs=[pl.BlockSpec((1,) + toep.shape[1:], lambda i: (i, 0, 0, 0))],
        out_specs=pl.BlockSpec((n_case, 1, kr * GRID_W, qr * GRID_W), lambda i: (0, i, 0, 0)),
        out_shape=jax.ShapeDtypeStruct((n_case, h, kr * GRID_W, qr * GRID_W), F32),
        compiler_params=_params(1), name="nbr_bias",
    )(toep)


def _na_bias_kernel(toep_ref, o_ref, *, cases):
    for c in range(cases.shape[0]):
        for i in range(cases.shape[2]):
            o_ref[c, 0, i * GRID_W:(i + 1) * GRID_W, :] = jnp.concatenate(
                [toep_ref[0, int(b)] for b in cases[c, :, i]], axis=1)


def _na_attn(qt, k, vt, kc, vct, bias, rest):
    nb, bw, s = qt.shape
    hp = HEAD_PAIR_W
    tq = NA_QROWS * GRID_W
    ng = s // tq
    n_sub = NA_GROUPS_PER_STEP
    nsteps = ng // n_sub
    l = kc.shape[1]

    def bias_spec(u):
        def index(b, h, i):
            g = i * n_sub + u
            return (jnp.where(g == 0, 0, jnp.where(g == ng - 1, 2, 1)), h, 0, 0)
        return pl.BlockSpec((1, 2, NA_KROWS * GRID_W, tq), index)

    return pl.pallas_call(
        functools.partial(_na_kernel, rows=s // GRID_W, n_sub=n_sub),
        grid=(nb, bw // hp, nsteps),
        in_specs=[pl.BlockSpec((1, hp, n_sub * tq), lambda b, h, i: (b, h, i)),
                  pl.BlockSpec((1, s, hp), lambda b, h, i: (b, 0, h)),
                  pl.BlockSpec((1, vt.shape[1], hp, NA_VT_CHUNK), lambda b, h, i: (b, 0, h, 0)),
                  pl.BlockSpec((1, l, hp), lambda b, h, i: (b, 0, h)),
                  pl.BlockSpec((1, vct.shape[1], hp, NA_VT_CHUNK), lambda b, h, i: (b, 0, h, 0))]
                 + [bias_spec(u) for u in range(n_sub)]
                 + [pl.BlockSpec((n_sub * tq, hp), lambda b, h, i: (b * nsteps + i, h))],
        out_specs=pl.BlockSpec((n_sub * tq, hp), lambda b, h, i: (b * nsteps + i, h)),
        out_shape=jax.ShapeDtypeStruct((nb * s, bw), MIX_DTYPE),
        compiler_params=_params(3), name="nbr_attn",
    )(qt, k, vt, kc, vct, *([bias] * n_sub), rest)


def _dense_attn_kernel(qt_ref, k_ref, vt_ref, gate_ref, o_ref):
    piece = (k_ref[0], [vt_ref[0, c] for c in range(vt_ref.shape[1])], None)
    o_ref[...] = _softmax_pv([(qt_ref[0], [piece], gate_ref[...])])[0].astype(o_ref.dtype)


def _dense_attn(qt, k, vt, rest):
    nb, bw, l = qt.shape
    hp = HEAD_PAIR_W
    return pl.pallas_call(
        _dense_attn_kernel,
        grid=(nb, bw // hp),
        in_specs=[pl.BlockSpec((1, hp, l), lambda b, h: (b, h, 0)),
                  pl.BlockSpec((1, l, hp), lambda b, h: (b, 0, h)),
                  pl.BlockSpec((1, vt.shape[1], hp, NA_VT_CHUNK), lambda b, h: (b, 0, h, 0)),
                  pl.BlockSpec((l, hp), lambda b, h: (b, h))],
        out_specs=pl.BlockSpec((l, hp), lambda b, h: (b, h)),
        out_shape=jax.ShapeDtypeStruct((nb * l, bw), MIX_DTYPE),
        compiler_params=_params(2), name="ctx_dense_attn",
    )(qt, k, vt, rest)


def _dft_kernel(c_ref, s_ref, p_ref, q_ref, gate_ref, o_ref):
    y = (jnp.dot(c_ref[...], p_ref[0], preferred_element_type=F32)
         + jnp.dot(s_ref[...], q_ref[0], preferred_element_type=F32))
    o_ref[0] = (y * _silu(gate_ref[0].astype(F32))).astype(o_ref.dtype)


def _dft_tables(n):
    idx = np.arange(n)
    ang = ((idx[:, None] * idx[None, :]) % n) * (2.0 * np.pi / n)
    scale = n ** -0.5
    return _bf16_table(np.cos(ang) * scale), _bf16_table(np.sin(ang) * scale)


def _dft_pos_kernel(m1_ref, m2_ref, g_ref, vr_ref, vi_ref, gate_ref, o_ref, a_s, *, nt):
    r = vr_ref.shape[1]
    for c in range(GRID_W // nt):
        cols = slice(c * nt, (c + 1) * nt)
        vrt = jnp.swapaxes(vr_ref[0, :, cols, :].astype(F32), 0, 1)
        vit = jnp.swapaxes(vi_ref[0, :, cols, :].astype(F32), 0, 1)
        outs = []
        for j in range(nt):
            outs.append(jnp.dot(m1_ref[...], vrt[j].astype(BF16), preferred_element_type=F32)
                        + jnp.dot(m2_ref[...], vit[j].astype(BF16), preferred_element_type=F32))
        at = jnp.swapaxes(jnp.stack(outs, axis=0), 0, 1)
        a_s[:, :, cols, :] = at.reshape(2, r, nt, LANES)
    k1t = min(nt, r)
    for c in range(r // k1t):
        outs = []
        for j in range(c * k1t, (c + 1) * k1t):
            a = jnp.concatenate([a_s[0, j], a_s[1, j]], axis=0).astype(BF16)
            outs.append(jnp.dot(g_ref[j], a, preferred_element_type=F32))
        ks = slice(c * k1t, (c + 1) * k1t)
        y = jnp.swapaxes(jnp.stack(outs, axis=0), 0, 1)
        o_ref[0, :, ks, :] = (y * _silu(gate_ref[0, :, ks, :].astype(F32))).astype(o_ref.dtype)


def _dft_factored_tables(n):
    r = n // GRID_W
    i = np.arange(r)
    ang = ((i[:, None] * i[None, :]) % r) * (2.0 * np.pi / r)
    c, s = np.cos(ang) * r ** -0.5, np.sin(ang) * r ** -0.5
    m1 = np.concatenate([c, -s], axis=0)
    m2 = np.concatenate([s, c], axis=0)
    k = i[:, None, None] + r * np.arange(GRID_W)[None, :, None]
    th = ((k * np.arange(GRID_W)[None, None, :]) % n) * (2.0 * np.pi / n)
    g = np.concatenate([np.cos(th), np.sin(th)], axis=2) * GRID_W ** -0.5
    return _bf16_table(m1), _bf16_table(m2), _bf16_table(g)


def _dft_positions_factored(tabs, p, q, rest):
    m1, m2, g = tabs
    nb, n, bw = p.shape
    r = n // GRID_W
    tok = pl.BlockSpec((1, r, GRID_W, LANES), lambda b, l: (b, 0, 0, l))
    y = pl.pallas_call(
        functools.partial(_dft_pos_kernel, nt=32),
        grid=(nb, bw // LANES),
        in_specs=[pl.BlockSpec(m1.shape, lambda b, l: (0, 0)),
                  pl.BlockSpec(m2.shape, lambda b, l: (0, 0)),
                  pl.BlockSpec(g.shape, lambda b, l: (0, 0, 0)),
                  tok, tok,
                  pl.BlockSpec((1, GRID_W, r, LANES), lambda b, l: (b, 0, 0, bw // LANES + l))],
        out_specs=pl.BlockSpec((1, GRID_W, r, LANES), lambda b, l: (b, 0, 0, l)),
        out_shape=jax.ShapeDtypeStruct((nb, GRID_W, r, bw), MIX_DTYPE),
        scratch_shapes=[pltpu.VMEM((2, r, GRID_W, LANES), F32)],
        compiler_params=_params(2), name="dft_positions_factored",
    )(m1, m2, g, p.reshape(nb, r, GRID_W, bw), q.reshape(nb, r, GRID_W, bw),
      rest.reshape(nb, GRID_W, r, 2 * bw))
    return y.reshape(nb * n, bw)


def _dft_positions(cn, sn, p, q, rest, tmo):
    nb, n, bw = p.shape
    return pl.pallas_call(
        _dft_kernel,
        grid=(n // tmo, nb),
        in_specs=[pl.BlockSpec((tmo, n), lambda i, b: (i, 0)),
                  pl.BlockSpec((tmo, n), lambda i, b: (i, 0)),
                  pl.BlockSpec((1, n, bw), lambda i, b: (b, 0, 0)),
                  pl.BlockSpec((1, n, bw), lambda i, b: (b, 0, 0)),
                  pl.BlockSpec((1, tmo, bw), lambda i, b: (b, i, 1))],
        out_specs=pl.BlockSpec((1, tmo, bw), lambda i, b: (b, i, 0)),
        out_shape=jax.ShapeDtypeStruct((nb, n, bw), MIX_DTYPE),
        compiler_params=_params(2), name="dft_positions",
    )(cn, sn, p, q, rest.reshape(nb, n, 2 * bw))


def _channel_dft_matrix():
    j = np.arange(D_GROUP_W)
    ang = 2.0 * np.pi * ((j[:, None] * j[None, :]) % D_GROUP_W) / D_GROUP_W
    eye = np.eye(D_GROUPS)
    cc = np.kron(eye, np.cos(ang) * D_GROUP_W ** -0.5)
    sc = np.kron(eye, np.sin(ang) * D_GROUP_W ** -0.5)
    return _bf16_table(np.concatenate([cc, -sc], axis=1))


def _post_kernel(*refs, even, tm, seq):
    if even:
        (x_ref, a_ref, bb_ref, cb_ref, ub_ref, gb_ref, cbp_ref, ubp_ref, cbn_ref, ubn_ref,
         cw_ref, mod_ref, g_ref, w_ref, o_ref) = refs
    else:
        (x_ref, a_ref, y_ref, mod_ref, g_ref, w_ref, o_ref) = refs
    d = x_ref.shape[-1]
    bw = a_ref.shape[-1]
    if even:
        i = pl.program_id(0)
        t0 = (i * tm) % seq
        z = cb_ref[...].astype(F32) * ub_ref[...].astype(F32)
        zp = cbp_ref[7:8, :].astype(F32) * ubp_ref[7:8, :].astype(F32) * jnp.where(t0 == 0, 0.0, 1.0)
        zn = cbn_ref[0:1, :].astype(F32) * ubn_ref[0:1, :].astype(F32) * jnp.where(t0 + tm == seq, 0.0, 1.0)
        rid = lax.broadcasted_iota(jnp.int32, z.shape, 0)
        z_prev = jnp.where(rid == 0, zp, pltpu.roll(z, 1, 0))
        z_next = jnp.where(rid == tm - 1, zn, pltpu.roll(z, tm - 1, 0))
        cw = cw_ref[...]
        y = cw[0:1] * z_prev + cw[1:2] * z + cw[2:3] * z_next
        second = bb_ref[...].astype(F32) * y * _silu(gb_ref[...].astype(F32))
    else:
        second = y_ref[...]
    yo = (jnp.dot(a_ref[...].astype(BF16), w_ref[0:bw, :], preferred_element_type=F32)
          + jnp.dot(second.astype(BF16), w_ref[bw:2 * bw, :], preferred_element_type=F32))
    nrm = yo * lax.rsqrt(jnp.mean(yo * yo, axis=-1, keepdims=True) + EPS) * g_ref[...]
    o_ref[...] = x_ref[...] + mod_ref[0][:, 2 * d:3 * d] * nrm


def _post(x2, a, second, rest, conv_w, mod3, mod_row_fn, g_post, w_out_bf, seq, tm, even):
    n, d = x2.shape
    bw = a.shape[1]
    tps = seq // tm
    const = lambda i: (0, 0)
    row = lambda i: (i, 0)
    in_specs = [pl.BlockSpec((tm, d), row), pl.BlockSpec((tm, bw), row)]
    args = [x2, a]
    if even:
        nblk8 = n // 8
        for c in (1, 2, 3, 4):
            in_specs.append(pl.BlockSpec((tm, bw), lambda i, c=c: (i, c)))
            args.append(rest)
        prev = lambda i, c: (jnp.maximum(i * (tm // 8) - 1, 0), c)
        nxt = lambda i, c: (jnp.minimum((i + 1) * (tm // 8), nblk8 - 1), c)
        for fn in (prev, nxt):
            for c in (2, 3):
                in_specs.append(pl.BlockSpec((8, bw), functools.partial(fn, c=c)))
                args.append(rest)
        in_specs.append(pl.BlockSpec(conv_w.shape, const))
        args.append(conv_w)
    else:
        in_specs.append(pl.BlockSpec((tm, bw), row))
        args.append(second)
    in_specs += [pl.BlockSpec((1, 1, mod3.shape[2]), lambda i: (mod_row_fn(i // tps), 0, 0)),
                 pl.BlockSpec((1, d), const),
                 _resident(w_out_bf.shape)]
    args += [mod3, g_post.reshape(1, d), w_out_bf]
    return pl.pallas_call(
        functools.partial(_post_kernel, even=even, tm=tm, seq=seq),
        grid=(n // tm,), in_specs=in_specs,
        out_specs=pl.BlockSpec((tm, d), row),
        out_shape=jax.ShapeDtypeStruct((n, d), F32),
        compiler_params=_params(1), name="post_even" if even else "post_odd",
    )(*args)


def _rope_tables(seq):
    t = np.arange(seq)
    row = (t // GRID_W).astype(np.float64)
    col = (t % GRID_W).astype(np.float64)
    nf = A_QK // 4
    inv_freq = ROPE_BASE ** (-np.arange(nf, dtype=np.float64) / nf)
    p = np.arange(A_QK)
    f_idx = p % nf
    use_col = (p % A_QK) >= A_QK // 2
    sign = np.where((p % (2 * nf)) < nf, -1.0, 1.0)
    pos = np.where(use_col[None, :], col[:, None], row[:, None])
    ang = pos * inv_freq[f_idx][None, :]
    cos = np.cos(ang).astype(np.float32)
    sin = (np.sin(ang) * sign[None, :]).astype(np.float32)
    reps = LANES // A_QK
    return (jnp.asarray(np.tile(cos, (1, reps))), jnp.asarray(np.tile(sin, (1, reps))),
            jnp.asarray(np.ascontiguousarray(cos.T)), jnp.asarray(np.ascontiguousarray(sin.T)))


def _lambda_init(layer):
    return 0.8 - 0.6 * math.exp(-0.3 * layer)


def kernel(x, c, ctx, c_ctx, w_mod, b_mod, norm_pre, norm_post, w_in_even, lam_a, subln_a, conv_b,
           w_in_odd, rpb_c, w_out):
    nb, seq, d = x.shape
    lctx = ctx.shape[1]
    depth = w_mod.shape[0]
    assert seq % (NA_QROWS * GRID_W * NA_GROUPS_PER_STEP) == 0 and seq // GRID_W >= NA_KROWS
    assert lctx % LANES == 0 and seq % KEY_CHUNK == 0 and nb < 16

    pad = jnp.zeros((16 - nb - 1, d), F32)
    cc = jnp.concatenate([c, c_ctx[None, :], pad], axis=0)
    mods = _mod_all(cc, w_mod, b_mod)

    rope_tabs = _rope_tables(seq)
    cs_bf = _channel_dft_matrix()
    dft_tabs = _dft_factored_tables(seq)
    cn_c, sn_c = _dft_tables(lctx)

    tm_x = 1024
    tm_c = lctx
    x_row = lambda b: b
    c_row = lambda b: nb

    x2 = x.reshape(nb * seq, d)
    c2 = ctx.reshape(nb * lctx, d)
    for l in range(depth):
        need_ctx = l < depth - 1
        j = l // 2
        mod3 = mods[l].reshape(16, 1, 3 * d)
        w_out_bf = w_out[l].astype(BF16)
        if l % 2 == 0:
            w_bf = w_in_even[j].astype(BF16)
            li = _lambda_init(l)
            qt, k, vt, rest = _inproj_even(x2, mod3, x_row, norm_pre[l], w_bf, seq, rope_tabs, tm_x)
            qct, kc, vct, rest_c = _inproj_even(c2, mod3, c_row, norm_pre[l], w_bf, lctx, None, tm_c)
            a = _diff_attn(lam_a[j], subln_a[j], qt, kc, vct, k, vt, rest, li, 512)
            x2n = _post(x2, a, None, rest, conv_b[j], mod3, x_row, norm_post[l], w_out_bf, seq, tm_x, True)
            if need_ctx:
                ac = _diff_attn(lam_a[j], subln_a[j], qct, kc, vct, None, None, rest_c, li, lctx)
                c2 = _post(c2, ac, None, rest_c, conv_b[j], mod3, c_row, norm_post[l], w_out_bf,
                           lctx, tm_c, True)
            x2 = x2n
        else:
            w_bf = w_in_odd[j].astype(BF16)
            q, k, v, rest, p, pq = _inproj_odd(x2, mod3, x_row, norm_pre[l], w_bf, cs_bf, seq, tm_x)
            qc, kc, vc, rest_c, p_c, pq_c = _inproj_odd(c2, mod3, c_row, norm_pre[l], w_bf, cs_bf, lctx, tm_c)
            bias = _na_bias(rpb_c[j], seq // GRID_W)
            a = _na_attn(q, k, v, kc, vc, bias, rest)
            y = _dft_positions_factored(dft_tabs, p, pq, rest)
            x2n = _post(x2, a, y, rest, None, mod3, x_row, norm_post[l], w_out_bf, seq, tm_x, False)
            if need_ctx:
                ac = _dense_attn(qc, kc, vc, rest_c)
                yc = _dft_positions(cn_c, sn_c, p_c, pq_c, rest_c, lctx).reshape(nb * lctx, -1)
                c2 = _post(c2, ac, yc, rest_c, None, mod3, c_row, norm_post[l], w_out_bf, lctx, tm_c, False)
            x2 = x2n
    return x2.reshape(nb, seq, d)
```
